```python
import jax, jax.numpy as jnp
from jax import lax
import numpy as np

D_MODEL = 1024
BATCH = 4
SEQ = 4096
DEPTH = 2
DEC_BATCH = 128
DEC_SEQ = 4
PAST_LEN = 8192
PAGE_SIZE = 128

N_META = 16
MLA_HEADS = 16
Q_LORA = 256
KV_LORA = 256
QK_NOPE = 64
QK_ROPE = 32
V_HEAD = 64
MLA_SCALE = (QK_NOPE + QK_ROPE) ** -0.5
ROPE_BASE = 10000.0
Q_BLOCK = 128
GLA_HEADS = 4
GLA_DK = D_MODEL // 2 // GLA_HEADS
GLA_DV = D_MODEL // GLA_HEADS
GLA_SCALE = GLA_DK ** -0.5
GATE_RANK = 16
GATE_TAU = 16.0
CHUNK = 64
D_FF = 4 * D_MODEL
EPS = 1e-6
N_MLA_LAYERS = (DEPTH + 1) // 2
N_GLA_LAYERS = DEPTH // 2

kernel_name = 'hybrid_mla_gla_decoder_step'


def rmsnorm(x, w):
    xf = x.astype(jnp.float32)
    y = xf * lax.rsqrt(jnp.mean(xf * xf, axis=-1, keepdims=True) + EPS)
    return (y * w.astype(jnp.float32)).astype(x.dtype)


def rope(x, pos):
    half = QK_ROPE // 2
    inv = ROPE_BASE ** (-jnp.arange(half, dtype=jnp.float32) / half)
    ang = pos.astype(jnp.float32)[:, None] * inv[None, :]
    ang = ang.reshape((ang.shape[0],) + (1,) * (x.ndim - 3) + (half,))
    cos, sin = jnp.cos(ang), jnp.sin(ang)
    x1, x2 = jnp.split(x.astype(jnp.float32), 2, axis=-1)
    return jnp.concatenate([x1 * cos - x2 * sin, x1 * sin + x2 * cos], axis=-1).astype(x.dtype)


def sq_relu_mlp(h, w_up, w_down):
    return jnp.square(jax.nn.relu(h @ w_up)) @ w_down


def mla_project(h, pos, w_in, q_norm, w_uq, kv_norm, w_uk):
    B, T, _ = h.shape
    a = h @ w_in
    cq, ckv, kpe = jnp.split(a, [Q_LORA, Q_LORA + KV_LORA], axis=-1)
    q = (rmsnorm(cq, q_norm) @ w_uq).reshape(B, T, MLA_HEADS, QK_NOPE + QK_ROPE)
    q_nope, q_pe = q[..., :QK_NOPE], q[..., QK_NOPE:]
    q_lat = jnp.einsum('bthn,chn->bthc', q_nope, w_uk)
    return q_lat, rope(q_pe, pos), rmsnorm(ckv, kv_norm), rope(kpe, pos)


def mla_scores(q_lat, q_pe, ckv, kpe):
    s = jnp.einsum('bqhc,bkc->bhqk', q_lat, ckv) + jnp.einsum('bqhr,bkr->bhqk', q_pe, kpe)
    return s.astype(jnp.float32) * MLA_SCALE


def mla_out(o_lat, w_uv, w_o):
    B, T = o_lat.shape[:2]
    o = jnp.einsum('bthc,chv->bthv', o_lat, w_uv).reshape(B, T, MLA_HEADS * V_HEAD)
    return o @ w_o


def mla_prompt(h, pos, w_in, q_norm, w_uq, kv_norm, w_uk, w_uv, w_o):
    q_lat, q_pe, ckv, kpe = mla_project(h, pos, w_in, q_norm, w_uq, kv_norm, w_uk)
    B, T = h.shape[:2]
    nqb = -(-T // Q_BLOCK)
    tq = nqb * Q_BLOCK

    def blocks(a):
        a = jnp.pad(a, ((0, 0), (0, tq - T)) + ((0, 0),) * (a.ndim - 2))
        return jnp.moveaxis(a.reshape((B, nqb, Q_BLOCK) + a.shape[2:]), 1, 0)

    qpos = jnp.arange(tq, dtype=jnp.int32).reshape(nqb, Q_BLOCK)
    kpos = jnp.arange(T, dtype=jnp.int32)

    def attend(args):
        ql, qp, qpos_b = args
        s = mla_scores(ql, qp, ckv, kpe)
        s = jnp.where(kpos[None, :] <= qpos_b[:, None], s, -jnp.inf)
        p = jax.nn.softmax(s, axis=-1).astype(ckv.dtype)
        return jnp.einsum('bhqk,bkc->bqhc', p, ckv)

    o_lat = lax.map(attend, (blocks(q_lat), blocks(q_pe), qpos))
    o_lat = jnp.moveaxis(o_lat, 0, 1).reshape(B, tq, MLA_HEADS, KV_LORA)[:, :T]
    return mla_out(o_lat, w_uv, w_o), ckv, kpe


def mla_sample(h, pos, cache_ckv, cache_kpe, layer, page_table, w_in, q_norm, w_uq, kv_norm, w_uk, w_uv, w_o):
    q_lat, q_pe, ckv_new, kpe_new = mla_project(h, pos, w_in, q_norm, w_uq, kv_norm, w_uk)
    DB, S = h.shape[:2]
    ckv_past = cache_ckv[layer, page_table].reshape(DB, -1, KV_LORA)
    kpe_past = cache_kpe[layer, page_table].reshape(DB, -1, QK_ROPE)
    P = ckv_past.shape[1]
    s_past = mla_scores(q_lat, q_pe, ckv_past, kpe_past)
    s_new = mla_scores(q_lat, q_pe, ckv_new, kpe_new)
    s_new = jnp.where(jnp.tril(jnp.ones((S, S), dtype=bool)), s_new, -jnp.inf)
    p = jax.nn.softmax(jnp.concatenate([s_past, s_new], axis=-1), axis=-1).astype(h.dtype)
    o_lat = (jnp.einsum('bhqk,bkc->bqhc', p[..., :P], ckv_past)
             + jnp.einsum('bhqk,bkc->bqhc', p[..., P:], ckv_new))
    return mla_out(o_lat, w_uv, w_o), ckv_new, kpe_new


def gla_recurrence(q, k, v, log_a, s0, lead):
    B, T, H, _ = q.shape
    total = lead + T
    nc = -(-total // CHUNK)
    pad = ((0, 0), (lead, nc * CHUNK - total), (0, 0), (0, 0))

    def chunks(t):
        t = jnp.pad(t.astype(jnp.float32), pad)
        return t.reshape(B, nc, CHUNK, H, t.shape[-1]).transpose(1, 0, 3, 2, 4)

    qc, kc, vc, gc = chunks(q), chunks(k), chunks(v), chunks(log_a)
    b = jnp.cumsum(gc, axis=3)
    b_last = b[:, :, :, -1:, :]
    q_in = qc * jnp.exp(b)
    k_in = kc * jnp.exp(-b)
    k_dec = kc * jnp.exp(b_last - b)
    causal = jnp.tril(jnp.ones((CHUNK, CHUNK), dtype=bool))
    A = jnp.where(causal, jnp.einsum('nbhid,nbhjd->nbhij', q_in, k_in), 0.0)
    o_intra = jnp.einsum('nbhij,nbhjv->nbhiv', A, vc)

    def step(s, xs):
        q_in_c, k_dec_c, v_c, bl = xs
        o_inter = jnp.einsum('bhid,bhdv->bhiv', q_in_c, s)
        s = jnp.exp(bl)[..., 0, :, None] * s + jnp.einsum('bhjd,bhjv->bhdv', k_dec_c, v_c)
        return s, o_inter

    s_fin, o_inter = lax.scan(step, s0.astype(jnp.float32), (q_in, k_dec, vc, b_last))
    o = (o_intra + o_inter).transpose(1, 0, 3, 2, 4).reshape(B, nc * CHUNK, H, -1)[:, lead:lead + T]
    return o, s_fin


def gla_mixer(h, s0, lead, w_in, w_gate, b_gate, g_norm, w_o):
    B, T, _ = h.shape
    hk = GLA_HEADS * GLA_DK
    hv = GLA_HEADS * GLA_DV
    a = h @ w_in
    q, k, v, r, gd = jnp.split(a, [hk, 2 * hk, 2 * hk + hv, 2 * hk + 2 * hv], axis=-1)

    def heads(t):
        return t.reshape(B, T, GLA_HEADS, -1)

    log_a = jax.nn.log_sigmoid((gd @ w_gate + b_gate).astype(jnp.float32)) / GATE_TAU
    o, s = gla_recurrence(heads(q) * GLA_SCALE, heads(k), heads(v), heads(log_a), s0, lead)
    o = rmsnorm(o.astype(h.dtype), g_norm) * jax.nn.silu(heads(r))
    return o.reshape(B, T, hv) @ w_o, s.astype(h.dtype)


def setup_inputs(seed: int = 0) -> dict:
    key = jax.random.key(seed)
    ks = jax.random.split(key, 32)
    f32 = jnp.float32

    def nrm(k, shape, scale=1.0):
        return jax.random.normal(k, shape, f32) * scale

    def gain(k, shape):
        return 1.0 + 0.01 * jax.random.normal(k, shape, f32)

    n_pages = PAST_LEN // PAGE_SIZE
    n_used = DEC_BATCH * n_pages
    n_pool = (n_used * 5) // 4
    page_table = jax.random.permutation(ks[5], n_pool)[:n_used].reshape(DEC_BATCH, n_pages).astype(jnp.int32)
    hk = GLA_HEADS * GLA_DK
    hv = GLA_HEADS * GLA_DV
    return {
        'x_prompt': nrm(ks[0], (BATCH, SEQ, D_MODEL)),
        'x_sample': nrm(ks[1], (DEC_BATCH, DEC_SEQ, D_MODEL)),
        'cache_ckv': nrm(ks[2], (N_MLA_LAYERS, n_pool, PAGE_SIZE, KV_LORA)),
        'cache_kpe': nrm(ks[3], (N_MLA_LAYERS, n_pool, PAGE_SIZE, QK_ROPE)),
        'state_gla': nrm(ks[4], (N_GLA_LAYERS, DEC_BATCH, GLA_HEADS, GLA_DK, GLA_DV), 0.5),
        'page_table': page_table,
        'meta_tokens': nrm(ks[6], (N_META, D_MODEL)),
        'norm_pre_mix': gain(ks[7], (DEPTH, D_MODEL)),
        'norm_post_mix': gain(ks[8], (DEPTH, D_MODEL)),
        'norm_pre_mlp': gain(ks[9], (DEPTH, D_MODEL)),
        'norm_post_mlp': gain(ks[10], (DEPTH, D_MODEL)),
        'mla_w_in': nrm(ks[11], (N_MLA_LAYERS, D_MODEL, Q_LORA + KV_LORA + QK_ROPE), D_MODEL ** -0.5),
        'mla_q_norm': gain(ks[12], (N_MLA_LAYERS, Q_LORA)),
        'mla_w_uq': nrm(ks[13], (N_MLA_LAYERS, Q_LORA, MLA_HEADS * (QK_NOPE + QK_ROPE)), Q_LORA ** -0.5),
        'mla_kv_norm': gain(ks[14], (N_MLA_LAYERS, KV_LORA)),
        'mla_w_uk': nrm(ks[15], (N_MLA_LAYERS, KV_LORA, MLA_HEADS, QK_NOPE), KV_LORA ** -0.5),
        'mla_w_uv': nrm(ks[16], (N_MLA_LAYERS, KV_LORA, MLA_HEADS, V_HEAD), KV_LORA ** -0.5),
        'mla_w_o': nrm(ks[17], (N_MLA_LAYERS, MLA_HEADS * V_HEAD, D_MODEL), (MLA_HEADS * V_HEAD) ** -0.5),
        'gla_w_in': nrm(ks[18], (N_GLA_LAYERS, D_MODEL, 2 * hk + 2 * hv + GATE_RANK), D_MODEL ** -0.5),
        'gla_w_gate': nrm(ks[19], (N_GLA_LAYERS, GATE_RANK, hk), GATE_RANK ** -0.5),
        'gla_b_gate': nrm(ks[20], (N_GLA_LAYERS, hk), 0.1),
        'gla_norm': gain(ks[21], (N_GLA_LAYERS, GLA_DV)),
        'gla_w_o': nrm(ks[22], (N_GLA_LAYERS, hv, D_MODEL), hv ** -0.5),
        'mlp_w_up': nrm(ks[23], (DEPTH, D_MODEL, D_FF), D_MODEL ** -0.5),
        'mlp_w_down': nrm(ks[24], (DEPTH, D_FF, D_MODEL), D_FF ** -0.5),
    }


def reference(x_prompt, x_sample, cache_ckv, cache_kpe, state_gla, page_table, meta_tokens,
              norm_pre_mix, norm_post_mix, norm_pre_mlp, norm_post_mlp,
              mla_w_in, mla_q_norm, mla_w_uq, mla_kv_norm, mla_w_uk, mla_w_uv, mla_w_o,
              gla_w_in, gla_w_gate, gla_b_gate, gla_norm, gla_w_o,
              mlp_w_up, mlp_w_down):
    B = x_prompt.shape[0]
    meta = jnp.broadcast_to(meta_tokens[None].astype(x_prompt.dtype), (B, N_META, D_MODEL))
    xp = jnp.concatenate([meta, x_prompt], axis=1)
    xs = x_sample
    past_len = page_table.shape[1] * cache_ckv.shape[2]
    pos_p = jnp.arange(xp.shape[1], dtype=jnp.int32)
    pos_s = past_len + jnp.arange(xs.shape[1], dtype=jnp.int32)

    ckv_p_l, kpe_p_l, ckv_s_l, kpe_s_l, gla_p_l, gla_s_l = [], [], [], [], [], []
    for i in range(DEPTH):
        j = i // 2
        hp = rmsnorm(xp, norm_pre_mix[i])
        hs = rmsnorm(xs, norm_pre_mix[i])
        if i % 2 == 0:
            w = (mla_w_in[j], mla_q_norm[j], mla_w_uq[j], mla_kv_norm[j], mla_w_uk[j], mla_w_uv[j], mla_w_o[j])
            mp, ckv_p, kpe_p = mla_prompt(hp, pos_p, *w)
            ms, ckv_s, kpe_s = mla_sample(hs, pos_s, cache_ckv, cache_kpe, j, page_table, *w)
            ckv_p_l.append(ckv_p)
            kpe_p_l.append(kpe_p)
            ckv_s_l.append(ckv_s)
            kpe_s_l.append(kpe_s)
        else:
            w = (gla_w_in[j], gla_w_gate[j], gla_b_gate[j], gla_norm[j], gla_w_o[j])
            s0 = jnp.zeros((B, GLA_HEADS, GLA_DK, GLA_DV), dtype=jnp.float32)
            mp, sp = gla_mixer(hp, s0, CHUNK - N_META, *w)
            ms, ss = gla_mixer(hs, state_gla[j], 0, *w)
            gla_p_l.append(sp)
            gla_s_l.append(ss)
        xp = xp + rmsnorm(mp, norm_post_mix[i])
        xs = xs + rmsnorm(ms, norm_post_mix[i])
        xp = xp + rmsnorm(sq_relu_mlp(rmsnorm(xp, norm_pre_mlp[i]), mlp_w_up[i], mlp_w_down[i]), norm_post_mlp[i])
        xs = xs + rmsnorm(sq_relu_mlp(rmsnorm(xs, norm_pre_mlp[i]), mlp_w_up[i], mlp_w_down[i]), norm_post_mlp[i])

    y_prompt = xp[:, N_META:]
    y_sample = xs
    new_ckv_prompt = jnp.stack(ckv_p_l)
    new_kpe_prompt = jnp.stack(kpe_p_l)
    new_ckv_sample = jnp.stack(ckv_s_l)
    new_kpe_sample = jnp.stack(kpe_s_l)
    new_gla_prompt = jnp.stack(gla_p_l)
    new_gla_sample = jnp.stack(gla_s_l)
    return (y_prompt, y_sample, new_ckv_prompt, new_kpe_prompt, new_ckv_sample, new_kpe_sample, new_gla_prompt, new_gla_sample)
```

```python
import functools

import jax
import jax.numpy as jnp
from jax import lax
from jax.experimental import pallas as pl
from jax.experimental.pallas import tpu as pltpu

F32 = jnp.float32
BF16 = jnp.bfloat16

N_META = 16
MLA_HEADS = 16
Q_LORA = 256
KV_LORA = 256
QK_NOPE = 64
QK_ROPE = 32
V_HEAD = 64
MLA_SCALE = (QK_NOPE + QK_ROPE) ** -0.5
ROPE_BASE = 10000.0
GLA_HEADS = 4
GLA_DK = 128
GLA_DV = 256
GLA_SCALE = GLA_DK ** -0.5
GATE_RANK = 16
GATE_TAU = 16.0
GLA_CHUNK = 64
EPS = 1e-6

LANES = 128
VMEM_LIMIT = 56 * 1024 * 1024


def _cparams(*sem):
    return pltpu.CompilerParams(dimension_semantics=sem, vmem_limit_bytes=VMEM_LIMIT)


def _rms(x, w):
    return x * lax.rsqrt(jnp.mean(x * x, axis=-1, keepdims=True) + EPS) * w


def _dot(a, b):
    return jnp.dot(a, b, preferred_element_type=F32)


def _dot_nt(a, b):
    return lax.dot_general(a, b, (((1,), (1,)), ((), ())), preferred_element_type=F32)


def _dot_tn(a, b):
    return lax.dot_general(a, b, (((0,), (0,)), ((), ())), preferred_element_type=F32)


def _full(shape):
    n = len(shape)
    return pl.BlockSpec(shape, lambda *_: (0,) * n)


def _fold_kernel(a_ref, b_ref, o_ref):
    o_ref[...] = lax.dot_general(a_ref[0], b_ref[0], (((1,), (1,)), ((), ())),
                                 precision=lax.Precision.HIGHEST,
                                 preferred_element_type=F32).astype(o_ref.dtype)


def _fold_qk(w_nope, w_uk):
    return pl.pallas_call(
        _fold_kernel,
        grid=(MLA_HEADS,),
        in_specs=[pl.BlockSpec((1, Q_LORA, QK_NOPE), lambda h: (h, 0, 0)),
                  pl.BlockSpec((1, KV_LORA, QK_NOPE), lambda h: (h, 0, 0))],
        out_specs=pl.BlockSpec((Q_LORA, KV_LORA), lambda h: (0, h)),
        out_shape=jax.ShapeDtypeStruct((Q_LORA, MLA_HEADS * KV_LORA), BF16),
        compiler_params=_cparams("arbitrary"),
        name="fold_qk",
    )(w_nope, w_uk)


def _mla_proj_kernel(x_ref, cos_ref, sin_ref, gpre_ref, win_ref, qn_ref, wql_ref, wqp_ref, wqps_ref,
                     kvn_ref, qlat_ref, qpe_ref, ckv_ref, kpe_ref, klat_ref, kpeb_ref, *, head_major):
    x = x_ref[0] if head_major else x_ref[...]
    h = _rms(x, gpre_ref[...]).astype(BF16)
    a = _dot(h, win_ref[...])
    cqn = _rms(a[:, :Q_LORA], qn_ref[...]).astype(BF16)
    ckv = _rms(a[:, Q_LORA:Q_LORA + KV_LORA], kvn_ref[...])
    cos = cos_ref[...]
    sin = sin_ref[...]
    o = Q_LORA + KV_LORA
    kpe = a[:, o:o + QK_ROPE] * cos[:, :QK_ROPE] + a[:, o + LANES:o + LANES + QK_ROPE] * sin[:, :QK_ROPE]
    qlat = _dot(cqn, wql_ref[...]) * MLA_SCALE
    qp = _dot(cqn, wqp_ref[...])
    qps = _dot(cqn, wqps_ref[...])
    nl = MLA_HEADS * QK_ROPE // LANES
    qpe = jnp.concatenate(
        [(qp[:, j * LANES:(j + 1) * LANES] * cos + qps[:, j * LANES:(j + 1) * LANES] * sin) * MLA_SCALE
         for j in range(nl)], axis=1)
    if head_major:
        for hd in range(MLA_HEADS):
            qlat_ref[0, hd] = qlat[:, hd * KV_LORA:(hd + 1) * KV_LORA].astype(BF16)
            qpe_ref[0, hd] = qpe[:, hd * QK_ROPE:(hd + 1) * QK_ROPE].astype(BF16)
        ckv_ref[0] = ckv
        kpe_ref[0] = kpe
        klat_ref[0] = ckv.astype(BF16)
        kpeb_ref[0] = kpe.astype(BF16)
    else:
        qlat_ref[...] = qlat.astype(BF16)
        qpe_ref[...] = qpe.astype(BF16)
        ckv_ref[...] = ckv
        kpe_ref[...] = kpe
        klat_ref[...] = ckv.astype(BF16)
        kpeb_ref[...] = kpe.astype(BF16)


def _mla_proj_weights_specs(w):
    return [_full(a.shape) for a in w]


def _mla_proj_prompt(x, cos, sin, w, tr):
    B, T, D = x.shape
    H = MLA_HEADS
    grid = (B, T // tr)
    in_specs = [pl.BlockSpec((1, tr, D), lambda b, i: (b, i, 0)),
                pl.BlockSpec((tr, LANES), lambda b, i: (i, 0)),
                pl.BlockSpec((tr, LANES), lambda b, i: (i, 0))] + _mla_proj_weights_specs(w)
    out_shape = [jax.ShapeDtypeStruct((B, H, T, KV_LORA), BF16),
                 jax.ShapeDtypeStruct((B, H, T, QK_ROPE), BF16),
                 jax.ShapeDtypeStruct((B, T, KV_LORA), F32),
                 jax.ShapeDtypeStruct((B, T, QK_ROPE), F32),
                 jax.ShapeDtypeStruct((B, T, KV_LORA), BF16),
                 jax.ShapeDtypeStruct((B, T, QK_ROPE), BF16)]
    out_specs = [pl.BlockSpec((1, H, tr, KV_LORA), lambda b, i: (b, 0, i, 0)),
                 pl.BlockSpec((1, H, tr, QK_ROPE), lambda b, i: (b, 0, i, 0)),
                 pl.BlockSpec((1, tr, KV_LORA), lambda b, i: (b, i, 0)),
                 pl.BlockSpec((1, tr, QK_ROPE), lambda b, i: (b, i, 0)),
                 pl.BlockSpec((1, tr, KV_LORA), lambda b, i: (b, i, 0)),
                 pl.BlockSpec((1, tr, QK_ROPE), lambda b, i: (b, i, 0))]
    return pl.pallas_call(
        functools.partial(_mla_proj_kernel, head_major=True),
        grid=grid, in_specs=in_specs, out_specs=out_specs, out_shape=out_shape,
        compiler_params=_cparams("arbitrary", "arbitrary"), name="mla_proj_prompt",
    )(x, cos, sin, *w)


def _mla_proj_small(x, cos, sin, w):
    R, D = x.shape
    H = MLA_HEADS
    in_specs = [_full((R, D)), _full((R, LANES)), _full((R, LANES))] + _mla_proj_weights_specs(w)
    shapes = [((R, H * KV_LORA), BF16), ((R, H * QK_ROPE), BF16), ((R, KV_LORA), F32),
              ((R, QK_ROPE), F32), ((R, KV_LORA), BF16), ((R, QK_ROPE), BF16)]
    return pl.pallas_call(
        functools.partial(_mla_proj_kernel, head_major=False),
        grid=(1,), in_specs=in_specs,
        out_specs=[_full(s) for s, _ in shapes],
        out_shape=[jax.ShapeDtypeStruct(s, d) for s, d in shapes],
        compiler_params=_cparams("arbitrary"), name="mla_proj_small",
    )(x, cos, sin, *w)


def _flash_kernel(qlat_ref, qpe_ref, klat_ref, kpe_ref, kmlat_ref, kmpe_ref, x_ref, wuv_ref, wo_ref,
                  gpost_ref, o_ref, m_sc, l_sc, acc_sc, ocat_sc, *, tq, hg):
    i = pl.program_id(1)
    ng = MLA_HEADS // hg
    rows = hg * tq

    def q_group(g):
        ql = qlat_ref[0, g * hg:(g + 1) * hg].reshape(rows, KV_LORA)
        qp = qpe_ref[0, g * hg:(g + 1) * hg].reshape(rows, QK_ROPE)
        return ql, qp

    def update(g, kl, kp, mask, first):
        ql, qp = q_group(g)
        s = _dot_nt(ql, kl) + _dot_nt(qp, kp)
        if mask is not None:
            s = jnp.where(mask, s, -jnp.inf)
        smax = jnp.max(s, axis=-1, keepdims=True)
        if first:
            m_new = smax
            p = jnp.exp(s - m_new)
            l_sc[g] = jnp.sum(p, axis=-1, keepdims=True)
            acc_sc[g] = _dot(p.astype(BF16), kl)
        else:
            m_old = m_sc[g]
            m_new = jnp.maximum(m_old, smax)
            alpha = jnp.exp(m_old - m_new)
            p = jnp.exp(s - m_new)
            l_sc[g] = alpha * l_sc[g] + jnp.sum(p, axis=-1, keepdims=True)
            acc_sc[g] = alpha * acc_sc[g] + _dot(p.astype(BF16), kl)
        m_sc[g] = m_new

    nm = kmlat_ref.shape[0]
    meta_mask = lax.broadcasted_iota(jnp.int32, (rows, nm), 1) < N_META
    kml = kmlat_ref[...]
    kmp = kmpe_ref[...]
    for g in range(ng):
        update(g, kml, kmp, meta_mask, True)

    def body(j, carry):
        off = pl.multiple_of(j * tq, tq)
        kl = klat_ref[0, pl.ds(off, tq), :]
        kp = kpe_ref[0, pl.ds(off, tq), :]
        for g in range(ng):
            update(g, kl, kp, None, False)
        return carry

    lax.fori_loop(0, i, body, 0)

    off = pl.multiple_of(i * tq, tq)
    kl = klat_ref[0, pl.ds(off, tq), :]
    kp = kpe_ref[0, pl.ds(off, tq), :]
    row_t = lax.broadcasted_iota(jnp.int32, (hg, tq, tq), 1).reshape(rows, tq)
    col = lax.broadcasted_iota(jnp.int32, (rows, tq), 1)
    causal = col <= row_t
    for g in range(ng):
        update(g, kl, kp, causal, False)

    for g in range(ng):
        o = acc_sc[g] / l_sc[g]
        for hh in range(hg):
            hd = g * hg + hh
            oh = _dot(o[hh * tq:(hh + 1) * tq].astype(BF16), wuv_ref[hd])
            ocat_sc[:, hd * V_HEAD:(hd + 1) * V_HEAD] = oh
    m = _dot(ocat_sc[...].astype(BF16), wo_ref[...])
    o_ref[0] = x_ref[0] + _rms(m, gpost_ref[...])


def _flash_prompt(qlat, qpe, klat, kpe, kmlat, kmpe, x, wuv, wo, gpost, tq, hg):
    B, H, T, C = qlat.shape
    D = x.shape[-1]
    rows = hg * tq
    ng = H // hg
    in_specs = [pl.BlockSpec((1, H, tq, C), lambda b, i: (b, 0, i, 0)),
                pl.BlockSpec((1, H, tq, QK_ROPE), lambda b, i: (b, 0, i, 0)),
                pl.BlockSpec((1, T, C), lambda b, i: (b, 0, 0)),
                pl.BlockSpec((1, T, QK_ROPE), lambda b, i: (b, 0, 0)),
                _full(kmlat.shape), _full(kmpe.shape),
                pl.BlockSpec((1, tq, D), lambda b, i: (b, i, 0)),
                _full(wuv.shape), _full(wo.shape), _full(gpost.shape)]
    return pl.pallas_call(
        functools.partial(_flash_kernel, tq=tq, hg=hg),
        grid=(B, T // tq), in_specs=in_specs,
        out_specs=pl.BlockSpec((1, tq, D), lambda b, i: (b, i, 0)),
        out_shape=jax.ShapeDtypeStruct((B, T, D), F32),
        scratch_shapes=[pltpu.VMEM((ng, rows, 1), F32), pltpu.VMEM((ng, rows, 1), F32),
                        pltpu.VMEM((ng, rows, C), F32), pltpu.VMEM((tq, H * V_HEAD), F32)],
        compiler_params=_cparams("arbitrary", "arbitrary"), name="flash_prompt",
    )(qlat, qpe, klat, kpe, kmlat, kmpe, x, wuv, wo, gpost)


def _meta_attn_kernel(qlat_ref, qpe_ref, kl_ref, kp_ref, o_ref):
    kl = kl_ref[...]
    s = _dot_nt(qlat_ref[...], kl) + _dot_nt(qpe_ref[...], kp_ref[...])
    r, n = s.shape
    tok = lax.broadcasted_iota(jnp.int32, (r // MLA_HEADS, MLA_HEADS, n), 0).reshape(r, n)
    col = lax.broadcasted_iota(jnp.int32, (r, n), 1)
    s = jnp.where(col <= tok, s, -jnp.inf)
    p = jnp.exp(s - jnp.max(s, axis=-1, keepdims=True))
    l = jnp.sum(p, axis=-1, keepdims=True)
    o_ref[...] = _dot(p.astype(BF16), kl) / l


def _meta_attn(qlat, qpe, kl, kp):
    r = qlat.shape[0]
    return pl.pallas_call(
        _meta_attn_kernel, grid=(1,),
        in_specs=[_full(qlat.shape), _full(qpe.shape), _full(kl.shape), _full(kp.shape)],
        out_specs=_full((r, KV_LORA)), out_shape=jax.ShapeDtypeStruct((r, KV_LORA), F32),
        compiler_params=_cparams("arbitrary"), name="meta_attn",
    )(qlat, qpe, kl, kp)


def _decode_kernel(pt_ref, qlat_ref, qpe_ref, *refs, pages, n_new):
    ckv_refs = refs[:pages]
    kpe_refs = refs[pages:2 * pages]
    knl_ref, knp_ref, o_ref, m_sc, l_sc, acc_sc = refs[2 * pages:]
    j = pl.program_id(1)
    ql = qlat_ref[...]
    qp = qpe_ref[...]

    @pl.when(j == 0)
    def _():
        m_sc[...] = jnp.full(m_sc.shape, -jnp.inf, F32)
        l_sc[...] = jnp.zeros(l_sc.shape, F32)
        acc_sc[...] = jnp.zeros(acc_sc.shape, F32)

    def accumulate(kls, ss):
        m_old = m_sc[...]
        m_new = m_old
        for s in ss:
            m_new = jnp.maximum(m_new, jnp.max(s, axis=-1, keepdims=True))
        alpha = jnp.exp(m_old - m_new)
        l = alpha * l_sc[...]
        acc = alpha * acc_sc[...]
        for kl, s in zip(kls, ss):
            p = jnp.exp(s - m_new)
            l = l + jnp.sum(p, axis=-1, keepdims=True)
            acc = acc + _dot(p.astype(BF16), kl)
        m_sc[...] = m_new
        l_sc[...] = l
        acc_sc[...] = acc

    kls, ss = [], []
    for p in range(pages):
        kl = ckv_refs[p][...].astype(BF16)
        kp = kpe_refs[p][...].astype(BF16)
        kls.append(kl)
        ss.append(_dot_nt(ql, kl) + _dot_nt(qp, kp))
    accumulate(kls, ss)

    @pl.when(j == pl.num_programs(1) - 1)
    def _():
        kl = knl_ref[0]
        s = _dot_nt(ql, kl) + _dot_nt(qp, knp_ref[0])
        r, n = s.shape
        tok = lax.broadcasted_iota(jnp.int32, (n_new, r // n_new, n), 0).reshape(r, n)
        col = lax.broadcasted_iota(jnp.int32, (r, n), 1)
        s = jnp.where(col <= tok, s, -jnp.inf)
        accumulate([kl], [s])
        o_ref[...] = acc_sc[...] / l_sc[...]


def _decode_attn(page_table, qlat, qpe, cache_ckv, cache_kpe, layer, knl, knp, pages):
    nseq, npages = page_table.shape
    page = cache_ckv.shape[2]
    n_new = qlat.shape[0] // (nseq * MLA_HEADS)
    rq = n_new * MLA_HEADS
    steps = npages // pages
    pt = page_table.reshape(-1)

    def cache_spec(width, p):
        return pl.BlockSpec((None, None, page, width),
                            lambda s, j, pt_ref: (layer, pt_ref[s * npages + j * pages + p], 0, 0))

    in_specs = ([pl.BlockSpec((rq, KV_LORA), lambda s, j, pt_ref: (s, 0)),
                 pl.BlockSpec((rq, QK_ROPE), lambda s, j, pt_ref: (s, 0))]
                + [cache_spec(KV_LORA, p) for p in range(pages)]
                + [cache_spec(QK_ROPE, p) for p in range(pages)]
                + [pl.BlockSpec((1,) + knl.shape[1:], lambda s, j, pt_ref: (s, 0, 0)),
                   pl.BlockSpec((1,) + knp.shape[1:], lambda s, j, pt_ref: (s, 0, 0))])
    grid_spec = pltpu.PrefetchScalarGridSpec(
        num_scalar_prefetch=1, grid=(nseq, steps), in_specs=in_specs,
        out_specs=pl.BlockSpec((rq, KV_LORA), lambda s, j, pt_ref: (s, 0)),
        scratch_shapes=[pltpu.VMEM((rq, 1), F32), pltpu.VMEM((rq, 1), F32), pltpu.VMEM((rq, KV_LORA), F32)])
    return pl.pallas_call(
        functools.partial(_decode_kernel, pages=pages, n_new=n_new),
        grid_spec=grid_spec,
        out_shape=jax.ShapeDtypeStruct((nseq * rq, KV_LORA), F32),
        compiler_params=_cparams("arbitrary", "arbitrary"), name="decode_attn",
    )(pt, qlat, qpe, *([cache_ckv] * pages), *([cache_kpe] * pages), knl, knp)


def _mla_out_kernel(o_ref, x_ref, wuv_ref, wo_ref, gpost_ref, y_ref, ocat_sc):
    for hd in range(MLA_HEADS):
        oh = _dot(o_ref[:, hd * KV_LORA:(hd + 1) * KV_LORA].astype(BF16), wuv_ref[hd])
        ocat_sc[:, hd * V_HEAD:(hd + 1) * V_HEAD] = oh
    m = _dot(ocat_sc[...].astype(BF16), wo_ref[...])
    y_ref[...] = x_ref[...] + _rms(m, gpost_ref[...])


def _mla_out_small(o, x, wuv, wo, gpost):
    R, D = x.shape
    return pl.pallas_call(
        _mla_out_kernel, grid=(1,),
        in_specs=[_full(o.shape), _full(x.shape), _full(wuv.shape), _full(wo.shape), _full(gpost.shape)],
        out_specs=_full((R, D)), out_shape=jax.ShapeDtypeStruct((R, D), F32),
        scratch_shapes=[pltpu.VMEM((R, MLA_HEADS * V_HEAD), F32)],
        compiler_params=_cparams("arbitrary"), name="mla_out_small",
    )(o, x, wuv, wo, gpost)


def _mlp_kernel(x_ref, gpre_ref, wup_ref, wdn_ref, gpost_ref, o_ref, *, fc):
    x = x_ref[...]
    h = _rms(x, gpre_ref[...]).astype(BF16)
    dff = wup_ref.shape[1]
    acc = None
    for c in range(dff // fc):
        u = _dot(h, wup_ref[:, c * fc:(c + 1) * fc])
        u = jnp.square(jnp.maximum(u, 0.0)).astype(BF16)
        d = _dot(u, wdn_ref[c * fc:(c + 1) * fc, :])
        acc = d if acc is None else acc + d
    o_ref[...] = x + _rms(acc, gpost_ref[...])


def _mlp(x, gpre, wup, wdn, gpost, tm, fc=1024):
    N, D = x.shape
    const = lambda a: pl.BlockSpec(a.shape, lambda i: (0, 0), pipeline_mode=pl.Buffered(1))
    return pl.pallas_call(
        functools.partial(_mlp_kernel, fc=fc), grid=(N // tm,),
        in_specs=[pl.BlockSpec((tm, D), lambda i: (i, 0)), const(gpre), const(wup), const(wdn), const(gpost)],
        out_specs=pl.BlockSpec((tm, D), lambda i: (i, 0)),
        out_shape=jax.ShapeDtypeStruct((N, D), F32),
        compiler_params=_cparams("arbitrary"), name="mlp",
    )(x, gpre, wup, wdn, gpost)


def _gla_proj_kernel(x_ref, gpre_ref, win_ref, wg_ref, bg_ref, q_ref, k_ref, v_ref, r_ref, la_ref):
    hk = GLA_HEADS * GLA_DK
    hv = GLA_HEADS * GLA_DV
    h = _rms(x_ref[...], gpre_ref[...]).astype(BF16)
    a = _dot(h, win_ref[...])
    q_ref[...] = a[:, :hk] * GLA_SCALE
    k_ref[...] = a[:, hk:2 * hk]
    v_ref[...] = a[:, 2 * hk:2 * hk + hv]
    r_ref[...] = a[:, 2 * hk + hv:2 * hk + 2 * hv]
    gd = a[:, 2 * hk + 2 * hv:].astype(BF16)
    z = _dot(gd, wg_ref[...]) + bg_ref[...]
    la_ref[...] = (jnp.minimum(z, 0.0) - jnp.log(1.0 + jnp.exp(-jnp.abs(z)))) * (1.0 / GATE_TAU)


def _gla_proj(x, gpre, win, wg, bg, tr):
    N, D = x.shape
    hk = GLA_HEADS * GLA_DK
    hv = GLA_HEADS * GLA_DV
    row = lambda w: pl.BlockSpec((tr, w), lambda i: (i, 0))
    return pl.pallas_call(
        _gla_proj_kernel, grid=(N // tr,),
        in_specs=[row(D), _full(gpre.shape), _full(win.shape), _full(wg.shape), _full(bg.shape)],
        out_specs=[row(hk), row(hk), row(hv), row(hv), row(hk)],
        out_shape=[jax.ShapeDtypeStruct((N, w), F32) for w in (hk, hk, hv, hv, hk)],
        compiler_params=_cparams("arbitrary"), name="gla_proj",
    )(x, gpre, win, wg, bg)


def _gla_gate(o, r, gn):
    return _rms(o, gn) * (r / (1.0 + jnp.exp(-r)))


def _gla_prompt_kernel(q_ref, k_ref, v_ref, r_ref, la_ref, x_ref, s0_ref, gn_ref, wo_ref, gpost_ref,
                       y_ref, sfin_ref, s_sc, ocat_sc, *, rows):
    c_sz = GLA_CHUNK

    @pl.when(pl.program_id(1) == 0)
    def _():
        s_sc[...] = s0_ref[...]

    ri = lax.broadcasted_iota(jnp.int32, (c_sz, c_sz), 0)
    ci = lax.broadcasted_iota(jnp.int32, (c_sz, c_sz), 1)
    tril = ci <= ri
    ltri = tril.astype(F32)
    gn = gn_ref[...]
    for c in range(rows // c_sz):
        sl = slice(c * c_sz, (c + 1) * c_sz)
        la = la_ref[0, sl, :]
        b = jnp.dot(ltri, la, precision=lax.Precision.HIGHEST, preferred_element_type=F32)
        b_last = b[c_sz - 1:c_sz, :]
        q_in = (q_ref[0, sl, :] * jnp.exp(b)).astype(BF16)
        k = k_ref[0, sl, :]
        k_in = (k * jnp.exp(-b)).astype(BF16)
        k_dec = (k * jnp.exp(b_last - b)).astype(BF16)
        dec = jnp.exp(b_last)
        for hd in range(GLA_HEADS):
            ks = slice(hd * GLA_DK, (hd + 1) * GLA_DK)
            vs = slice(hd * GLA_DV, (hd + 1) * GLA_DV)
            v = v_ref[0, sl, vs].astype(BF16)
            a = jnp.where(tril, _dot_nt(q_in[:, ks], k_in[:, ks]), 0.0)
            s_old = s_sc[hd]
            o = _dot(a.astype(BF16), v) + _dot(q_in[:, ks], s_old.astype(BF16))
            dcol = jnp.transpose(jnp.broadcast_to(dec[:, ks], (GLA_DK, GLA_DK)))
            dfull = jnp.concatenate([dcol] * (GLA_DV // GLA_DK), axis=1)
            s_sc[hd] = dfull * s_old + _dot_tn(k_dec[:, ks], v)
            ocat_sc[sl, vs] = _gla_gate(o, r_ref[0, sl, vs], gn)
    m = _dot(ocat_sc[...].astype(BF16), wo_ref[...])
    y_ref[0] = x_ref[0] + _rms(m, gpost_ref[...])

    @pl.when(pl.program_id(1) == pl.num_programs(1) - 1)
    def _():
        sfin_ref[0] = s_sc[...]


def _gla_prompt(q, k, v, r, la, x, s0, gn, wo, gpost, rows):
    B, T, D = x.shape
    hk = GLA_HEADS * GLA_DK
    hv = GLA_HEADS * GLA_DV
    blk = lambda w: pl.BlockSpec((1, rows, w), lambda b, i: (b, i, 0))
    return pl.pallas_call(
        functools.partial(_gla_prompt_kernel, rows=rows), grid=(B, T // rows),
        in_specs=[blk(hk), blk(hk), blk(hv), blk(hv), blk(hk), blk(D), _full(s0.shape), _full(gn.shape),
                  _full(wo.shape), _full(gpost.shape)],
        out_specs=[blk(D), pl.BlockSpec((1,) + s0.shape, lambda b, i: (b, 0, 0, 0))],
        out_shape=[jax.ShapeDtypeStruct((B, T, D), F32), jax.ShapeDtypeStruct((B,) + s0.shape, F32)],
        scratch_shapes=[pltpu.VMEM(s0.shape, F32), pltpu.VMEM((rows, hv), F32)],
        compiler_params=_cparams("arbitrary", "arbitrary"), name="gla_prompt",
    )(q, k, v, r, la, x, s0, gn, wo, gpost)


def _gla_tokens_kernel(qt_ref, kt_ref, lat_ref, v_ref, s0_ref, o_ref, sfin_ref, *, ntok):
    for hd in range(GLA_HEADS):
        ks = slice(hd * GLA_DK, (hd + 1) * GLA_DK)
        vs = slice(hd * GLA_DV, (hd + 1) * GLA_DV)
        s = s0_ref[0, hd]
        for t in range(ntok):
            a = jnp.exp(lat_ref[0, ks, t:t + 1])
            s = a * s + kt_ref[0, ks, t:t + 1] * v_ref[0, t:t + 1, vs]
            o_ref[0, t:t + 1, vs] = jnp.sum(qt_ref[0, ks, t:t + 1] * s, axis=0, keepdims=True)
        sfin_ref[0, hd] = s


def _gla_tokens(qt, kt, lat, v, s0):
    nseq, hk, ntok = qt.shape
    hv = v.shape[-1]
    col = pl.BlockSpec((1, hk, ntok), lambda s: (s, 0, 0))
    st = pl.BlockSpec((1,) + s0.shape[1:], lambda s: (s, 0, 0, 0))
    return pl.pallas_call(
        functools.partial(_gla_tokens_kernel, ntok=ntok), grid=(nseq,),
        in_specs=[col, col, col, pl.BlockSpec((1, ntok, hv), lambda s: (s, 0, 0)), st],
        out_specs=[pl.BlockSpec((1, ntok, hv), lambda s: (s, 0, 0)), st],
        out_shape=[jax.ShapeDtypeStruct((nseq, ntok, hv), F32), jax.ShapeDtypeStruct(s0.shape, F32)],
        compiler_params=_cparams("arbitrary"), name="gla_tokens",
    )(qt, kt, lat, v, s0)


def _gla_out_kernel(o_ref, r_ref, x_ref, gn_ref, wo_ref, gpost_ref, y_ref, ocat_sc):
    gn = gn_ref[...]
    for hd in range(GLA_HEADS):
        vs = slice(hd * GLA_DV, (hd + 1) * GLA_DV)
        ocat_sc[:, vs] = _gla_gate(o_ref[:, vs], r_ref[:, vs], gn)
    m = _dot(ocat_sc[...].astype(BF16), wo_ref[...])
    y_ref[...] = x_ref[...] + _rms(m, gpost_ref[...])


def _gla_out_small(o, r, x, gn, wo, gpost):
    R, D = x.shape
    return pl.pallas_call(
        _gla_out_kernel, grid=(1,),
        in_specs=[_full(o.shape), _full(r.shape), _full(x.shape), _full(gn.shape), _full(wo.shape),
                  _full(gpost.shape)],
        out_specs=_full((R, D)), out_shape=jax.ShapeDtypeStruct((R, D), F32),
        scratch_shapes=[pltpu.VMEM(o.shape, F32)],
        compiler_params=_cparams("arbitrary"), name="gla_out_small",
    )(o, r, x, gn, wo, gpost)


def _rope_tables(pos):
    half = QK_ROPE // 2
    inv = ROPE_BASE ** (-jnp.arange(half, dtype=F32) / half)
    ang = pos.astype(F32)[:, None] * inv[None, :]
    c, s = jnp.cos(ang), jnp.sin(ang)
    reps = LANES // QK_ROPE
    return (jnp.tile(jnp.concatenate([c, c], axis=1), (1, reps)),
            jnp.tile(jnp.concatenate([-s, s], axis=1), (1, reps)))


def _swap_halves(w):
    half = QK_ROPE // 2
    return jnp.concatenate([w[..., half:], w[..., :half]], axis=-1)


def kernel(x_prompt, x_sample, cache_ckv, cache_kpe, state_gla, page_table, meta_tokens, norm_pre_mix, norm_post_mix, norm_pre_mlp, norm_post_mlp, mla_w_in, mla_q_norm, mla_w_uq, mla_kv_norm, mla_w_uk, mla_w_uv, mla_w_o, gla_w_in, gla_w_gate, gla_b_gate, gla_norm, gla_w_o, mlp_w_up, mlp_w_down):
    B, T, D = x_prompt.shape
    DB, S, _ = x_sample.shape
    H = MLA_HEADS
    n_s = DB * S
    past_len = page_table.shape[1] * cache_ckv.shape[2]
    row = lambda a: a.reshape(1, -1)

    w_in = mla_w_in[0]
    o = Q_LORA + KV_LORA
    w_kpe = w_in[:, o:]
    zpad = jnp.zeros((D, LANES - QK_ROPE), F32)
    w_in_p = jnp.concatenate([w_in[:, :o], w_kpe, zpad, _swap_halves(w_kpe), zpad], axis=1).astype(BF16)
    w_uq = mla_w_uq[0].reshape(Q_LORA, H, QK_NOPE + QK_ROPE)
    w_nope = jnp.transpose(w_uq[:, :, :QK_NOPE], (1, 0, 2))
    w_uk = jnp.transpose(mla_w_uk[0], (1, 0, 2))
    w_qlat = _fold_qk(w_nope, w_uk)
    w_qpe = w_uq[:, :, QK_NOPE:]
    w_qp = w_qpe.reshape(Q_LORA, H * QK_ROPE).astype(BF16)
    w_qps = _swap_halves(w_qpe).reshape(Q_LORA, H * QK_ROPE).astype(BF16)
    proj_w = (row(norm_pre_mix[0]), w_in_p, row(mla_q_norm[0]), w_qlat, w_qp, w_qps, row(mla_kv_norm[0]))
    w_uv = jnp.transpose(mla_w_uv[0], (1, 0, 2)).astype(BF16)
    w_o = mla_w_o[0].astype(BF16)
    g_post0 = row(norm_post_mix[0])

    x_small = jnp.concatenate([x_sample.reshape(n_s, D), meta_tokens], axis=0)
    pos_small = jnp.concatenate([past_len + jnp.tile(jnp.arange(S, dtype=jnp.int32), DB),
                                 jnp.arange(N_META, dtype=jnp.int32)])
    cos_s, sin_s = _rope_tables(pos_small)
    qlat_s, qpe_s, ckv_s, kpe_s, klat_s, kpeb_s = _mla_proj_small(x_small, cos_s, sin_s, proj_w)

    npad = LANES
    kml = jnp.pad(klat_s[n_s:], ((0, npad - N_META), (0, 0)))
    kmp = jnp.pad(kpeb_s[n_s:], ((0, npad - N_META), (0, 0)))
    o_meta = _meta_attn(qlat_s[n_s:].reshape(N_META * H, KV_LORA), qpe_s[n_s:].reshape(N_META * H, QK_ROPE),
                        kml, kmp)
    knl = jnp.pad(klat_s[:n_s].reshape(DB, S, KV_LORA), ((0, 0), (0, npad - S), (0, 0)))
    knp = jnp.pad(kpeb_s[:n_s].reshape(DB, S, QK_ROPE), ((0, 0), (0, npad - S), (0, 0)))
    o_samp = _decode_attn(page_table, qlat_s[:n_s].reshape(n_s * H, KV_LORA),
                          qpe_s[:n_s].reshape(n_s * H, QK_ROPE), cache_ckv, cache_kpe, 0, knl, knp, pages=8)
    o_small = jnp.concatenate([o_samp.reshape(n_s, H * KV_LORA), o_meta.reshape(N_META, H * KV_LORA)], axis=0)
    xs1 = _mla_out_small(o_small, x_small, w_uv, w_o, g_post0)

    cos_p, sin_p = _rope_tables(N_META + jnp.arange(T, dtype=jnp.int32))
    qlat_p, qpe_p, ckv_p, kpe_p, klat_p, kpeb_p = _mla_proj_prompt(x_prompt, cos_p, sin_p, proj_w, tr=256)
    xp1 = _flash_prompt(qlat_p, qpe_p, klat_p, kpeb_p, kml, kmp, x_prompt, w_uv, w_o, g_post0, tq=256, hg=4)

    wup0, wdn0 = mlp_w_up[0].astype(BF16), mlp_w_down[0].astype(BF16)
    mlp0 = (row(norm_pre_mlp[0]), wup0, wdn0, row(norm_post_mlp[0]))
    xp2 = _mlp(xp1.reshape(B * T, D), *mlp0, tm=512)
    xs2 = _mlp(xs1, *mlp0, tm=xs1.shape[0])

    hk = GLA_HEADS * GLA_DK
    hv = GLA_HEADS * GLA_DV
    gw = gla_w_in[0]
    gw_p = jnp.concatenate([gw, jnp.zeros((D, LANES - GATE_RANK), F32)], axis=1).astype(BF16)
    wg_p = jnp.concatenate([gla_w_gate[0], jnp.zeros((LANES - GATE_RANK, hk), F32)], axis=0).astype(BF16)
    gla_pw = (row(norm_pre_mix[1]), gw_p, wg_p, row(gla_b_gate[0]))
    gn = row(gla_norm[0])
    gwo = gla_w_o[0].astype(BF16)
    g_post1 = row(norm_post_mix[1])

    q_s, k_s, v_s, r_s, la_s = _gla_proj(xs2, *gla_pw, tr=xs2.shape[0])
    tcol = lambda a, n, l: jnp.transpose(a.reshape(n, l, hk), (0, 2, 1))
    o_gs, s_gs = _gla_tokens(tcol(q_s[:n_s], DB, S), tcol(k_s[:n_s], DB, S), tcol(la_s[:n_s], DB, S),
                             v_s[:n_s].reshape(DB, S, hv), state_gla[0])
    zero_state = jnp.zeros((1, GLA_HEADS, GLA_DK, GLA_DV), F32)
    o_gm, s_gm = _gla_tokens(tcol(q_s[n_s:], 1, N_META), tcol(k_s[n_s:], 1, N_META), tcol(la_s[n_s:], 1, N_META),
                             v_s[n_s:].reshape(1, N_META, hv), zero_state)
    o_gsmall = jnp.concatenate([o_gs.reshape(n_s, hv), o_gm.reshape(N_META, hv)], axis=0)
    xs3 = _gla_out_small(o_gsmall, r_s, xs2, gn, gwo, g_post1)

    q_p, k_p, v_p, r_p, la_p = _gla_proj(xp2, *gla_pw, tr=512)
    r3 = lambda a: a.reshape(B, T, a.shape[-1])
    xp3, s_gp = _gla_prompt(r3(q_p), r3(k_p), r3(v_p), r3(r_p), r3(la_p), xp2.reshape(B, T, D), s_gm[0],
                            gn, gwo, g_post1, rows=256)

    mlp1 = (row(norm_pre_mlp[1]), mlp_w_up[1].astype(BF16), mlp_w_down[1].astype(BF16), row(norm_post_mlp[1]))
    y_prompt = _mlp(xp3.reshape(B * T, D), *mlp1, tm=512).reshape(B, T, D)
    xs4 = _mlp(xs3, *mlp1, tm=xs3.shape[0])
    y_sample = xs4[:n_s].reshape(DB, S, D)

    bmeta = lambda a: jnp.broadcast_to(a[None], (B,) + a.shape)
    new_ckv_prompt = jnp.concatenate([bmeta(ckv_s[n_s:]), ckv_p], axis=1)[None]
    new_kpe_prompt = jnp.concatenate([bmeta(kpe_s[n_s:]), kpe_p], axis=1)[None]
    new_ckv_sample = ckv_s[:n_s].reshape(1, DB, S, KV_LORA)
    new_kpe_sample = kpe_s[:n_s].reshape(1, DB, S, QK_ROPE)
    return (y_prompt, y_sample, new_ckv_prompt, new_kpe_prompt, new_ckv_sample, new_kpe_sample,
            s_gp[None], s_gs[None])
```

```python
import functools

import jax
import jax.numpy as jnp
from jax import lax
from jax.experimental import pallas as pl
from jax.experimental.pallas import tpu as pltpu

F32 = jnp.float32
BF16 = jnp.bfloat16

N_META = 16
MLA_HEADS = 16
Q_LORA = 256
KV_LORA = 256
QK_NOPE = 64
QK_ROPE = 32
V_HEAD = 64
MLA_SCALE = (QK_NOPE + QK_ROPE) ** -0.5
ROPE_BASE = 10000.0
GLA_HEADS = 4
GLA_DK = 128
GLA_DV = 256
GLA_SCALE = GLA_DK ** -0.5
GATE_RANK = 16
GATE_TAU = 16.0
GLA_CHUNK = 64
EPS = 1e-6

LANES = 128
VMEM_LIMIT = 56 * 1024 * 1024


def _cparams(*sem):
    return pltpu.CompilerParams(dimension_semantics=sem, vmem_limit_bytes=VMEM_LIMIT)


def _rms(x, w):
    return x * lax.rsqrt(jnp.mean(x * x, axis=-1, keepdims=True) + EPS) * w


def _dot(a, b):
    return jnp.dot(a, b, preferred_element_type=F32)


def _dot_nt(a, b):
    return lax.dot_general(a, b, (((1,), (1,)), ((), ())), preferred_element_type=F32)


def _dot_tn(a, b):
    return lax.dot_general(a, b, (((0,), (0,)), ((), ())), preferred_element_type=F32)


def _full(shape):
    n = len(shape)
    return pl.BlockSpec(shape, lambda *_: (0,) * n)


def _fold_kernel(a_ref, b_ref, o_ref):
    o_ref[...] = lax.dot_general(a_ref[0], b_ref[0], (((1,), (1,)), ((), ())),
                                 precision=lax.Precision.HIGHEST,
                                 preferred_element_type=F32).astype(o_ref.dtype)


def _fold_qk(w_nope, w_uk):
    return pl.pallas_call(
        _fold_kernel,
        grid=(MLA_HEADS,),
        in_specs=[pl.BlockSpec((1, Q_LORA, QK_NOPE), lambda h: (h, 0, 0)),
                  pl.BlockSpec((1, KV_LORA, QK_NOPE), lambda h: (h, 0, 0))],
        out_specs=pl.BlockSpec((Q_LORA, KV_LORA), lambda h: (0, h)),
        out_shape=jax.ShapeDtypeStruct((Q_LORA, MLA_HEADS * KV_LORA), BF16),
        compiler_params=_cparams("arbitrary"),
        name="fold_qk",
    )(w_nope, w_uk)


def _mla_proj_kernel(x_ref, cos_ref, sin_ref, gpre_ref, win_ref, qn_ref, wql_ref, wqp_ref, wqps_ref,
                     kvn_ref, qlat_ref, qpe_ref, ckv_ref, kpe_ref, klat_ref, kpeb_ref, *, head_major):
    x = x_ref[0] if head_major else x_ref[...]
    h = _rms(x, gpre_ref[...]).astype(BF16)
    a = _dot(h, win_ref[...])
    cqn = _rms(a[:, :Q_LORA], qn_ref[...]).astype(BF16)
    ckv = _rms(a[:, Q_LORA:Q_LORA + KV_LORA], kvn_ref[...])
    cos = cos_ref[...]
    sin = sin_ref[...]
    o = Q_LORA + KV_LORA
    kpe = a[:, o:o + QK_ROPE] * cos[:, :QK_ROPE] + a[:, o + LANES:o + LANES + QK_ROPE] * sin[:, :QK_ROPE]
    qlat = _dot(cqn, wql_ref[...]) * MLA_SCALE
    qp = _dot(cqn, wqp_ref[...])
    qps = _dot(cqn, wqps_ref[...])
    nl = MLA_HEADS * QK_ROPE // LANES
    qpe = jnp.concatenate(
        [(qp[:, j * LANES:(j + 1) * LANES] * cos + qps[:, j * LANES:(j + 1) * LANES] * sin) * MLA_SCALE
         for j in range(nl)], axis=1)
    if head_major:
        for hd in range(MLA_HEADS):
            qlat_ref[0, hd] = qlat[:, hd * KV_LORA:(hd + 1) * KV_LORA].astype(BF16)
            qpe_ref[0, hd] = qpe[:, hd * QK_ROPE:(hd + 1) * QK_ROPE].astype(BF16)
        ckv_ref[0] = ckv
        kpe_ref[0] = kpe
        klat_ref[0] = ckv.astype(BF16)
        kpeb_ref[0] = kpe.astype(BF16)
    else:
        qlat_ref[...] = qlat.astype(BF16)
        qpe_ref[...] = qpe.astype(BF16)
        ckv_ref[...] = ckv
        kpe_ref[...] = kpe
        klat_ref[...] = ckv.astype(BF16)
        kpeb_ref[...] = kpe.astype(BF16)


def _mla_proj_weights_specs(w):
    return [_full(a.shape) for a in w]


def _mla_proj_prompt(x, cos, sin, w, tr):
    B, T, D = x.shape
    H = MLA_HEADS
    grid = (B, T // tr)
    in_specs = [pl.BlockSpec((1, tr, D), lambda b, i: (b, i, 0)),
                pl.BlockSpec((tr, LANES), lambda b, i: (i, 0)),
                pl.BlockSpec((tr, LANES), lambda b, i: (i, 0))] + _mla_proj_weights_specs(w)
    out_shape = [jax.ShapeDtypeStruct((B, H, T, KV_LORA), BF16),
                 jax.ShapeDtypeStruct((B, H, T, QK_ROPE), BF16),
                 jax.ShapeDtypeStruct((B, T, KV_LORA), F32),
                 jax.ShapeDtypeStruct((B, T, QK_ROPE), F32),
                 jax.ShapeDtypeStruct((B, T, KV_LORA), BF16),
                 jax.ShapeDtypeStruct((B, T, QK_ROPE), BF16)]
    out_specs = [pl.BlockSpec((1, H, tr, KV_LORA), lambda b, i: (b, 0, i, 0)),
                 pl.BlockSpec((1, H, tr, QK_ROPE), lambda b, i: (b, 0, i, 0)),
                 pl.BlockSpec((1, tr, KV_LORA), lambda b, i: (b, i, 0)),
                 pl.BlockSpec((1, tr, QK_ROPE), lambda b, i: (b, i, 0)),
                 pl.BlockSpec((1, tr, KV_LORA), lambda b, i: (b, i, 0)),
                 pl.BlockSpec((1, tr, QK_ROPE), lambda b, i: (b, i, 0))]
    return pl.pallas_call(
        functools.partial(_mla_proj_kernel, head_major=True),
        grid=grid, in_specs=in_specs, out_specs=out_specs, out_shape=out_shape,
        compiler_params=_cparams("arbitrary", "arbitrary"), name="mla_proj_prompt",
    )(x, cos, sin, *w)


def _mla_proj_small(x, cos, sin, w):
    R, D = x.shape
    H = MLA_HEADS
    in_specs = [_full((R, D)), _full((R, LANES)), _full((R, LANES))] + _mla_proj_weights_specs(w)
    shapes = [((R, H * KV_LORA), BF16), ((R, H * QK_ROPE), BF16), ((R, KV_LORA), F32),
              ((R, QK_ROPE), F32), ((R, KV_LORA), BF16), ((R, QK_ROPE), BF16)]
    return pl.pallas_call(
        functools.partial(_mla_proj_kernel, head_major=False),
        grid=(1,), in_specs=in_specs,
        out_specs=[_full(s) for s, _ in shapes],
        out_shape=[jax.ShapeDtypeStruct(s, d) for s, d in shapes],
        compiler_params=_cparams("arbitrary"), name="mla_proj_small",
    )(x, cos, sin, *w)


def _flash_kernel(qlat_ref, qpe_ref, klat_ref, kpe_ref, kmlat_ref, kmpe_ref, x_ref, wuv_ref, wo_ref,
                  gpost_ref, o_ref, m_sc, l_sc, acc_sc, ocat_sc, *, tq, hg):
    i = pl.program_id(1)
    ng = MLA_HEADS // hg
    rows = hg * tq

    def q_group(g):
        ql = qlat_ref[0, g * hg:(g + 1) * hg].reshape(rows, KV_LORA)
        qp = qpe_ref[0, g * hg:(g + 1) * hg].reshape(rows, QK_ROPE)
        return ql, qp

    def update(g, kl, kp, mask, first):
        ql, qp = q_group(g)
        s = _dot_nt(ql, kl) + _dot_nt(qp, kp)
        if mask is not None:
            s = jnp.where(mask, s, -jnp.inf)
        parts = [s[:, c * LANES:(c + 1) * LANES] for c in range(s.shape[1] // LANES)]
        smax = functools.reduce(jnp.maximum, parts)
        mrow = jnp.max(smax, axis=-1, keepdims=True)
        if first:
            m_new = jnp.broadcast_to(mrow, (rows, LANES))
        else:
            m_old = m_sc[g]
            m_new = jnp.maximum(m_old, mrow)
        ps = [jnp.exp(part - m_new) for part in parts]
        psum = functools.reduce(jnp.add, ps)
        p = ps[0] if len(ps) == 1 else jnp.concatenate(ps, axis=1)
        pv = _dot(p.astype(BF16), kl)
        if first:
            l_sc[g] = psum
            acc_sc[g] = pv
        else:
            alpha = jnp.exp(m_old - m_new)
            l_sc[g] = alpha * l_sc[g] + psum
            acc_sc[g] = jnp.concatenate([alpha] * (KV_LORA // LANES), axis=1) * acc_sc[g] + pv
        m_sc[g] = m_new

    nm = kmlat_ref.shape[0]
    meta_mask = lax.broadcasted_iota(jnp.int32, (rows, nm), 1) < N_META
    kml = kmlat_ref[...]
    kmp = kmpe_ref[...]
    for g in range(ng):
        update(g, kml, kmp, meta_mask, True)

    def body(j, carry):
        off = pl.multiple_of(j * 2 * tq, 2 * tq)
        kl = klat_ref[0, pl.ds(off, 2 * tq), :]
        kp = kpe_ref[0, pl.ds(off, 2 * tq), :]
        for g in range(ng):
            update(g, kl, kp, None, False)
        return carry

    lax.fori_loop(0, i // 2, body, 0)

    def causal_mask(width, shift):
        row_t = lax.broadcasted_iota(jnp.int32, (hg, tq, width), 1).reshape(rows, width)
        col = lax.broadcasted_iota(jnp.int32, (rows, width), 1)
        return col <= row_t + shift

    @pl.when(i % 2 == 1)
    def _():
        off = pl.multiple_of((i - 1) * tq, tq)
        kl = klat_ref[0, pl.ds(off, 2 * tq), :]
        kp = kpe_ref[0, pl.ds(off, 2 * tq), :]
        causal = causal_mask(2 * tq, tq)
        for g in range(ng):
            update(g, kl, kp, causal, False)

    @pl.when(i % 2 == 0)
    def _():
        off = pl.multiple_of(i * tq, tq)
        kl = klat_ref[0, pl.ds(off, tq), :]
        kp = kpe_ref[0, pl.ds(off, tq), :]
        causal = causal_mask(tq, 0)
        for g in range(ng):
            update(g, kl, kp, causal, False)

    for g in range(ng):
        o = acc_sc[g] / jnp.sum(l_sc[g], axis=-1, keepdims=True)
        for hh in range(hg):
            hd = g * hg + hh
            oh = _dot(o[hh * tq:(hh + 1) * tq].astype(BF16), wuv_ref[hd])
            ocat_sc[:, hd * V_HEAD:(hd + 1) * V_HEAD] = oh
    m = _dot(ocat_sc[...].astype(BF16), wo_ref[...])
    o_ref[0] = x_ref[0] + _rms(m, gpost_ref[...])


def _flash_prompt(qlat, qpe, klat, kpe, kmlat, kmpe, x, wuv, wo, gpost, tq, hg):
    B, H, T, C = qlat.shape
    D = x.shape[-1]
    rows = hg * tq
    ng = H // hg
    in_specs = [pl.BlockSpec((1, H, tq, C), lambda b, i: (b, 0, i, 0)),
                pl.BlockSpec((1, H, tq, QK_ROPE), lambda b, i: (b, 0, i, 0)),
                pl.BlockSpec((1, T, C), lambda b, i: (b, 0, 0)),
                pl.BlockSpec((1, T, QK_ROPE), lambda b, i: (b, 0, 0)),
                _full(kmlat.shape), _full(kmpe.shape),
                pl.BlockSpec((1, tq, D), lambda b, i: (b, i, 0)),
                _full(wuv.shape), _full(wo.shape), _full(gpost.shape)]
    return pl.pallas_call(
        functools.partial(_flash_kernel, tq=tq, hg=hg),
        grid=(B, T // tq), in_specs=in_specs,
        out_specs=pl.BlockSpec((1, tq, D), lambda b, i: (b, i, 0)),
        out_shape=jax.ShapeDtypeStruct((B, T, D), F32),
        scratch_shapes=[pltpu.VMEM((ng, rows, LANES), F32), pltpu.VMEM((ng, rows, LANES), F32),
                        pltpu.VMEM((ng, rows, C), F32), pltpu.VMEM((tq, H * V_HEAD), F32)],
        compiler_params=_cparams("arbitrary", "arbitrary"), name="flash_prompt",
    )(qlat, qpe, klat, kpe, kmlat, kmpe, x, wuv, wo, gpost)


def _meta_attn_kernel(qlat_ref, qpe_ref, kl_ref, kp_ref, o_ref):
    kl = kl_ref[...]
    s = _dot_nt(qlat_ref[...], kl) + _dot_nt(qpe_ref[...], kp_ref[...])
    r, n = s.shape
    tok = lax.broadcasted_iota(jnp.int32, (r // MLA_HEADS, MLA_HEADS, n), 0).reshape(r, n)
    col = lax.broadcasted_iota(jnp.int32, (r, n), 1)
    s = jnp.where(col <= tok, s, -jnp.inf)
    p = jnp.exp(s - jnp.max(s, axis=-1, keepdims=True))
    l = jnp.sum(p, axis=-1, keepdims=True)
    o_ref[...] = _dot(p.astype(BF16), kl) / l


def _meta_attn(qlat, qpe, kl, kp):
    r = qlat.shape[0]
    return pl.pallas_call(
        _meta_attn_kernel, grid=(1,),
        in_specs=[_full(qlat.shape), _full(qpe.shape), _full(kl.shape), _full(kp.shape)],
        out_specs=_full((r, KV_LORA)), out_shape=jax.ShapeDtypeStruct((r, KV_LORA), F32),
        compiler_params=_cparams("arbitrary"), name="meta_attn",
    )(qlat, qpe, kl, kp)


def _decode_kernel(pt_ref, qlat_ref, qpe_ref, *refs, pages, n_new):
    ckv_refs = refs[:pages]
    kpt_refs = refs[pages:2 * pages]
    knl_ref, knp_ref, o_ref, kl_sc, kpt_sc, m_sc, l_sc, acc_sc = refs[2 * pages:]
    j = pl.program_id(1)
    page = ckv_refs[0].shape[0]
    ql = qlat_ref[...]
    qp = qpe_ref[...]
    rq = ql.shape[0]

    @pl.when(j == 0)
    def _():
        m_sc[...] = jnp.full(m_sc.shape, -jnp.inf, F32)
        l_sc[...] = jnp.zeros(l_sc.shape, F32)
        acc_sc[...] = jnp.zeros(acc_sc.shape, F32)

    def accumulate(kl, s):
        parts = [s[:, c * LANES:(c + 1) * LANES] for c in range(s.shape[1] // LANES)]
        m_old = m_sc[...]
        m_new = jnp.maximum(m_old, jnp.max(functools.reduce(jnp.maximum, parts), axis=-1, keepdims=True))
        alpha = jnp.exp(m_old - m_new)
        ps = [jnp.exp(part - m_new) for part in parts]
        p = ps[0] if len(ps) == 1 else jnp.concatenate(ps, axis=1)
        l_sc[...] = alpha * l_sc[...] + functools.reduce(jnp.add, ps)
        acc_sc[...] = (jnp.concatenate([alpha] * (KV_LORA // LANES), axis=1) * acc_sc[...]
                       + _dot(p.astype(BF16), kl))
        m_sc[...] = m_new

    for p in range(pages):
        kl_sc[p * page:(p + 1) * page, :] = ckv_refs[p][...].astype(BF16)
        kpt_sc[:, p * page:(p + 1) * page] = kpt_refs[p][...].astype(BF16)
    kl = kl_sc[...]
    accumulate(kl, _dot_nt(ql, kl) + _dot(qp, kpt_sc[...]))

    @pl.when(j == pl.num_programs(1) - 1)
    def _():
        kn = knl_ref[0]
        s = _dot_nt(ql, kn) + _dot_nt(qp, knp_ref[0])
        n = s.shape[1]
        tok = lax.broadcasted_iota(jnp.int32, (n_new, rq // n_new, n), 0).reshape(rq, n)
        col = lax.broadcasted_iota(jnp.int32, (rq, n), 1)
        accumulate(kn, jnp.where(col <= tok, s, -jnp.inf))
        o_ref[...] = acc_sc[...] / jnp.sum(l_sc[...], axis=-1, keepdims=True)


def _decode_attn(page_table, qlat, qpe, cache_ckv, cache_kpt, layer, knl, knp, pages):
    nseq, npages = page_table.shape
    page = cache_ckv.shape[2]
    n_new = qlat.shape[0] // (nseq * MLA_HEADS)
    rq = n_new * MLA_HEADS
    steps = npages // pages
    pt = page_table.reshape(-1)

    def cache_spec(shape, p):
        return pl.BlockSpec((None, None) + shape,
                            lambda s, j, pt_ref: (layer, pt_ref[s * npages + j * pages + p], 0, 0))

    in_specs = ([pl.BlockSpec((rq, KV_LORA), lambda s, j, pt_ref: (s, 0)),
                 pl.BlockSpec((rq, QK_ROPE), lambda s, j, pt_ref: (s, 0))]
                + [cache_spec((page, KV_LORA), p) for p in range(pages)]
                + [cache_spec((QK_ROPE, page), p) for p in range(pages)]
                + [pl.BlockSpec((1,) + knl.shape[1:], lambda s, j, pt_ref: (s, 0, 0)),
                   pl.BlockSpec((1,) + knp.shape[1:], lambda s, j, pt_ref: (s, 0, 0))])
    grid_spec = pltpu.PrefetchScalarGridSpec(
        num_scalar_prefetch=1, grid=(nseq, steps), in_specs=in_specs,
        out_specs=pl.BlockSpec((rq, KV_LORA), lambda s, j, pt_ref: (s, 0)),
        scratch_shapes=[pltpu.VMEM((pages * page, KV_LORA), BF16), pltpu.VMEM((QK_ROPE, pages * page), BF16),
                        pltpu.VMEM((rq, LANES), F32), pltpu.VMEM((rq, LANES), F32),
                        pltpu.VMEM((rq, KV_LORA), F32)])
    return pl.pallas_call(
        functools.partial(_decode_kernel, pages=pages, n_new=n_new),
        grid_spec=grid_spec,
        out_shape=jax.ShapeDtypeStruct((nseq * rq, KV_LORA), F32),
        compiler_params=_cparams("arbitrary", "arbitrary"), name="decode_attn",
    )(pt, qlat, qpe, *([cache_ckv] * pages), *([cache_kpt] * pages), knl, knp)


def _mla_out_kernel(o_ref, x_ref, wuv_ref, wo_ref, gpost_ref, y_ref, ocat_sc):
    for hd in range(MLA_HEADS):
        oh = _dot(o_ref[:, hd * KV_LORA:(hd + 1) * KV_LORA].astype(BF16), wuv_ref[hd])
        ocat_sc[:, hd * V_HEAD:(hd + 1) * V_HEAD] = oh
    m = _dot(ocat_sc[...].astype(BF16), wo_ref[...])
    y_ref[...] = x_ref[...] + _rms(m, gpost_ref[...])


def _mla_out_small(o, x, wuv, wo, gpost):
    R, D = x.shape
    return pl.pallas_call(
        _mla_out_kernel, grid=(1,),
        in_specs=[_full(o.shape), _full(x.shape), _full(wuv.shape), _full(wo.shape), _full(gpost.shape)],
        out_specs=_full((R, D)), out_shape=jax.ShapeDtypeStruct((R, D), F32),
        scratch_shapes=[pltpu.VMEM((R, MLA_HEADS * V_HEAD), F32)],
        compiler_params=_cparams("arbitrary"), name="mla_out_small",
    )(o, x, wuv, wo, gpost)


def _mlp_kernel(x_ref, gpre_ref, wup_ref, wdn_ref, gpost_ref, o_ref, *, fc):
    x = x_ref[...]
    h = _rms(x, gpre_ref[...]).astype(BF16)
    dff = wup_ref.shape[1]
    acc = None
    for c in range(dff // fc):
        u = _dot(h, wup_ref[:, c * fc:(c + 1) * fc])
        u = jnp.square(jnp.maximum(u, 0.0)).astype(BF16)
        d = _dot(u, wdn_ref[c * fc:(c + 1) * fc, :])
        acc = d if acc is None else acc + d
    o_ref[...] = x + _rms(acc, gpost_ref[...])


def _mlp(x, gpre, wup, wdn, gpost, tm, fc=1024):
    N, D = x.shape
    const = lambda a: pl.BlockSpec(a.shape, lambda i: (0, 0), pipeline_mode=pl.Buffered(1))
    return pl.pallas_call(
        functools.partial(_mlp_kernel, fc=fc), grid=(N // tm,),
        in_specs=[pl.BlockSpec((tm, D), lambda i: (i, 0)), const(gpre), const(wup), const(wdn), const(gpost)],
        out_specs=pl.BlockSpec((tm, D), lambda i: (i, 0)),
        out_shape=jax.ShapeDtypeStruct((N, D), F32),
        compiler_params=_cparams("arbitrary"), name="mlp",
    )(x, gpre, wup, wdn, gpost)


def _gla_proj_kernel(x_ref, gpre_ref, win_ref, wg_ref, bg_ref, q_ref, k_ref, v_ref, r_ref, la_ref):
    hk = GLA_HEADS * GLA_DK
    hv = GLA_HEADS * GLA_DV
    h = _rms(x_ref[...], gpre_ref[...]).astype(BF16)
    a = _dot(h, win_ref[...])
    q_ref[...] = a[:, :hk] * GLA_SCALE
    k_ref[...] = a[:, hk:2 * hk]
    v_ref[...] = a[:, 2 * hk:2 * hk + hv]
    r_ref[...] = a[:, 2 * hk + hv:2 * hk + 2 * hv]
    gd = a[:, 2 * hk + 2 * hv:].astype(BF16)
    z = _dot(gd, wg_ref[...]) + bg_ref[...]
    la_ref[...] = (jnp.minimum(z, 0.0) - jnp.log(1.0 + jnp.exp(-jnp.abs(z)))) * (1.0 / GATE_TAU)


def _gla_proj(x, gpre, win, wg, bg, tr):
    N, D = x.shape
    hk = GLA_HEADS * GLA_DK
    hv = GLA_HEADS * GLA_DV
    row = lambda w: pl.BlockSpec((tr, w), lambda i: (i, 0))
    return pl.pallas_call(
        _gla_proj_kernel, grid=(N // tr,),
        in_specs=[row(D), _full(gpre.shape), _full(win.shape), _full(wg.shape), _full(bg.shape)],
        out_specs=[row(hk), row(hk), row(hv), row(hv), row(hk)],
        out_shape=[jax.ShapeDtypeStruct((N, w), F32) for w in (hk, hk, hv, hv, hk)],
        compiler_params=_cparams("arbitrary"), name="gla_proj",
    )(x, gpre, win, wg, bg)


def _gla_gate(o, r, gn):
    return _rms(o, gn) * (r / (1.0 + jnp.exp(-r)))


def _gla_prompt_kernel(q_ref, k_ref, v_ref, r_ref, la_ref, x_ref, s0_ref, gn_ref, wo_ref, gpost_ref,
                       y_ref, sfin_ref, s_sc, ocat_sc, *, rows):
    c_sz = GLA_CHUNK

    @pl.when(pl.program_id(1) == 0)
    def _():
        s_sc[...] = s0_ref[...]

    ri = lax.broadcasted_iota(jnp.int32, (c_sz, c_sz), 0)
    ci = lax.broadcasted_iota(jnp.int32, (c_sz, c_sz), 1)
    tril = ci <= ri
    ltri = tril.astype(F32)
    gn = gn_ref[...]
    for c in range(rows // c_sz):
        sl = slice(c * c_sz, (c + 1) * c_sz)
        la = la_ref[0, sl, :]
        b = jnp.dot(ltri, la, precision=lax.Precision.HIGHEST, preferred_element_type=F32)
        b_last = b[c_sz - 1:c_sz, :]
        q_in = (q_ref[0, sl, :] * jnp.exp(b)).astype(BF16)
        k = k_ref[0, sl, :]
        k_in = (k * jnp.exp(-b)).astype(BF16)
        k_dec = (k * jnp.exp(b_last - b)).astype(BF16)
        dec = jnp.exp(b_last)
        for hd in range(GLA_HEADS):
            ks = slice(hd * GLA_DK, (hd + 1) * GLA_DK)
            vs = slice(hd * GLA_DV, (hd + 1) * GLA_DV)
            v = v_ref[0, sl, vs].astype(BF16)
            a = jnp.where(tril, _dot_nt(q_in[:, ks], k_in[:, ks]), 0.0)
            s_old = s_sc[hd]
            o = _dot(a.astype(BF16), v) + _dot(q_in[:, ks], s_old.astype(BF16))
            dcol = jnp.transpose(jnp.broadcast_to(dec[:, ks], (GLA_DK, GLA_DK)))
            dfull = jnp.concatenate([dcol] * (GLA_DV // GLA_DK), axis=1)
            s_sc[hd] = dfull * s_old + _dot_tn(k_dec[:, ks], v)
            ocat_sc[sl, vs] = _gla_gate(o, r_ref[0, sl, vs], gn)
    m = _dot(ocat_sc[...].astype(BF16), wo_ref[...])
    y_ref[0] = x_ref[0] + _rms(m, gpost_ref[...])

    @pl.when(pl.program_id(1) == pl.num_programs(1) - 1)
    def _():
        sfin_ref[0] = s_sc[...]


def _gla_prompt(q, k, v, r, la, x, s0, gn, wo, gpost, rows):
    B, T, D = x.shape
    hk = GLA_HEADS * GLA_DK
    hv = GLA_HEADS * GLA_DV
    blk = lambda w: pl.BlockSpec((1, rows, w), lambda b, i: (b, i, 0))
    return pl.pallas_call(
        functools.partial(_gla_prompt_kernel, rows=rows), grid=(B, T // rows),
        in_specs=[blk(hk), blk(hk), blk(hv), blk(hv), blk(hk), blk(D), _full(s0.shape), _full(gn.shape),
                  _full(wo.shape), _full(gpost.shape)],
        out_specs=[blk(D), pl.BlockSpec((1,) + s0.shape, lambda b, i: (b, 0, 0, 0))],
        out_shape=[jax.ShapeDtypeStruct((B, T, D), F32), jax.ShapeDtypeStruct((B,) + s0.shape, F32)],
        scratch_shapes=[pltpu.VMEM(s0.shape, F32), pltpu.VMEM((rows, hv), F32)],
        compiler_params=_cparams("arbitrary", "arbitrary"), name="gla_prompt",
    )(q, k, v, r, la, x, s0, gn, wo, gpost)


def _gla_tokens_kernel(qt_ref, kt_ref, lat_ref, v_ref, s0_ref, o_ref, sfin_ref, *, ntok):
    for hd in range(GLA_HEADS):
        ks = slice(hd * GLA_DK, (hd + 1) * GLA_DK)
        vs = slice(hd * GLA_DV, (hd + 1) * GLA_DV)
        s = s0_ref[0, hd]
        for t in range(ntok):
            a = jnp.exp(lat_ref[0, ks, t:t + 1])
            s = a * s + kt_ref[0, ks, t:t + 1] * v_ref[0, t:t + 1, vs]
            o_ref[0, t:t + 1, vs] = jnp.sum(qt_ref[0, ks, t:t + 1] * s, axis=0, keepdims=True)
        sfin_ref[0, hd] = s


def _gla_tokens(qt, kt, lat, v, s0):
    nseq, hk, ntok = qt.shape
    hv = v.shape[-1]
    col = pl.BlockSpec((1, hk, ntok), lambda s: (s, 0, 0))
    st = pl.BlockSpec((1,) + s0.shape[1:], lambda s: (s, 0, 0, 0))
    return pl.pallas_call(
        functools.partial(_gla_tokens_kernel, ntok=ntok), grid=(nseq,),
        in_specs=[col, col, col, pl.BlockSpec((1, ntok, hv), lambda s: (s, 0, 0)), st],
        out_specs=[pl.BlockSpec((1, ntok, hv), lambda s: (s, 0, 0)), st],
        out_shape=[jax.ShapeDtypeStruct((nseq, ntok, hv), F32), jax.ShapeDtypeStruct(s0.shape, F32)],
        compiler_params=_cparams("arbitrary"), name="gla_tokens",
    )(qt, kt, lat, v, s0)


def _gla_out_kernel(o_ref, r_ref, x_ref, gn_ref, wo_ref, gpost_ref, y_ref, ocat_sc):
    gn = gn_ref[...]
    for hd in range(GLA_HEADS):
        vs = slice(hd * GLA_DV, (hd + 1) * GLA_DV)
        ocat_sc[:, vs] = _gla_gate(o_ref[:, vs], r_ref[:, vs], gn)
    m = _dot(ocat_sc[...].astype(BF16), wo_ref[...])
    y_ref[...] = x_ref[...] + _rms(m, gpost_ref[...])


def _gla_out_small(o, r, x, gn, wo, gpost):
    R, D = x.shape
    return pl.pallas_call(
        _gla_out_kernel, grid=(1,),
        in_specs=[_full(o.shape), _full(r.shape), _full(x.shape), _full(gn.shape), _full(wo.shape),
                  _full(gpost.shape)],
        out_specs=_full((R, D)), out_shape=jax.ShapeDtypeStruct((R, D), F32),
        scratch_shapes=[pltpu.VMEM(o.shape, F32)],
        compiler_params=_cparams("arbitrary"), name="gla_out_small",
    )(o, r, x, gn, wo, gpost)


def _rope_tables(pos):
    half = QK_ROPE // 2
    inv = ROPE_BASE ** (-jnp.arange(half, dtype=F32) / half)
    ang = pos.astype(F32)[:, None] * inv[None, :]
    c, s = jnp.cos(ang), jnp.sin(ang)
    reps = LANES // QK_ROPE
    return (jnp.tile(jnp.concatenate([c, c], axis=1), (1, reps)),
            jnp.tile(jnp.concatenate([-s, s], axis=1), (1, reps)))


def _swap_halves(w):
    half = QK_ROPE // 2
    return jnp.concatenate([w[..., half:], w[..., :half]], axis=-1)


def kernel(x_prompt, x_sample, cache_ckv, cache_kpe, state_gla, page_table, meta_tokens, norm_pre_mix, norm_post_mix, norm_pre_mlp, norm_post_mlp, mla_w_in, mla_q_norm, mla_w_uq, mla_kv_norm, mla_w_uk, mla_w_uv, mla_w_o, gla_w_in, gla_w_gate, gla_b_gate, gla_norm, gla_w_o, mlp_w_up, mlp_w_down):
    B, T, D = x_prompt.shape
    DB, S, _ = x_sample.shape
    H = MLA_HEADS
    n_s = DB * S
    past_len = page_table.shape[1] * cache_ckv.shape[2]
    row = lambda a: a.reshape(1, -1)

    w_in = mla_w_in[0]
    o = Q_LORA + KV_LORA
    w_kpe = w_in[:, o:]
    zpad = jnp.zeros((D, LANES - QK_ROPE), F32)
    w_in_p = jnp.concatenate([w_in[:, :o], w_kpe, zpad, _swap_halves(w_kpe), zpad], axis=1).astype(BF16)
    w_uq = mla_w_uq[0].reshape(Q_LORA, H, QK_NOPE + QK_ROPE)
    w_nope = jnp.transpose(w_uq[:, :, :QK_NOPE], (1, 0, 2))
    w_uk = jnp.transpose(mla_w_uk[0], (1, 0, 2))
    w_qlat = _fold_qk(w_nope, w_uk)
    w_qpe = w_uq[:, :, QK_NOPE:]
    w_qp = w_qpe.reshape(Q_LORA, H * QK_ROPE).astype(BF16)
    w_qps = _swap_halves(w_qpe).reshape(Q_LORA, H * QK_ROPE).astype(BF16)
    proj_w = (row(norm_pre_mix[0]), w_in_p, row(mla_q_norm[0]), w_qlat, w_qp, w_qps, row(mla_kv_norm[0]))
    w_uv = jnp.transpose(mla_w_uv[0], (1, 0, 2)).astype(BF16)
    w_o = mla_w_o[0].astype(BF16)
    g_post0 = row(norm_post_mix[0])

    x_small = jnp.concatenate([x_sample.reshape(n_s, D), meta_tokens], axis=0)
    pos_small = jnp.concatenate([past_len + jnp.tile(jnp.arange(S, dtype=jnp.int32), DB),
                                 jnp.arange(N_META, dtype=jnp.int32)])
    cos_s, sin_s = _rope_tables(pos_small)
    qlat_s, qpe_s, ckv_s, kpe_s, klat_s, kpeb_s = _mla_proj_small(x_small, cos_s, sin_s, proj_w)

    npad = LANES
    kml = jnp.pad(klat_s[n_s:], ((0, npad - N_META), (0, 0)))
    kmp = jnp.pad(kpeb_s[n_s:], ((0, npad - N_META), (0, 0)))
    o_meta = _meta_attn(qlat_s[n_s:].reshape(N_META * H, KV_LORA), qpe_s[n_s:].reshape(N_META * H, QK_ROPE),
                        kml, kmp)
    knl = jnp.pad(klat_s[:n_s].reshape(DB, S, KV_LORA), ((0, 0), (0, npad - S), (0, 0)))
    knp = jnp.pad(kpeb_s[:n_s].reshape(DB, S, QK_ROPE), ((0, 0), (0, npad - S), (0, 0)))
    o_samp = _decode_attn(page_table, qlat_s[:n_s].reshape(n_s * H, KV_LORA),
                          qpe_s[:n_s].reshape(n_s * H, QK_ROPE), cache_ckv, jnp.swapaxes(cache_kpe, 2, 3),
                          0, knl, knp, pages=16)
    o_small = jnp.concatenate([o_samp.reshape(n_s, H * KV_LORA), o_meta.reshape(N_META, H * KV_LORA)], axis=0)
    xs1 = _mla_out_small(o_small, x_small, w_uv, w_o, g_post0)

    cos_p, sin_p = _rope_tables(N_META + jnp.arange(T, dtype=jnp.int32))
    qlat_p, qpe_p, ckv_p, kpe_p, klat_p, kpeb_p = _mla_proj_prompt(x_prompt, cos_p, sin_p, proj_w, tr=256)
    xp1 = _flash_prompt(qlat_p, qpe_p, klat_p, kpeb_p, kml, kmp, x_prompt, w_uv, w_o, g_post0, tq=256, hg=4)

    wup0, wdn0 = mlp_w_up[0].astype(BF16), mlp_w_down[0].astype(BF16)
    mlp0 = (row(norm_pre_mlp[0]), wup0, wdn0, row(norm_post_mlp[0]))
    xp2 = _mlp(xp1.reshape(B * T, D), *mlp0, tm=512)
    xs2 = _mlp(xs1, *mlp0, tm=xs1.shape[0])

    hk = GLA_HEADS * GLA_DK
    hv = GLA_HEADS * GLA_DV
    gw = gla_w_in[0]
    gw_p = jnp.concatenate([gw, jnp.zeros((D, LANES - GATE_RANK), F32)], axis=1).astype(BF16)
    wg_p = jnp.concatenate([gla_w_gate[0], jnp.zeros((LANES - GATE_RANK, hk), F32)], axis=0).astype(BF16)
    gla_pw = (row(norm_pre_mix[1]), gw_p, wg_p, row(gla_b_gate[0]))
    gn = row(gla_norm[0])
    gwo = gla_w_o[0].astype(BF16)
    g_post1 = row(norm_post_mix[1])

    q_s, k_s, v_s, r_s, la_s = _gla_proj(xs2, *gla_pw, tr=xs2.shape[0])
    tcol = lambda a, n, l: jnp.transpose(a.reshape(n, l, hk), (0, 2, 1))
    o_gs, s_gs = _gla_tokens(tcol(q_s[:n_s], DB, S), tcol(k_s[:n_s], DB, S), tcol(la_s[:n_s], DB, S),
                             v_s[:n_s].reshape(DB, S, hv), state_gla[0])
    zero_state = jnp.zeros((1, GLA_HEADS, GLA_DK, GLA_DV), F32)
    o_gm, s_gm = _gla_tokens(tcol(q_s[n_s:], 1, N_META), tcol(k_s[n_s:], 1, N_META), tcol(la_s[n_s:], 1, N_META),
                             v_s[n_s:].reshape(1, N_META, hv), zero_state)
    o_gsmall = jnp.concatenate([o_gs.reshape(n_s, hv), o_gm.reshape(N_META, hv)], axis=0)
    xs3 = _gla_out_small(o_gsmall, r_s, xs2, gn, gwo, g_post1)

    q_p, k_p, v_p, r_p, la_p = _gla_proj(xp2, *gla_pw, tr=512)
    r3 = lambda a: a.reshape(B, T, a.shape[-1])
    xp3, s_gp = _gla_prompt(r3(q_p), r3(k_p), r3(v_p), r3(r_p), r3(la_p), xp2.reshape(B, T, D), s_gm[0],
                            gn, gwo, g_post1, rows=256)

    mlp1 = (row(norm_pre_mlp[1]), mlp_w_up[1].astype(BF16), mlp_w_down[1].astype(BF16), row(norm_post_mlp[1]))
    y_prompt = _mlp(xp3.reshape(B * T, D), *mlp1, tm=512).reshape(B, T, D)
    xs4 = _mlp(xs3, *mlp1, tm=xs3.shape[0])
    y_sample = xs4[:n_s].reshape(DB, S, D)

    bmeta = lambda a: jnp.broadcast_to(a[None], (B,) + a.shape)
    new_ckv_prompt = jnp.concatenate([bmeta(ckv_s[n_s:]), ckv_p], axis=1)[None]
    new_kpe_prompt = jnp.concatenate([bmeta(kpe_s[n_s:]), kpe_p], axis=1)[None]
    new_ckv_sample = ckv_s[:n_s].reshape(1, DB, S, KV_LORA)
    new_kpe_sample = kpe_s[:n_s].reshape(1, DB, S, QK_ROPE)
    return (y_prompt, y_sample, new_ckv_prompt, new_kpe_prompt, new_ckv_sample, new_kpe_sample,
            s_gp[None], s_gs[None])
```

```python
import functools

import jax
import jax.numpy as jnp
from jax import lax
from jax.experimental import pallas as pl
from jax.experimental.pallas import tpu as pltpu

F32 = jnp.float32
BF16 = jnp.bfloat16

N_META = 16
MLA_HEADS = 16
Q_LORA = 256
KV_LORA = 256
QK_NOPE = 64
QK_ROPE = 32
V_HEAD = 64
MLA_SCALE = (QK_NOPE + QK_ROPE) ** -0.5
ROPE_BASE = 10000.0
GLA_HEADS = 4
GLA_DK = 128
GLA_DV = 256
GLA_SCALE = GLA_DK ** -0.5
GATE_RANK = 16
GATE_TAU = 16.0
GLA_CHUNK = 64
EPS = 1e-6

LANES = 128
VMEM_LIMIT = 56 * 1024 * 1024


def _cparams(*sem):
    return pltpu.CompilerParams(dimension_semantics=sem, vmem_limit_bytes=VMEM_LIMIT)


def _rms(x, w):
    return x * lax.rsqrt(jnp.mean(x * x, axis=-1, keepdims=True) + EPS) * w


def _dot(a, b):
    return jnp.dot(a, b, preferred_element_type=F32)


def _dot_nt(a, b):
    return lax.dot_general(a, b, (((1,), (1,)), ((), ())), preferred_element_type=F32)


def _dot_tn(a, b):
    return lax.dot_general(a, b, (((0,), (0,)), ((), ())), preferred_element_type=F32)


def _full(shape):
    n = len(shape)
    return pl.BlockSpec(shape, lambda *_: (0,) * n)


def _fold_kernel(a_ref, b_ref, o_ref):
    o_ref[...] = lax.dot_general(a_ref[0], b_ref[0], (((1,), (1,)), ((), ())),
                                 precision=lax.Precision.HIGHEST,
                                 preferred_element_type=F32).astype(o_ref.dtype)


def _fold_qk(w_nope, w_uk):
    return pl.pallas_call(
        _fold_kernel,
        grid=(MLA_HEADS,),
        in_specs=[pl.BlockSpec((1, Q_LORA, QK_NOPE), lambda h: (h, 0, 0)),
                  pl.BlockSpec((1, KV_LORA, QK_NOPE), lambda h: (h, 0, 0))],
        out_specs=pl.BlockSpec((Q_LORA, KV_LORA), lambda h: (0, h)),
        out_shape=jax.ShapeDtypeStruct((Q_LORA, MLA_HEADS * KV_LORA), BF16),
        compiler_params=_cparams("arbitrary"),
        name="fold_qk",
    )(w_nope, w_uk)


def _mla_proj_kernel(x_ref, cos_ref, sin_ref, gpre_ref, win_ref, qn_ref, wql_ref, wqp_ref, wqps_ref,
                     kvn_ref, qlat_ref, qpe_ref, ckv_ref, kpe_ref, klat_ref, kpeb_ref, *, head_major):
    x = x_ref[0] if head_major else x_ref[...]
    h = _rms(x, gpre_ref[...]).astype(BF16)
    a = _dot(h, win_ref[...])
    cqn = _rms(a[:, :Q_LORA], qn_ref[...]).astype(BF16)
    ckv = _rms(a[:, Q_LORA:Q_LORA + KV_LORA], kvn_ref[...])
    cos = cos_ref[...]
    sin = sin_ref[...]
    o = Q_LORA + KV_LORA
    kpe = a[:, o:o + QK_ROPE] * cos[:, :QK_ROPE] + a[:, o + LANES:o + LANES + QK_ROPE] * sin[:, :QK_ROPE]
    qlat = _dot(cqn, wql_ref[...]) * MLA_SCALE
    qp = _dot(cqn, wqp_ref[...])
    qps = _dot(cqn, wqps_ref[...])
    nl = MLA_HEADS * QK_ROPE // LANES
    qpe = jnp.concatenate(
        [(qp[:, j * LANES:(j + 1) * LANES] * cos + qps[:, j * LANES:(j + 1) * LANES] * sin) * MLA_SCALE
         for j in range(nl)], axis=1)
    if head_major:
        for hd in range(MLA_HEADS):
            qlat_ref[0, hd] = qlat[:, hd * KV_LORA:(hd + 1) * KV_LORA].astype(BF16)
            qpe_ref[0, hd] = qpe[:, hd * QK_ROPE:(hd + 1) * QK_ROPE].astype(BF16)
        ckv_ref[0] = ckv
        kpe_ref[0] = kpe
        klat_ref[0] = ckv.astype(BF16)
        kpeb_ref[0] = kpe.astype(BF16)
    else:
        qlat_ref[...] = qlat.astype(BF16)
        qpe_ref[...] = qpe.astype(BF16)
        ckv_ref[...] = ckv
        kpe_ref[...] = kpe
        klat_ref[...] = ckv.astype(BF16)
        kpeb_ref[...] = kpe.astype(BF16)


def _mla_proj_weights_specs(w):
    return [_full(a.shape) for a in w]


def _mla_proj_prompt(x, cos, sin, w, tr):
    B, T, D = x.shape
    H = MLA_HEADS
    grid = (B, T // tr)
    in_specs = [pl.BlockSpec((1, tr, D), lambda b, i: (b, i, 0)),
                pl.BlockSpec((tr, LANES), lambda b, i: (i, 0)),
                pl.BlockSpec((tr, LANES), lambda b, i: (i, 0))] + _mla_proj_weights_specs(w)
    out_shape = [jax.ShapeDtypeStruct((B, H, T, KV_LORA), BF16),
                 jax.ShapeDtypeStruct((B, H, T, QK_ROPE), BF16),
                 jax.ShapeDtypeStruct((B, T, KV_LORA), F32),
                 jax.ShapeDtypeStruct((B, T, QK_ROPE), F32),
                 jax.ShapeDtypeStruct((B, T, KV_LORA), BF16),
                 jax.ShapeDtypeStruct((B, T, QK_ROPE), BF16)]
    out_specs = [pl.BlockSpec((1, H, tr, KV_LORA), lambda b, i: (b, 0, i, 0)),
                 pl.BlockSpec((1, H, tr, QK_ROPE), lambda b, i: (b, 0, i, 0)),
                 pl.BlockSpec((1, tr, KV_LORA), lambda b, i: (b, i, 0)),
                 pl.BlockSpec((1, tr, QK_ROPE), lambda b, i: (b, i, 0)),
                 pl.BlockSpec((1, tr, KV_LORA), lambda b, i: (b, i, 0)),
                 pl.BlockSpec((1, tr, QK_ROPE), lambda b, i: (b, i, 0))]
    return pl.pallas_call(
        functools.partial(_mla_proj_kernel, head_major=True),
        grid=grid, in_specs=in_specs, out_specs=out_specs, out_shape=out_shape,
        compiler_params=_cparams("arbitrary", "arbitrary"), name="mla_proj_prompt",
    )(x, cos, sin, *w)


def _mla_proj_small(x, cos, sin, w):
    R, D = x.shape
    H = MLA_HEADS
    in_specs = [_full((R, D)), _full((R, LANES)), _full((R, LANES))] + _mla_proj_weights_specs(w)
    shapes = [((R, H * KV_LORA), BF16), ((R, H * QK_ROPE), BF16), ((R, KV_LORA), F32),
              ((R, QK_ROPE), F32), ((R, KV_LORA), BF16), ((R, QK_ROPE), BF16)]
    return pl.pallas_call(
        functools.partial(_mla_proj_kernel, head_major=False),
        grid=(1,), in_specs=in_specs,
        out_specs=[_full(s) for s, _ in shapes],
        out_shape=[jax.ShapeDtypeStruct(s, d) for s, d in shapes],
        compiler_params=_cparams("arbitrary"), name="mla_proj_small",
    )(x, cos, sin, *w)


def _flash_kernel(qlat_ref, qpe_ref, klat_ref, kpe_ref, kmlat_ref, kmpe_ref, x_ref, wuv_ref, wo_ref,
                  gpost_ref, o_ref, m_sc, l_sc, acc_sc, ocat_sc, *, tq, hg):
    i = pl.program_id(1)
    ng = MLA_HEADS // hg
    rows = hg * tq

    def q_group(g):
        ql = qlat_ref[0, g * hg:(g + 1) * hg].reshape(rows, KV_LORA)
        qp = qpe_ref[0, g * hg:(g + 1) * hg].reshape(rows, QK_ROPE)
        return ql, qp

    def update(g, kl, kp, mask, first):
        ql, qp = q_group(g)
        s = _dot_nt(ql, kl) + _dot_nt(qp, kp)
        if mask is not None:
            s = jnp.where(mask, s, -jnp.inf)
        parts = [s[:, c * LANES:(c + 1) * LANES] for c in range(s.shape[1] // LANES)]
        smax = functools.reduce(jnp.maximum, parts)
        mrow = jnp.max(smax, axis=-1, keepdims=True)
        if first:
            m_new = jnp.broadcast_to(mrow, (rows, LANES))
        else:
            m_old = m_sc[g]
            m_new = jnp.maximum(m_old, mrow)
        ps = [jnp.exp(part - m_new) for part in parts]
        psum = functools.reduce(jnp.add, ps)
        p = ps[0] if len(ps) == 1 else jnp.concatenate(ps, axis=1)
        pv = _dot(p.astype(BF16), kl)
        if first:
            l_sc[g] = psum
            acc_sc[g] = pv
        else:
            alpha = jnp.exp(m_old - m_new)
            l_sc[g] = alpha * l_sc[g] + psum
            acc_sc[g] = jnp.concatenate([alpha] * (KV_LORA // LANES), axis=1) * acc_sc[g] + pv
        m_sc[g] = m_new

    nm = kmlat_ref.shape[0]
    meta_mask = lax.broadcasted_iota(jnp.int32, (rows, nm), 1) < N_META
    kml = kmlat_ref[...]
    kmp = kmpe_ref[...]
    for g in range(ng):
        update(g, kml, kmp, meta_mask, True)

    def body(j, carry):
        off = pl.multiple_of(j * 2 * tq, 2 * tq)
        kl = klat_ref[0, pl.ds(off, 2 * tq), :]
        kp = kpe_ref[0, pl.ds(off, 2 * tq), :]
        for g in range(ng):
            update(g, kl, kp, None, False)
        return carry

    lax.fori_loop(0, i // 2, body, 0)

    def causal_mask(width, shift):
        row_t = lax.broadcasted_iota(jnp.int32, (hg, tq, width), 1).reshape(rows, width)
        col = lax.broadcasted_iota(jnp.int32, (rows, width), 1)
        return col <= row_t + shift

    @pl.when(i % 2 == 1)
    def _():
        off = pl.multiple_of((i - 1) * tq, tq)
        kl = klat_ref[0, pl.ds(off, 2 * tq), :]
        kp = kpe_ref[0, pl.ds(off, 2 * tq), :]
        causal = causal_mask(2 * tq, tq)
        for g in range(ng):
            update(g, kl, kp, causal, False)

    @pl.when(i % 2 == 0)
    def _():
        off = pl.multiple_of(i * tq, tq)
        kl = klat_ref[0, pl.ds(off, tq), :]
        kp = kpe_ref[0, pl.ds(off, tq), :]
        causal = causal_mask(tq, 0)
        for g in range(ng):
            update(g, kl, kp, causal, False)

    for g in range(ng):
        o = acc_sc[g] / jnp.sum(l_sc[g], axis=-1, keepdims=True)
        for hh in range(hg):
            hd = g * hg + hh
            oh = _dot(o[hh * tq:(hh + 1) * tq].astype(BF16), wuv_ref[hd])
            ocat_sc[:, hd * V_HEAD:(hd + 1) * V_HEAD] = oh
    m = _dot(ocat_sc[...].astype(BF16), wo_ref[...])
    o_ref[0] = x_ref[0] + _rms(m, gpost_ref[...])


def _flash_prompt(qlat, qpe, klat, kpe, kmlat, kmpe, x, wuv, wo, gpost, tq, hg):
    B, H, T, C = qlat.shape
    D = x.shape[-1]
    rows = hg * tq
    ng = H // hg
    in_specs = [pl.BlockSpec((1, H, tq, C), lambda b, i: (b, 0, i, 0)),
                pl.BlockSpec((1, H, tq, QK_ROPE), lambda b, i: (b, 0, i, 0)),
                pl.BlockSpec((1, T, C), lambda b, i: (b, 0, 0)),
                pl.BlockSpec((1, T, QK_ROPE), lambda b, i: (b, 0, 0)),
                _full(kmlat.shape), _full(kmpe.shape),
                pl.BlockSpec((1, tq, D), lambda b, i: (b, i, 0)),
                _full(wuv.shape), _full(wo.shape), _full(gpost.shape)]
    return pl.pallas_call(
        functools.partial(_flash_kernel, tq=tq, hg=hg),
        grid=(B, T // tq), in_specs=in_specs,
        out_specs=pl.BlockSpec((1, tq, D), lambda b, i: (b, i, 0)),
        out_shape=jax.ShapeDtypeStruct((B, T, D), F32),
        scratch_shapes=[pltpu.VMEM((ng, rows, LANES), F32), pltpu.VMEM((ng, rows, LANES), F32),
                        pltpu.VMEM((ng, rows, C), F32), pltpu.VMEM((tq, H * V_HEAD), F32)],
        compiler_params=_cparams("arbitrary", "arbitrary"), name="flash_prompt",
    )(qlat, qpe, klat, kpe, kmlat, kmpe, x, wuv, wo, gpost)


def _meta_attn_kernel(qlat_ref, qpe_ref, kl_ref, kp_ref, o_ref):
    kl = kl_ref[...]
    s = _dot_nt(qlat_ref[...], kl) + _dot_nt(qpe_ref[...], kp_ref[...])
    r, n = s.shape
    tok = lax.broadcasted_iota(jnp.int32, (r // MLA_HEADS, MLA_HEADS, n), 0).reshape(r, n)
    col = lax.broadcasted_iota(jnp.int32, (r, n), 1)
    s = jnp.where(col <= tok, s, -jnp.inf)
    p = jnp.exp(s - jnp.max(s, axis=-1, keepdims=True))
    l = jnp.sum(p, axis=-1, keepdims=True)
    o_ref[...] = _dot(p.astype(BF16), kl) / l


def _meta_attn(qlat, qpe, kl, kp):
    r = qlat.shape[0]
    return pl.pallas_call(
        _meta_attn_kernel, grid=(1,),
        in_specs=[_full(qlat.shape), _full(qpe.shape), _full(kl.shape), _full(kp.shape)],
        out_specs=_full((r, KV_LORA)), out_shape=jax.ShapeDtypeStruct((r, KV_LORA), F32),
        compiler_params=_cparams("arbitrary"), name="meta_attn",
    )(qlat, qpe, kl, kp)


def _decode_kernel(pt_ref, qlat_ref, qpe_ref, knl_ref, knp_ref, ckv_hbm, kpt_hbm, o_ref,
                   ckv_buf, kpt_buf, sem, m_sc, l_sc, acc_sc, *, layer, npages, gp, nbuf, n_new):
    s = pl.program_id(0)
    nseq = pl.num_programs(0)
    ngroups = npages // gp
    page = ckv_buf.shape[1] // gp
    ql = qlat_ref[...]
    qp = qpe_ref[...]
    rq = ql.shape[0]

    def group_copies(seq, g):
        slot = g % nbuf
        cps = []
        for p in range(gp):
            pid = pt_ref[seq * npages + g * gp + p]
            cps.append(pltpu.make_async_copy(ckv_hbm.at[layer, pid],
                                             ckv_buf.at[slot, pl.ds(p * page, page), :], sem.at[slot]))
            cps.append(pltpu.make_async_copy(kpt_hbm.at[layer, pid],
                                             kpt_buf.at[slot, :, pl.ds(p * page, page)], sem.at[slot]))
        return cps

    @pl.when(s == 0)
    def _():
        for g in range(nbuf - 1):
            for cp in group_copies(0, g):
                cp.start()

    m_sc[...] = jnp.full(m_sc.shape, -jnp.inf, F32)
    l_sc[...] = jnp.zeros(l_sc.shape, F32)
    acc_sc[...] = jnp.zeros(acc_sc.shape, F32)

    def accumulate(kl, s):
        parts = [s[:, c * LANES:(c + 1) * LANES] for c in range(s.shape[1] // LANES)]
        m_old = m_sc[...]
        m_new = jnp.maximum(m_old, jnp.max(functools.reduce(jnp.maximum, parts), axis=-1, keepdims=True))
        alpha = jnp.exp(m_old - m_new)
        ps = [jnp.exp(part - m_new) for part in parts]
        p = ps[0] if len(ps) == 1 else jnp.concatenate(ps, axis=1)
        l_sc[...] = alpha * l_sc[...] + functools.reduce(jnp.add, ps)
        acc_sc[...] = (jnp.concatenate([alpha] * (KV_LORA // LANES), axis=1) * acc_sc[...]
                       + _dot(p.astype(BF16), kl))
        m_sc[...] = m_new

    def scores(g):
        for cp in group_copies(s, g):
            cp.wait()
        slot = g % nbuf
        kl = ckv_buf[slot].astype(BF16)
        return kl, _dot_nt(ql, kl) + _dot(qp, kpt_buf[slot].astype(BF16))

    cur = scores(0)
    for g in range(ngroups):
        nxt = g + nbuf - 1
        if nxt < ngroups:
            for cp in group_copies(s, nxt):
                cp.start()
        else:
            @pl.when(s + 1 < nseq)
            def _():
                for cp in group_copies(s + 1, nxt - ngroups):
                    cp.start()
        ahead = scores(g + 1) if g + 1 < ngroups else None
        accumulate(*cur)
        cur = ahead

    kn = knl_ref[0]
    sn = _dot_nt(ql, kn) + _dot_nt(qp, knp_ref[0])
    n = sn.shape[1]
    tok = lax.broadcasted_iota(jnp.int32, (n_new, rq // n_new, n), 0).reshape(rq, n)
    col = lax.broadcasted_iota(jnp.int32, (rq, n), 1)
    accumulate(kn, jnp.where(col <= tok, sn, -jnp.inf))
    o_ref[...] = acc_sc[...] / jnp.sum(l_sc[...], axis=-1, keepdims=True)


DECODE_GROUP_PAGES = 16
DECODE_RING_SLOTS = 4


def _decode_attn(page_table, qlat, qpe, cache_ckv, cache_kpt, layer, knl, knp):
    nseq, npages = page_table.shape
    page = cache_ckv.shape[2]
    n_new = qlat.shape[0] // (nseq * MLA_HEADS)
    rq = n_new * MLA_HEADS
    gp, nbuf = DECODE_GROUP_PAGES, DECODE_RING_SLOTS
    assert npages % gp == 0 and (npages // gp) % nbuf == 0
    pt = page_table.reshape(-1)
    in_specs = [pl.BlockSpec((rq, KV_LORA), lambda s, pt_ref: (s, 0)),
                pl.BlockSpec((rq, QK_ROPE), lambda s, pt_ref: (s, 0)),
                pl.BlockSpec((1,) + knl.shape[1:], lambda s, pt_ref: (s, 0, 0)),
                pl.BlockSpec((1,) + knp.shape[1:], lambda s, pt_ref: (s, 0, 0)),
                pl.BlockSpec(memory_space=pl.ANY), pl.BlockSpec(memory_space=pl.ANY)]
    grid_spec = pltpu.PrefetchScalarGridSpec(
        num_scalar_prefetch=1, grid=(nseq,), in_specs=in_specs,
        out_specs=pl.BlockSpec((rq, KV_LORA), lambda s, pt_ref: (s, 0)),
        scratch_shapes=[pltpu.VMEM((nbuf, gp * page, KV_LORA), F32), pltpu.VMEM((nbuf, QK_ROPE, gp * page), F32),
                        pltpu.SemaphoreType.DMA((nbuf,)),
                        pltpu.VMEM((rq, LANES), F32), pltpu.VMEM((rq, LANES), F32),
                        pltpu.VMEM((rq, KV_LORA), F32)])
    return pl.pallas_call(
        functools.partial(_decode_kernel, layer=layer, npages=npages, gp=gp, nbuf=nbuf, n_new=n_new),
        grid_spec=grid_spec,
        out_shape=jax.ShapeDtypeStruct((nseq * rq, KV_LORA), F32),
        compiler_params=_cparams("arbitrary"), name="decode_attn",
    )(pt, qlat, qpe, knl, knp, cache_ckv, cache_kpt)


def _mla_out_kernel(o_ref, x_ref, wuv_ref, wo_ref, gpost_ref, y_ref, ocat_sc):
    for hd in range(MLA_HEADS):
        oh = _dot(o_ref[:, hd * KV_LORA:(hd + 1) * KV_LORA].astype(BF16), wuv_ref[hd])
        ocat_sc[:, hd * V_HEAD:(hd + 1) * V_HEAD] = oh
    m = _dot(ocat_sc[...].astype(BF16), wo_ref[...])
    y_ref[...] = x_ref[...] + _rms(m, gpost_ref[...])


def _mla_out_small(o, x, wuv, wo, gpost):
    R, D = x.shape
    return pl.pallas_call(
        _mla_out_kernel, grid=(1,),
        in_specs=[_full(o.shape), _full(x.shape), _full(wuv.shape), _full(wo.shape), _full(gpost.shape)],
        out_specs=_full((R, D)), out_shape=jax.ShapeDtypeStruct((R, D), F32),
        scratch_shapes=[pltpu.VMEM((R, MLA_HEADS * V_HEAD), F32)],
        compiler_params=_cparams("arbitrary"), name="mla_out_small",
    )(o, x, wuv, wo, gpost)


def _mlp_kernel(x_ref, gpre_ref, wup_ref, wdn_ref, gpost_ref, o_ref, *, fc):
    x = x_ref[...]
    h = _rms(x, gpre_ref[...]).astype(BF16)
    dff = wup_ref.shape[1]
    acc = None
    for c in range(dff // fc):
        u = _dot(h, wup_ref[:, c * fc:(c + 1) * fc])
        u = jnp.square(jnp.maximum(u, 0.0)).astype(BF16)
        d = _dot(u, wdn_ref[c * fc:(c + 1) * fc, :])
        acc = d if acc is None else acc + d
    o_ref[...] = x + _rms(acc, gpost_ref[...])


def _mlp(x, gpre, wup, wdn, gpost, tm, fc=1024):
    N, D = x.shape
    const = lambda a: pl.BlockSpec(a.shape, lambda i: (0, 0), pipeline_mode=pl.Buffered(1))
    return pl.pallas_call(
        functools.partial(_mlp_kernel, fc=fc), grid=(N // tm,),
        in_specs=[pl.BlockSpec((tm, D), lambda i: (i, 0)), const(gpre), const(wup), const(wdn), const(gpost)],
        out_specs=pl.BlockSpec((tm, D), lambda i: (i, 0)),
        out_shape=jax.ShapeDtypeStruct((N, D), F32),
        compiler_params=_cparams("arbitrary"), name="mlp",
    )(x, gpre, wup, wdn, gpost)


def _gla_proj_kernel(x_ref, gpre_ref, win_ref, wg_ref, bg_ref, q_ref, k_ref, v_ref, r_ref, la_ref):
    hk = GLA_HEADS * GLA_DK
    hv = GLA_HEADS * GLA_DV
    h = _rms(x_ref[...], gpre_ref[...]).astype(BF16)
    a = _dot(h, win_ref[...])
    q_ref[...] = a[:, :hk] * GLA_SCALE
    k_ref[...] = a[:, hk:2 * hk]
    v_ref[...] = a[:, 2 * hk:2 * hk + hv]
    r_ref[...] = a[:, 2 * hk + hv:2 * hk + 2 * hv]
    gd = a[:, 2 * hk + 2 * hv:].astype(BF16)
    z = _dot(gd, wg_ref[...]) + bg_ref[...]
    la_ref[...] = (jnp.minimum(z, 0.0) - jnp.log(1.0 + jnp.exp(-jnp.abs(z)))) * (1.0 / GATE_TAU)


def _gla_proj(x, gpre, win, wg, bg, tr):
    N, D = x.shape
    hk = GLA_HEADS * GLA_DK
    hv = GLA_HEADS * GLA_DV
    row = lambda w: pl.BlockSpec((tr, w), lambda i: (i, 0))
    return pl.pallas_call(
        _gla_proj_kernel, grid=(N // tr,),
        in_specs=[row(D), _full(gpre.shape), _full(win.shape), _full(wg.shape), _full(bg.shape)],
        out_specs=[row(hk), row(hk), row(hv), row(hv), row(hk)],
        out_shape=[jax.ShapeDtypeStruct((N, w), F32) for w in (hk, hk, hv, hv, hk)],
        compiler_params=_cparams("arbitrary"), name="gla_proj",
    )(x, gpre, win, wg, bg)


def _gla_gate(o, r, gn):
    return _rms(o, gn) * (r / (1.0 + jnp.exp(-r)))


def _gla_prompt_kernel(q_ref, k_ref, v_ref, r_ref, la_ref, x_ref, s0_ref, gn_ref, wo_ref, gpost_ref,
                       y_ref, sfin_ref, s_sc, ocat_sc, *, rows):
    c_sz = GLA_CHUNK

    @pl.when(pl.program_id(1) == 0)
    def _():
        s_sc[...] = s0_ref[...]

    ri = lax.broadcasted_iota(jnp.int32, (c_sz, c_sz), 0)
    ci = lax.broadcasted_iota(jnp.int32, (c_sz, c_sz), 1)
    tril = ci <= ri
    ltri = tril.astype(F32)
    gn = gn_ref[...]
    for c in range(rows // c_sz):
        sl = slice(c * c_sz, (c + 1) * c_sz)
        la = la_ref[0, sl, :]
        b = jnp.dot(ltri, la, precision=lax.Precision.HIGHEST, preferred_element_type=F32)
        b_last = b[c_sz - 1:c_sz, :]
        q_in = (q_ref[0, sl, :] * jnp.exp(b)).astype(BF16)
        k = k_ref[0, sl, :]
        k_in = (k * jnp.exp(-b)).astype(BF16)
        k_dec = (k * jnp.exp(b_last - b)).astype(BF16)
        dec = jnp.exp(b_last)
        for hd in range(GLA_HEADS):
            ks = slice(hd * GLA_DK, (hd + 1) * GLA_DK)
            vs = slice(hd * GLA_DV, (hd + 1) * GLA_DV)
            v = v_ref[0, sl, vs].astype(BF16)
            a = jnp.where(tril, _dot_nt(q_in[:, ks], k_in[:, ks]), 0.0)
            s_old = s_sc[hd]
            o = _dot(a.astype(BF16), v) + _dot(q_in[:, ks], s_old.astype(BF16))
            dcol = jnp.transpose(jnp.broadcast_to(dec[:, ks], (GLA_DK, GLA_DK)))
            dfull = jnp.concatenate([dcol] * (GLA_DV // GLA_DK), axis=1)
            s_sc[hd] = dfull * s_old + _dot_tn(k_dec[:, ks], v)
            ocat_sc[sl, vs] = _gla_gate(o, r_ref[0, sl, vs], gn)
    m = _dot(ocat_sc[...].astype(BF16), wo_ref[...])
    y_ref[0] = x_ref[0] + _rms(m, gpost_ref[...])

    @pl.when(pl.program_id(1) == pl.num_programs(1) - 1)
    def _():
        sfin_ref[0] = s_sc[...]


def _gla_prompt(q, k, v, r, la, x, s0, gn, wo, gpost, rows):
    B, T, D = x.shape
    hk = GLA_HEADS * GLA_DK
    hv = GLA_HEADS * GLA_DV
    blk = lambda w: pl.BlockSpec((1, rows, w), lambda b, i: (b, i, 0))
    return pl.pallas_call(
        functools.partial(_gla_prompt_kernel, rows=rows), grid=(B, T // rows),
        in_specs=[blk(hk), blk(hk), blk(hv), blk(hv), blk(hk), blk(D), _full(s0.shape), _full(gn.shape),
                  _full(wo.shape), _full(gpost.shape)],
        out_specs=[blk(D), pl.BlockSpec((1,) + s0.shape, lambda b, i: (b, 0, 0, 0))],
        out_shape=[jax.ShapeDtypeStruct((B, T, D), F32), jax.ShapeDtypeStruct((B,) + s0.shape, F32)],
        scratch_shapes=[pltpu.VMEM(s0.shape, F32), pltpu.VMEM((rows, hv), F32)],
        compiler_params=_cparams("arbitrary", "arbitrary"), name="gla_prompt",
    )(q, k, v, r, la, x, s0, gn, wo, gpost)


def _gla_tokens_kernel(qt_ref, kt_ref, lat_ref, v_ref, s0_ref, o_ref, sfin_ref, *, ntok):
    for hd in range(GLA_HEADS):
        ks = slice(hd * GLA_DK, (hd + 1) * GLA_DK)
        vs = slice(hd * GLA_DV, (hd + 1) * GLA_DV)
        s = s0_ref[0, hd]
        for t in range(ntok):
            a = jnp.exp(lat_ref[0, ks, t:t + 1])
            s = a * s + kt_ref[0, ks, t:t + 1] * v_ref[0, t:t + 1, vs]
            o_ref[0, t:t + 1, vs] = jnp.sum(qt_ref[0, ks, t:t + 1] * s, axis=0, keepdims=True)
        sfin_ref[0, hd] = s


def _gla_tokens(qt, kt, lat, v, s0):
    nseq, hk, ntok = qt.shape
    hv = v.shape[-1]
    col = pl.BlockSpec((1, hk, ntok), lambda s: (s, 0, 0))
    st = pl.BlockSpec((1,) + s0.shape[1:], lambda s: (s, 0, 0, 0))
    return pl.pallas_call(
        functools.partial(_gla_tokens_kernel, ntok=ntok), grid=(nseq,),
        in_specs=[col, col, col, pl.BlockSpec((1, ntok, hv), lambda s: (s, 0, 0)), st],
        out_specs=[pl.BlockSpec((1, ntok, hv), lambda s: (s, 0, 0)), st],
        out_shape=[jax.ShapeDtypeStruct((nseq, ntok, hv), F32), jax.ShapeDtypeStruct(s0.shape, F32)],
        compiler_params=_cparams("arbitrary"), name="gla_tokens",
    )(qt, kt, lat, v, s0)


def _gla_out_kernel(o_ref, r_ref, x_ref, gn_ref, wo_ref, gpost_ref, y_ref, ocat_sc):
    gn = gn_ref[...]
    for hd in range(GLA_HEADS):
        vs = slice(hd * GLA_DV, (hd + 1) * GLA_DV)
        ocat_sc[:, vs] = _gla_gate(o_ref[:, vs], r_ref[:, vs], gn)
    m = _dot(ocat_sc[...].astype(BF16), wo_ref[...])
    y_ref[...] = x_ref[...] + _rms(m, gpost_ref[...])


def _gla_out_small(o, r, x, gn, wo, gpost):
    R, D = x.shape
    return pl.pallas_call(
        _gla_out_kernel, grid=(1,),
        in_specs=[_full(o.shape), _full(r.shape), _full(x.shape), _full(gn.shape), _full(wo.shape),
                  _full(gpost.shape)],
        out_specs=_full((R, D)), out_shape=jax.ShapeDtypeStruct((R, D), F32),
        scratch_shapes=[pltpu.VMEM(o.shape, F32)],
        compiler_params=_cparams("arbitrary"), name="gla_out_small",
    )(o, r, x, gn, wo, gpost)


def _rope_tables(pos):
    half = QK_ROPE // 2
    inv = ROPE_BASE ** (-jnp.arange(half, dtype=F32) / half)
    ang = pos.astype(F32)[:, None] * inv[None, :]
    c, s = jnp.cos(ang), jnp.sin(ang)
    reps = LANES // QK_ROPE
    return (jnp.tile(jnp.concatenate([c, c], axis=1), (1, reps)),
            jnp.tile(jnp.concatenate([-s, s], axis=1), (1, reps)))


def _swap_halves(w):
    half = QK_ROPE // 2
    return jnp.concatenate([w[..., half:], w[..., :half]], axis=-1)


def kernel(x_prompt, x_sample, cache_ckv, cache_kpe, state_gla, page_table, meta_tokens, norm_pre_mix, norm_post_mix, norm_pre_mlp, norm_post_mlp, mla_w_in, mla_q_norm, mla_w_uq, mla_kv_norm, mla_w_uk, mla_w_uv, mla_w_o, gla_w_in, gla_w_gate, gla_b_gate, gla_norm, gla_w_o, mlp_w_up, mlp_w_down):
    B, T, D = x_prompt.shape
    DB, S, _ = x_sample.shape
    H = MLA_HEADS
    n_s = DB * S
    past_len = page_table.shape[1] * cache_ckv.shape[2]
    row = lambda a: a.reshape(1, -1)

    w_in = mla_w_in[0]
    o = Q_LORA + KV_LORA
    w_kpe = w_in[:, o:]
    zpad = jnp.zeros((D, LANES - QK_ROPE), F32)
    w_in_p = jnp.concatenate([w_in[:, :o], w_kpe, zpad, _swap_halves(w_kpe), zpad], axis=1).astype(BF16)
    w_uq = mla_w_uq[0].reshape(Q_LORA, H, QK_NOPE + QK_ROPE)
    w_nope = jnp.transpose(w_uq[:, :, :QK_NOPE], (1, 0, 2))
    w_uk = jnp.transpose(mla_w_uk[0], (1, 0, 2))
    w_qlat = _fold_qk(w_nope, w_uk)
    w_qpe = w_uq[:, :, QK_NOPE:]
    w_qp = w_qpe.reshape(Q_LORA, H * QK_ROPE).astype(BF16)
    w_qps = _swap_halves(w_qpe).reshape(Q_LORA, H * QK_ROPE).astype(BF16)
    proj_w = (row(norm_pre_mix[0]), w_in_p, row(mla_q_norm[0]), w_qlat, w_qp, w_qps, row(mla_kv_norm[0]))
    w_uv = jnp.transpose(mla_w_uv[0], (1, 0, 2)).astype(BF16)
    w_o = mla_w_o[0].astype(BF16)
    g_post0 = row(norm_post_mix[0])

    x_small = jnp.concatenate([x_sample.reshape(n_s, D), meta_tokens], axis=0)
    pos_small = jnp.concatenate([past_len + jnp.tile(jnp.arange(S, dtype=jnp.int32), DB),
                                 jnp.arange(N_META, dtype=jnp.int32)])
    cos_s, sin_s = _rope_tables(pos_small)
    qlat_s, qpe_s, ckv_s, kpe_s, klat_s, kpeb_s = _mla_proj_small(x_small, cos_s, sin_s, proj_w)

    npad = LANES
    kml = jnp.pad(klat_s[n_s:], ((0, npad - N_META), (0, 0)))
    kmp = jnp.pad(kpeb_s[n_s:], ((0, npad - N_META), (0, 0)))
    o_meta = _meta_attn(qlat_s[n_s:].reshape(N_META * H, KV_LORA), qpe_s[n_s:].reshape(N_META * H, QK_ROPE),
                        kml, kmp)
    knl = jnp.pad(klat_s[:n_s].reshape(DB, S, KV_LORA), ((0, 0), (0, npad - S), (0, 0)))
    knp = jnp.pad(kpeb_s[:n_s].reshape(DB, S, QK_ROPE), ((0, 0), (0, npad - S), (0, 0)))
    o_samp = _decode_attn(page_table, qlat_s[:n_s].reshape(n_s * H, KV_LORA),
                          qpe_s[:n_s].reshape(n_s * H, QK_ROPE), cache_ckv, jnp.swapaxes(cache_kpe, 2, 3),
                          0, knl, knp)
    o_small = jnp.concatenate([o_samp.reshape(n_s, H * KV_LORA), o_meta.reshape(N_META, H * KV_LORA)], axis=0)
    xs1 = _mla_out_small(o_small, x_small, w_uv, w_o, g_post0)

    cos_p, sin_p = _rope_tables(N_META + jnp.arange(T, dtype=jnp.int32))
    qlat_p, qpe_p, ckv_p, kpe_p, klat_p, kpeb_p = _mla_proj_prompt(x_prompt, cos_p, sin_p, proj_w, tr=256)
    xp1 = _flash_prompt(qlat_p, qpe_p, klat_p, kpeb_p, kml, kmp, x_prompt, w_uv, w_o, g_post0, tq=256, hg=4)

    wup0, wdn0 = mlp_w_up[0].astype(BF16), mlp_w_down[0].astype(BF16)
    mlp0 = (row(norm_pre_mlp[0]), wup0, wdn0, row(norm_post_mlp[0]))
    xp2 = _mlp(xp1.reshape(B * T, D), *mlp0, tm=512)
    xs2 = _mlp(xs1, *mlp0, tm=xs1.shape[0])

    hk = GLA_HEADS * GLA_DK
    hv = GLA_HEADS * GLA_DV
    gw = gla_w_in[0]
    gw_p = jnp.concatenate([gw, jnp.zeros((D, LANES - GATE_RANK), F32)], axis=1).astype(BF16)
    wg_p = jnp.concatenate([gla_w_gate[0], jnp.zeros((LANES - GATE_RANK, hk), F32)], axis=0).astype(BF16)
    gla_pw = (row(norm_pre_mix[1]), gw_p, wg_p, row(gla_b_gate[0]))
    gn = row(gla_norm[0])
    gwo = gla_w_o[0].astype(BF16)
    g_post1 = row(norm_post_mix[1])

    q_s, k_s, v_s, r_s, la_s = _gla_proj(xs2, *gla_pw, tr=xs2.shape[0])
    tcol = lambda a, n, l: jnp.transpose(a.reshape(n, l, hk), (0, 2, 1))
    o_gs, s_gs = _gla_tokens(tcol(q_s[:n_s], DB, S), tcol(k_s[:n_s], DB, S), tcol(la_s[:n_s], DB, S),
                             v_s[:n_s].reshape(DB, S, hv), state_gla[0])
    zero_state = jnp.zeros((1, GLA_HEADS, GLA_DK, GLA_DV), F32)
    o_gm, s_gm = _gla_tokens(tcol(q_s[n_s:], 1, N_META), tcol(k_s[n_s:], 1, N_META), tcol(la_s[n_s:], 1, N_META),
                             v_s[n_s:].reshape(1, N_META, hv), zero_state)
    o_gsmall = jnp.concatenate([o_gs.reshape(n_s, hv), o_gm.reshape(N_META, hv)], axis=0)
    xs3 = _gla_out_small(o_gsmall, r_s, xs2, gn, gwo, g_post1)

    q_p, k_p, v_p, r_p, la_p = _gla_proj(xp2, *gla_pw, tr=512)
    r3 = lambda a: a.reshape(B, T, a.shape[-1])
    xp3, s_gp = _gla_prompt(r3(q_p), r3(k_p), r3(v_p), r3(r_p), r3(la_p), xp2.reshape(B, T, D), s_gm[0],
                            gn, gwo, g_post1, rows=256)

    mlp1 = (row(norm_pre_mlp[1]), mlp_w_up[1].astype(BF16), mlp_w_down[1].astype(BF16), row(norm_post_mlp[1]))
    y_prompt = _mlp(xp3.reshape(B * T, D), *mlp1, tm=512).reshape(B, T, D)
    xs4 = _mlp(xs3, *mlp1, tm=xs3.shape[0])
    y_sample = xs4[:n_s].reshape(DB, S, D)

    bmeta = lambda a: jnp.broadcast_to(a[None], (B,) + a.shape)
    new_ckv_prompt = jnp.concatenate([bmeta(ckv_s[n_s:]), ckv_p], axis=1)[None]
    new_kpe_prompt = jnp.concatenate([bmeta(kpe_s[n_s:]), kpe_p], axis=1)[None]
    new_ckv_sample = ckv_s[:n_s].reshape(1, DB, S, KV_LORA)
    new_kpe_sample = kpe_s[:n_s].reshape(1, DB, S, QK_ROPE)
    return (y_prompt, y_sample, new_ckv_prompt, new_kpe_prompt, new_ckv_sample, new_kpe_sample,
            s_gp[None], s_gs[None])
```

```python
import functools

import jax
import jax.numpy as jnp
from jax import lax
from jax.experimental import pallas as pl
from jax.experimental.pallas import tpu as pltpu

F32 = jnp.float32
BF16 = jnp.bfloat16

N_META = 16
MLA_HEADS = 16
Q_LORA = 256
KV_LORA = 256
QK_NOPE = 64
QK_ROPE = 32
V_HEAD = 64
MLA_SCALE = (QK_NOPE + QK_ROPE) ** -0.5
ROPE_BASE = 10000.0
GLA_HEADS = 4
GLA_DK = 128
GLA_DV = 256
GLA_SCALE = GLA_DK ** -0.5
GATE_RANK = 16
GATE_TAU = 16.0
GLA_CHUNK = 64
EPS = 1e-6

LANES = 128
VMEM_LIMIT = 56 * 1024 * 1024


def _cparams(*sem):
    return pltpu.CompilerParams(dimension_semantics=sem, vmem_limit_bytes=VMEM_LIMIT)


def _rms(x, w):
    return x * lax.rsqrt(jnp.mean(x * x, axis=-1, keepdims=True) + EPS) * w


def _dot(a, b):
    return jnp.dot(a, b, preferred_element_type=F32)


def _dot_nt(a, b):
    return lax.dot_general(a, b, (((1,), (1,)), ((), ())), preferred_element_type=F32)


def _dot_tn(a, b):
    return lax.dot_general(a, b, (((0,), (0,)), ((), ())), preferred_element_type=F32)


def _full(shape):
    n = len(shape)
    return pl.BlockSpec(shape, lambda *_: (0,) * n)


def _fold_kernel(a_ref, b_ref, o_ref):
    o_ref[...] = lax.dot_general(a_ref[0], b_ref[0], (((1,), (1,)), ((), ())),
                                 precision=lax.Precision.HIGHEST,
                                 preferred_element_type=F32).astype(o_ref.dtype)


def _fold_qk(w_nope, w_uk):
    return pl.pallas_call(
        _fold_kernel,
        grid=(MLA_HEADS,),
        in_specs=[pl.BlockSpec((1, Q_LORA, QK_NOPE), lambda h: (h, 0, 0)),
                  pl.BlockSpec((1, KV_LORA, QK_NOPE), lambda h: (h, 0, 0))],
        out_specs=pl.BlockSpec((Q_LORA, KV_LORA), lambda h: (0, h)),
        out_shape=jax.ShapeDtypeStruct((Q_LORA, MLA_HEADS * KV_LORA), BF16),
        compiler_params=_cparams("arbitrary"),
        name="fold_qk",
    )(w_nope, w_uk)


def _mla_proj_kernel(x_ref, cos_ref, sin_ref, gpre_ref, win_ref, qn_ref, wql_ref, wqp_ref, wqps_ref,
                     kvn_ref, qlat_ref, qpe_ref, ckv_ref, kpe_ref, klat_ref, kpeb_ref):
    h = _rms(x_ref[...], gpre_ref[...]).astype(BF16)
    a = _dot(h, win_ref[...])
    cqn = _rms(a[:, :Q_LORA], qn_ref[...]).astype(BF16)
    ckv = _rms(a[:, Q_LORA:Q_LORA + KV_LORA], kvn_ref[...])
    cos = cos_ref[...]
    sin = sin_ref[...]
    o = Q_LORA + KV_LORA
    kpe = a[:, o:o + QK_ROPE] * cos[:, :QK_ROPE] + a[:, o + LANES:o + LANES + QK_ROPE] * sin[:, :QK_ROPE]
    qlat = _dot(cqn, wql_ref[...]) * MLA_SCALE
    qp = _dot(cqn, wqp_ref[...])
    qps = _dot(cqn, wqps_ref[...])
    nl = MLA_HEADS * QK_ROPE // LANES
    qpe = jnp.concatenate(
        [(qp[:, j * LANES:(j + 1) * LANES] * cos + qps[:, j * LANES:(j + 1) * LANES] * sin) * MLA_SCALE
         for j in range(nl)], axis=1)
    qlat_ref[...] = qlat.astype(BF16)
    qpe_ref[...] = qpe.astype(BF16)
    ckv_ref[...] = ckv
    kpe_ref[...] = kpe
    klat_ref[...] = ckv.astype(BF16)
    kpeb_ref[...] = kpe.astype(BF16)


def _mla_proj_small(x, cos, sin, w):
    R, D = x.shape
    H = MLA_HEADS
    in_specs = [_full((R, D)), _full((R, LANES)), _full((R, LANES))] + [_full(a.shape) for a in w]
    shapes = [((R, H * KV_LORA), BF16), ((R, H * QK_ROPE), BF16), ((R, KV_LORA), F32),
              ((R, QK_ROPE), F32), ((R, KV_LORA), BF16), ((R, QK_ROPE), BF16)]
    return pl.pallas_call(
        _mla_proj_kernel, grid=(1,), in_specs=in_specs,
        out_specs=[_full(s) for s, _ in shapes],
        out_shape=[jax.ShapeDtypeStruct(s, d) for s, d in shapes],
        compiler_params=_cparams("arbitrary"), name="mla_proj_small",
    )(x, cos, sin, *w)


HEAD_W = LANES


def _mha_proj_kernel(x_ref, cosk_ref, sink_ref, gpre_ref, win_ref, qn_ref, wq_ref, wqs_ref, kvn_ref,
                     wk_ref, wv_ref, q_ref, k_ref, v_ref, ckv_ref, kpe_ref):
    h = _rms(x_ref[0], gpre_ref[...]).astype(BF16)
    a = _dot(h, win_ref[...])
    cqn = _rms(a[:, :Q_LORA], qn_ref[...]).astype(BF16)
    ckv = _rms(a[:, Q_LORA:Q_LORA + KV_LORA], kvn_ref[...])
    cosk = cosk_ref[...]
    sink = sink_ref[...]
    o = Q_LORA + KV_LORA
    kpe = a[:, o:o + HEAD_W] * cosk + a[:, o + HEAD_W:o + 2 * HEAD_W] * sink
    ckv_ref[0] = ckv
    kpe_ref[0] = kpe[:, QK_NOPE:QK_NOPE + QK_ROPE]
    ckv_b = ckv.astype(BF16)
    k_all = _dot(ckv_b, wk_ref[...])
    v_all = _dot(ckv_b, wv_ref[...])
    q_raw = _dot(cqn, wq_ref[...])
    q_swp = _dot(cqn, wqs_ref[...])
    lane = lax.broadcasted_iota(jnp.int32, (1, HEAD_W), 1)
    cosq = cosk + jnp.where(lane < QK_NOPE, 1.0, 0.0)
    for hd in range(MLA_HEADS):
        sl = slice(hd * HEAD_W, (hd + 1) * HEAD_W)
        k_ref[0, hd] = (k_all[:, sl] + kpe).astype(BF16)
        q_ref[0, hd] = ((q_raw[:, sl] * cosq + q_swp[:, sl] * sink) * MLA_SCALE).astype(BF16)
    for p in range(MLA_HEADS * V_HEAD // LANES):
        v_ref[0, p] = v_all[:, p * LANES:(p + 1) * LANES].astype(BF16)


def _mha_proj(x, cosk, sink, w, tr):
    B, T, D = x.shape
    H = MLA_HEADS
    nv = H * V_HEAD // LANES
    in_specs = [pl.BlockSpec((1, tr, D), lambda b, i: (b, i, 0)),
                pl.BlockSpec((tr, HEAD_W), lambda b, i: (i, 0)),
                pl.BlockSpec((tr, HEAD_W), lambda b, i: (i, 0))] + [_full(a.shape) for a in w]
    out_shape = [jax.ShapeDtypeStruct((B, H, T, HEAD_W), BF16),
                 jax.ShapeDtypeStruct((B, H, T, HEAD_W), BF16),
                 jax.ShapeDtypeStruct((B, nv, T, LANES), BF16),
                 jax.ShapeDtypeStruct((B, T, KV_LORA), F32),
                 jax.ShapeDtypeStruct((B, T, QK_ROPE), F32)]
    out_specs = [pl.BlockSpec((1, H, tr, HEAD_W), lambda b, i: (b, 0, i, 0)),
                 pl.BlockSpec((1, H, tr, HEAD_W), lambda b, i: (b, 0, i, 0)),
                 pl.BlockSpec((1, nv, tr, LANES), lambda b, i: (b, 0, i, 0)),
                 pl.BlockSpec((1, tr, KV_LORA), lambda b, i: (b, i, 0)),
                 pl.BlockSpec((1, tr, QK_ROPE), lambda b, i: (b, i, 0))]
    return pl.pallas_call(
        _mha_proj_kernel, grid=(B, T // tr), in_specs=in_specs, out_specs=out_specs, out_shape=out_shape,
        compiler_params=_cparams("arbitrary", "arbitrary"), name="mha_proj",
    )(x, cosk, sink, *w)


def _mha_flash_kernel(q_ref, k_ref, v_ref, km_ref, vm_ref, o_ref, m_sc, l_sc, acc_sc, *, tq):
    i = pl.program_id(2)
    nh = q_ref.shape[1]

    def scores(hh, kb, mask):
        s = _dot_nt(q_ref[0, hh], kb)
        return s if mask is None else jnp.where(mask, s, -jnp.inf)

    def apply(hh, s, vb, first):
        parts = [s[:, c * LANES:(c + 1) * LANES] for c in range(s.shape[1] // LANES)]
        mrow = jnp.max(functools.reduce(jnp.maximum, parts), axis=-1, keepdims=True)
        if first:
            m_new = jnp.broadcast_to(mrow, (tq, LANES))
        else:
            m_old = m_sc[hh]
            m_new = jnp.maximum(m_old, mrow)
        ps = [jnp.exp(part - m_new) for part in parts]
        psum = functools.reduce(jnp.add, ps)
        p = ps[0] if len(ps) == 1 else jnp.concatenate(ps, axis=1)
        pv = _dot(p.astype(BF16), vb)
        if first:
            l_sc[hh] = psum
            acc_sc[hh] = pv
        else:
            alpha = jnp.exp(m_old - m_new)
            l_sc[hh] = alpha * l_sc[hh] + psum
            acc_sc[hh] = alpha * acc_sc[hh] + pv
        m_sc[hh] = m_new

    nm = km_ref.shape[2]
    meta_mask = lax.broadcasted_iota(jnp.int32, (tq, nm), 1) < N_META
    hpg = nh // v_ref.shape[1]

    def step(kblock, vblock, mask, first):
        ss = [scores(hh, kblock(hh), mask) for hh in range(nh)]
        for hh in range(nh):
            apply(hh, ss[hh], vblock(hh // hpg), first)

    step(lambda hh: km_ref[0, hh], lambda g: vm_ref[0, g], meta_mask, True)

    def body(j, carry):
        off = pl.multiple_of(j * tq, tq)
        step(lambda hh: k_ref[0, hh, pl.ds(off, tq), :], lambda g: v_ref[0, g, pl.ds(off, tq), :], None, False)
        return carry

    lax.fori_loop(0, i, body, 0)

    off = pl.multiple_of(i * tq, tq)
    causal = (lax.broadcasted_iota(jnp.int32, (tq, tq), 1) <= lax.broadcasted_iota(jnp.int32, (tq, tq), 0))
    step(lambda hh: k_ref[0, hh, pl.ds(off, tq), :], lambda g: v_ref[0, g, pl.ds(off, tq), :], causal, False)

    lane = lax.broadcasted_iota(jnp.int32, (tq, LANES), 1)
    for g in range(nh // hpg):
        out = None
        for r in range(hpg):
            hh = g * hpg + r
            o = acc_sc[hh] / jnp.sum(l_sc[hh], axis=-1, keepdims=True)
            out = o if out is None else jnp.where(lane >= r * V_HEAD, o, out)
        o_ref[0, :, g * LANES:(g + 1) * LANES] = out.astype(o_ref.dtype)


MHA_GROUPS_PER_STEP = 2


def _mha_flash(q, k, v, km, vm, tq):
    B, H, T, W = q.shape
    nv = v.shape[1]
    ng = MHA_GROUPS_PER_STEP
    nh = H // nv * ng
    return pl.pallas_call(
        functools.partial(_mha_flash_kernel, tq=tq), grid=(B, nv // ng, T // tq),
        in_specs=[pl.BlockSpec((1, nh, tq, W), lambda b, p, i: (b, p, i, 0)),
                  pl.BlockSpec((1, nh, T, W), lambda b, p, i: (b, p, 0, 0)),
                  pl.BlockSpec((1, ng, T, LANES), lambda b, p, i: (b, p, 0, 0)),
                  pl.BlockSpec((1, nh) + km.shape[2:], lambda b, p, i: (0, p, 0, 0)),
                  pl.BlockSpec((1, ng) + vm.shape[2:], lambda b, p, i: (0, p, 0, 0))],
        out_specs=pl.BlockSpec((1, tq, ng * LANES), lambda b, p, i: (b, i, p)),
        out_shape=jax.ShapeDtypeStruct((B, T, nv * LANES), BF16),
        scratch_shapes=[pltpu.VMEM((nh, tq, LANES), F32), pltpu.VMEM((nh, tq, LANES), F32),
                        pltpu.VMEM((nh, tq, LANES), F32)],
        compiler_params=_cparams("arbitrary", "arbitrary", "arbitrary"), name="mha_flash",
    )(q, k, v, km, vm)


def _meta_attn_kernel(qlat_ref, qpe_ref, kl_ref, kp_ref, o_ref):
    kl = kl_ref[...]
    s = _dot_nt(qlat_ref[...], kl) + _dot_nt(qpe_ref[...], kp_ref[...])
    r, n = s.shape
    tok = lax.broadcasted_iota(jnp.int32, (r // MLA_HEADS, MLA_HEADS, n), 0).reshape(r, n)
    col = lax.broadcasted_iota(jnp.int32, (r, n), 1)
    s = jnp.where(col <= tok, s, -jnp.inf)
    p = jnp.exp(s - jnp.max(s, axis=-1, keepdims=True))
    l = jnp.sum(p, axis=-1, keepdims=True)
    o_ref[...] = _dot(p.astype(BF16), kl) / l


def _meta_attn(qlat, qpe, kl, kp):
    r = qlat.shape[0]
    return pl.pallas_call(
        _meta_attn_kernel, grid=(1,),
        in_specs=[_full(qlat.shape), _full(qpe.shape), _full(kl.shape), _full(kp.shape)],
        out_specs=_full((r, KV_LORA)), out_shape=jax.ShapeDtypeStruct((r, KV_LORA), F32),
        compiler_params=_cparams("arbitrary"), name="meta_attn",
    )(qlat, qpe, kl, kp)


def _decode_kernel(pt_ref, qlat_ref, qpe_ref, knl_ref, knp_ref, ckv_hbm, kpt_hbm, o_ref,
                   ckv_buf, kpt_buf, sem, m_sc, l_sc, acc_sc, *, layer, npages, gp, nbuf, n_new):
    s = pl.program_id(0)
    nseq = pl.num_programs(0)
    ngroups = npages // gp
    page = ckv_buf.shape[1] // gp
    ql = qlat_ref[...]
    qp = qpe_ref[...]
    rq = ql.shape[0]

    def group_copies(seq, g):
        slot = g % nbuf
        cps = []
        for p in range(gp):
            pid = pt_ref[seq * npages + g * gp + p]
            cps.append(pltpu.make_async_copy(ckv_hbm.at[layer, pid],
                                             ckv_buf.at[slot, pl.ds(p * page, page), :], sem.at[slot]))
            cps.append(pltpu.make_async_copy(kpt_hbm.at[layer, pid],
                                             kpt_buf.at[slot, :, pl.ds(p * page, page)], sem.at[slot]))
        return cps

    @pl.when(s == 0)
    def _():
        for g in range(nbuf - 1):
            for cp in group_copies(0, g):
                cp.start()

    m_sc[...] = jnp.full(m_sc.shape, -jnp.inf, F32)
    l_sc[...] = jnp.zeros(l_sc.shape, F32)
    acc_sc[...] = jnp.zeros(acc_sc.shape, F32)

    def accumulate(kl, s):
        parts = [s[:, c * LANES:(c + 1) * LANES] for c in range(s.shape[1] // LANES)]
        m_old = m_sc[...]
        m_new = jnp.maximum(m_old, jnp.max(functools.reduce(jnp.maximum, parts), axis=-1, keepdims=True))
        alpha = jnp.exp(m_old - m_new)
        ps = [jnp.exp(part - m_new) for part in parts]
        p = ps[0] if len(ps) == 1 else jnp.concatenate(ps, axis=1)
        l_sc[...] = alpha * l_sc[...] + functools.reduce(jnp.add, ps)
        acc_sc[...] = (jnp.concatenate([alpha] * (KV_LORA // LANES), axis=1) * acc_sc[...]
                       + _dot(p.astype(BF16), kl))
        m_sc[...] = m_new

    def scores(g):
        for cp in group_copies(s, g):
            cp.wait()
        slot = g % nbuf
        kl = ckv_buf[slot].astype(BF16)
        return kl, _dot_nt(ql, kl) + _dot(qp, kpt_buf[slot].astype(BF16))

    cur = scores(0)
    for g in range(ngroups):
        nxt = g + nbuf - 1
        if nxt < ngroups:
            for cp in group_copies(s, nxt):
                cp.start()
        else:
            @pl.when(s + 1 < nseq)
            def _():
                for cp in group_copies(s + 1, nxt - ngroups):
                    cp.start()
        ahead = scores(g + 1) if g + 1 < ngroups else None
        accumulate(*cur)
        cur = ahead

    kn = knl_ref[0]
    sn = _dot_nt(ql, kn) + _dot_nt(qp, knp_ref[0])
    n = sn.shape[1]
    tok = lax.broadcasted_iota(jnp.int32, (n_new, rq // n_new, n), 0).reshape(rq, n)
    col = lax.broadcasted_iota(jnp.int32, (rq, n), 1)
    accumulate(kn, jnp.where(col <= tok, sn, -jnp.inf))
    o_ref[...] = acc_sc[...] / jnp.sum(l_sc[...], axis=-1, keepdims=True)


DECODE_GROUP_PAGES = 16
DECODE_RING_SLOTS = 4


def _decode_attn(page_table, qlat, qpe, cache_ckv, cache_kpt, layer, knl, knp):
    nseq, npages = page_table.shape
    page = cache_ckv.shape[2]
    n_new = qlat.shape[0] // (nseq * MLA_HEADS)
    rq = n_new * MLA_HEADS
    gp, nbuf = DECODE_GROUP_PAGES, DECODE_RING_SLOTS
    assert npages % gp == 0 and (npages // gp) % nbuf == 0
    pt = page_table.reshape(-1)
    in_specs = [pl.BlockSpec((rq, KV_LORA), lambda s, pt_ref: (s, 0)),
                pl.BlockSpec((rq, QK_ROPE), lambda s, pt_ref: (s, 0)),
                pl.BlockSpec((1,) + knl.shape[1:], lambda s, pt_ref: (s, 0, 0)),
                pl.BlockSpec((1,) + knp.shape[1:], lambda s, pt_ref: (s, 0, 0)),
                pl.BlockSpec(memory_space=pl.ANY), pl.BlockSpec(memory_space=pl.ANY)]
    grid_spec = pltpu.PrefetchScalarGridSpec(
        num_scalar_prefetch=1, grid=(nseq,), in_specs=in_specs,
        out_specs=pl.BlockSpec((rq, KV_LORA), lambda s, pt_ref: (s, 0)),
        scratch_shapes=[pltpu.VMEM((nbuf, gp * page, KV_LORA), F32), pltpu.VMEM((nbuf, QK_ROPE, gp * page), F32),
                        pltpu.SemaphoreType.DMA((nbuf,)),
                        pltpu.VMEM((rq, LANES), F32), pltpu.VMEM((rq, LANES), F32),
                        pltpu.VMEM((rq, KV_LORA), F32)])
    return pl.pallas_call(
        functools.partial(_decode_kernel, layer=layer, npages=npages, gp=gp, nbuf=nbuf, n_new=n_new),
        grid_spec=grid_spec,
        out_shape=jax.ShapeDtypeStruct((nseq * rq, KV_LORA), F32),
        compiler_params=_cparams("arbitrary"), name="decode_attn",
    )(pt, qlat, qpe, knl, knp, cache_ckv, cache_kpt)


def _mla_out_kernel(o_ref, x_ref, wuv_ref, wo_ref, gpost_ref, y_ref, ocat_sc):
    for hd in range(MLA_HEADS):
        oh = _dot(o_ref[:, hd * KV_LORA:(hd + 1) * KV_LORA].astype(BF16), wuv_ref[hd])
        ocat_sc[:, hd * V_HEAD:(hd + 1) * V_HEAD] = oh
    m = _dot(ocat_sc[...].astype(BF16), wo_ref[...])
    y_ref[...] = x_ref[...] + _rms(m, gpost_ref[...])


def _mla_out_small(o, x, wuv, wo, gpost):
    R, D = x.shape
    return pl.pallas_call(
        _mla_out_kernel, grid=(1,),
        in_specs=[_full(o.shape), _full(x.shape), _full(wuv.shape), _full(wo.shape), _full(gpost.shape)],
        out_specs=_full((R, D)), out_shape=jax.ShapeDtypeStruct((R, D), F32),
        scratch_shapes=[pltpu.VMEM((R, MLA_HEADS * V_HEAD), F32)],
        compiler_params=_cparams("arbitrary"), name="mla_out_small",
    )(o, x, wuv, wo, gpost)


def _mlp_kernel(*refs, fc, attn):
    if attn:
        a_ref, x_ref, wo_ref, gmix_ref, gpre_ref, wup_ref, wdn_ref, gpost_ref, o_ref = refs
        x = x_ref[...] + _rms(_dot(a_ref[...], wo_ref[...]), gmix_ref[...])
    else:
        x_ref, gpre_ref, wup_ref, wdn_ref, gpost_ref, o_ref = refs
        x = x_ref[...]
    h = _rms(x, gpre_ref[...]).astype(BF16)
    dff = wup_ref.shape[1]
    acc = None
    for c in range(dff // fc):
        u = _dot(h, wup_ref[:, c * fc:(c + 1) * fc])
        u = jnp.square(jnp.maximum(u, 0.0)).astype(BF16)
        d = _dot(u, wdn_ref[c * fc:(c + 1) * fc, :])
        acc = d if acc is None else acc + d
    o_ref[...] = x + _rms(acc, gpost_ref[...])


def _mlp(x, gpre, wup, wdn, gpost, tm, fc=1024, attn=None):
    N, D = x.shape
    const = lambda a: pl.BlockSpec(a.shape, lambda i: (0, 0), pipeline_mode=pl.Buffered(1))
    rows = lambda: pl.BlockSpec((tm, D), lambda i: (i, 0))
    args = [x, gpre, wup, wdn, gpost]
    in_specs = [rows(), const(gpre), const(wup), const(wdn), const(gpost)]
    if attn is not None:
        a, wo, gmix = attn
        args = [a, x, wo, gmix] + args[1:]
        in_specs = [rows(), rows(), const(wo), const(gmix)] + in_specs[1:]
    return pl.pallas_call(
        functools.partial(_mlp_kernel, fc=fc, attn=attn is not None), grid=(N // tm,),
        in_specs=in_specs, out_specs=rows(),
        out_shape=jax.ShapeDtypeStruct((N, D), F32),
        compiler_params=_cparams("arbitrary"), name="mlp",
    )(*args)


def _gla_proj_kernel(x_ref, gpre_ref, win_ref, wg_ref, bg_ref, q_ref, k_ref, v_ref, r_ref, la_ref):
    hk = GLA_HEADS * GLA_DK
    hv = GLA_HEADS * GLA_DV
    h = _rms(x_ref[...], gpre_ref[...]).astype(BF16)
    a = _dot(h, win_ref[...])
    q_ref[...] = a[:, :hk] * GLA_SCALE
    k_ref[...] = a[:, hk:2 * hk]
    v_ref[...] = a[:, 2 * hk:2 * hk + hv]
    r_ref[...] = a[:, 2 * hk + hv:2 * hk + 2 * hv]
    gd = a[:, 2 * hk + 2 * hv:].astype(BF16)
    z = _dot(gd, wg_ref[...]) + bg_ref[...]
    la_ref[...] = (jnp.minimum(z, 0.0) - jnp.log(1.0 + jnp.exp(-jnp.abs(z)))) * (1.0 / GATE_TAU)


def _gla_proj(x, gpre, win, wg, bg, tr):
    N, D = x.shape
    hk = GLA_HEADS * GLA_DK
    hv = GLA_HEADS * GLA_DV
    row = lambda w: pl.BlockSpec((tr, w), lambda i: (i, 0))
    return pl.pallas_call(
        _gla_proj_kernel, grid=(N // tr,),
        in_specs=[row(D), _full(gpre.shape), _full(win.shape), _full(wg.shape), _full(bg.shape)],
        out_specs=[row(hk), row(hk), row(hv), row(hv), row(hk)],
        out_shape=[jax.ShapeDtypeStruct((N, w), F32) for w in (hk, hk, hv, hv, hk)],
        compiler_params=_cparams("arbitrary"), name="gla_proj",
    )(x, gpre, win, wg, bg)


def _gla_gate(o, r, gn):
    return _rms(o, gn) * (r / (1.0 + jnp.exp(-r)))


def _gla_prompt_kernel(q_ref, k_ref, v_ref, r_ref, la_ref, x_ref, s0_ref, gn_ref, wo_ref, gpost_ref,
                       y_ref, sfin_ref, s_sc, ocat_sc, *, rows):
    c_sz = GLA_CHUNK

    @pl.when(pl.program_id(1) == 0)
    def _():
        s_sc[...] = s0_ref[...]

    ri = lax.broadcasted_iota(jnp.int32, (c_sz, c_sz), 0)
    ci = lax.broadcasted_iota(jnp.int32, (c_sz, c_sz), 1)
    tril = ci <= ri
    ltri = tril.astype(F32)
    gn = gn_ref[...]
    for c in range(rows // c_sz):
        sl = slice(c * c_sz, (c + 1) * c_sz)
        la = la_ref[0, sl, :]
        b = jnp.dot(ltri, la, precision=lax.Precision.HIGHEST, preferred_element_type=F32)
        b_last = b[c_sz - 1:c_sz, :]
        q_in = (q_ref[0, sl, :] * jnp.exp(b)).astype(BF16)
        k = k_ref[0, sl, :]
        k_in = (k * jnp.exp(-b)).astype(BF16)
        k_dec = (k * jnp.exp(b_last - b)).astype(BF16)
        dec = jnp.exp(b_last)
        for hd in range(GLA_HEADS):
            ks = slice(hd * GLA_DK, (hd + 1) * GLA_DK)
            vs = slice(hd * GLA_DV, (hd + 1) * GLA_DV)
            v = v_ref[0, sl, vs].astype(BF16)
            a = jnp.where(tril, _dot_nt(q_in[:, ks], k_in[:, ks]), 0.0)
            s_old = s_sc[hd]
            o = _dot(a.astype(BF16), v) + _dot(q_in[:, ks], s_old.astype(BF16))
            dcol = jnp.transpose(jnp.broadcast_to(dec[:, ks], (GLA_DK, GLA_DK)))
            dfull = jnp.concatenate([dcol] * (GLA_DV // GLA_DK), axis=1)
            s_sc[hd] = dfull * s_old + _dot_tn(k_dec[:, ks], v)
            ocat_sc[sl, vs] = _gla_gate(o, r_ref[0, sl, vs], gn)
    m = _dot(ocat_sc[...].astype(BF16), wo_ref[...])
    y_ref[0] = x_ref[0] + _rms(m, gpost_ref[...])

    @pl.when(pl.program_id(1) == pl.num_programs(1) - 1)
    def _():
        sfin_ref[0] = s_sc[...]


def _gla_prompt(q, k, v, r, la, x, s0, gn, wo, gpost, rows):
    B, T, D = x.shape
    hk = GLA_HEADS * GLA_DK
    hv = GLA_HEADS * GLA_DV
    blk = lambda w: pl.BlockSpec((1, rows, w), lambda b, i: (b, i, 0))
    return pl.pallas_call(
        functools.partial(_gla_prompt_kernel, rows=rows), grid=(B, T // rows),
        in_specs=[blk(hk), blk(hk), blk(hv), blk(hv), blk(hk), blk(D), _full(s0.shape), _full(gn.shape),
                  _full(wo.shape), _full(gpost.shape)],
        out_specs=[blk(D), pl.BlockSpec((1,) + s0.shape, lambda b, i: (b, 0, 0, 0))],
        out_shape=[jax.ShapeDtypeStruct((B, T, D), F32), jax.ShapeDtypeStruct((B,) + s0.shape, F32)],
        scratch_shapes=[pltpu.VMEM(s0.shape, F32), pltpu.VMEM((rows, hv), F32)],
        compiler_params=_cparams("arbitrary", "arbitrary"), name="gla_prompt",
    )(q, k, v, r, la, x, s0, gn, wo, gpost)


def _gla_tokens_kernel(qt_ref, kt_ref, lat_ref, v_ref, s0_ref, o_ref, sfin_ref, *, ntok):
    for hd in range(GLA_HEADS):
        ks = slice(hd * GLA_DK, (hd + 1) * GLA_DK)
        vs = slice(hd * GLA_DV, (hd + 1) * GLA_DV)
        s = s0_ref[0, hd]
        for t in range(ntok):
            a = jnp.exp(lat_ref[0, ks, t:t + 1])
            s = a * s + kt_ref[0, ks, t:t + 1] * v_ref[0, t:t + 1, vs]
            o_ref[0, t:t + 1, vs] = jnp.sum(qt_ref[0, ks, t:t + 1] * s, axis=0, keepdims=True)
        sfin_ref[0, hd] = s


def _gla_tokens(qt, kt, lat, v, s0):
    nseq, hk, ntok = qt.shape
    hv = v.shape[-1]
    col = pl.BlockSpec((1, hk, ntok), lambda s: (s, 0, 0))
    st = pl.BlockSpec((1,) + s0.shape[1:], lambda s: (s, 0, 0, 0))
    return pl.pallas_call(
        functools.partial(_gla_tokens_kernel, ntok=ntok), grid=(nseq,),
        in_specs=[col, col, col, pl.BlockSpec((1, ntok, hv), lambda s: (s, 0, 0)), st],
        out_specs=[pl.BlockSpec((1, ntok, hv), lambda s: (s, 0, 0)), st],
        out_shape=[jax.ShapeDtypeStruct((nseq, ntok, hv), F32), jax.ShapeDtypeStruct(s0.shape, F32)],
        compiler_params=_cparams("arbitrary"), name="gla_tokens",
    )(qt, kt, lat, v, s0)


def _gla_out_kernel(o_ref, r_ref, x_ref, gn_ref, wo_ref, gpost_ref, y_ref, ocat_sc):
    gn = gn_ref[...]
    for hd in range(GLA_HEADS):
        vs = slice(hd * GLA_DV, (hd + 1) * GLA_DV)
        ocat_sc[:, vs] = _gla_gate(o_ref[:, vs], r_ref[:, vs], gn)
    m = _dot(ocat_sc[...].astype(BF16), wo_ref[...])
    y_ref[...] = x_ref[...] + _rms(m, gpost_ref[...])


def _gla_out_small(o, r, x, gn, wo, gpost):
    R, D = x.shape
    return pl.pallas_call(
        _gla_out_kernel, grid=(1,),
        in_specs=[_full(o.shape), _full(r.shape), _full(x.shape), _full(gn.shape), _full(wo.shape),
                  _full(gpost.shape)],
        out_specs=_full((R, D)), out_shape=jax.ShapeDtypeStruct((R, D), F32),
        scratch_shapes=[pltpu.VMEM(o.shape, F32)],
        compiler_params=_cparams("arbitrary"), name="gla_out_small",
    )(o, r, x, gn, wo, gpost)


def _rope_tables(pos):
    half = QK_ROPE // 2
    inv = ROPE_BASE ** (-jnp.arange(half, dtype=F32) / half)
    ang = pos.astype(F32)[:, None] * inv[None, :]
    c, s = jnp.cos(ang), jnp.sin(ang)
    reps = LANES // QK_ROPE
    return (jnp.tile(jnp.concatenate([c, c], axis=1), (1, reps)),
            jnp.tile(jnp.concatenate([-s, s], axis=1), (1, reps)))


def _rope_tables_head(pos):
    half = QK_ROPE // 2
    inv = ROPE_BASE ** (-jnp.arange(half, dtype=F32) / half)
    ang = pos.astype(F32)[:, None] * inv[None, :]
    c, s = jnp.cos(ang), jnp.sin(ang)
    z = lambda n: jnp.zeros((pos.shape[0], n), F32)
    tail = HEAD_W - QK_NOPE - QK_ROPE
    return (jnp.concatenate([z(QK_NOPE), c, c, z(tail)], axis=1),
            jnp.concatenate([z(QK_NOPE), -s, s, z(tail)], axis=1))


def _swap_halves(w):
    half = QK_ROPE // 2
    return jnp.concatenate([w[..., half:], w[..., :half]], axis=-1)


def kernel(x_prompt, x_sample, cache_ckv, cache_kpe, state_gla, page_table, meta_tokens, norm_pre_mix, norm_post_mix, norm_pre_mlp, norm_post_mlp, mla_w_in, mla_q_norm, mla_w_uq, mla_kv_norm, mla_w_uk, mla_w_uv, mla_w_o, gla_w_in, gla_w_gate, gla_b_gate, gla_norm, gla_w_o, mlp_w_up, mlp_w_down):
    B, T, D = x_prompt.shape
    DB, S, _ = x_sample.shape
    H = MLA_HEADS
    n_s = DB * S
    past_len = page_table.shape[1] * cache_ckv.shape[2]
    row = lambda a: a.reshape(1, -1)

    w_in = mla_w_in[0]
    o = Q_LORA + KV_LORA
    w_kpe = w_in[:, o:]
    zpad = jnp.zeros((D, LANES - QK_ROPE), F32)
    w_in_p = jnp.concatenate([w_in[:, :o], w_kpe, zpad, _swap_halves(w_kpe), zpad], axis=1).astype(BF16)
    w_uq = mla_w_uq[0].reshape(Q_LORA, H, QK_NOPE + QK_ROPE)
    w_nope = jnp.transpose(w_uq[:, :, :QK_NOPE], (1, 0, 2))
    w_uk = jnp.transpose(mla_w_uk[0], (1, 0, 2))
    w_qlat = _fold_qk(w_nope, w_uk)
    w_qpe = w_uq[:, :, QK_NOPE:]
    w_qp = w_qpe.reshape(Q_LORA, H * QK_ROPE).astype(BF16)
    w_qps = _swap_halves(w_qpe).reshape(Q_LORA, H * QK_ROPE).astype(BF16)
    proj_w = (row(norm_pre_mix[0]), w_in_p, row(mla_q_norm[0]), w_qlat, w_qp, w_qps, row(mla_kv_norm[0]))
    w_uv = jnp.transpose(mla_w_uv[0], (1, 0, 2)).astype(BF16)
    w_o = mla_w_o[0].astype(BF16)
    g_post0 = row(norm_post_mix[0])

    x_small = jnp.concatenate([x_sample.reshape(n_s, D), meta_tokens], axis=0)
    pos_small = jnp.concatenate([past_len + jnp.tile(jnp.arange(S, dtype=jnp.int32), DB),
                                 jnp.arange(N_META, dtype=jnp.int32)])
    cos_s, sin_s = _rope_tables(pos_small)
    qlat_s, qpe_s, ckv_s, kpe_s, klat_s, kpeb_s = _mla_proj_small(x_small, cos_s, sin_s, proj_w)

    npad = LANES
    kml = jnp.pad(klat_s[n_s:], ((0, npad - N_META), (0, 0)))
    kmp = jnp.pad(kpeb_s[n_s:], ((0, npad - N_META), (0, 0)))
    o_meta = _meta_attn(qlat_s[n_s:].reshape(N_META * H, KV_LORA), qpe_s[n_s:].reshape(N_META * H, QK_ROPE),
                        kml, kmp)
    knl = jnp.pad(klat_s[:n_s].reshape(DB, S, KV_LORA), ((0, 0), (0, npad - S), (0, 0)))
    knp = jnp.pad(kpeb_s[:n_s].reshape(DB, S, QK_ROPE), ((0, 0), (0, npad - S), (0, 0)))
    o_samp = _decode_attn(page_table, qlat_s[:n_s].reshape(n_s * H, KV_LORA),
                          qpe_s[:n_s].reshape(n_s * H, QK_ROPE), cache_ckv, jnp.swapaxes(cache_kpe, 2, 3),
                          0, knl, knp)
    o_small = jnp.concatenate([o_samp.reshape(n_s, H * KV_LORA), o_meta.reshape(N_META, H * KV_LORA)], axis=0)
    xs1 = _mla_out_small(o_small, x_small, w_uv, w_o, g_post0)

    zl = lambda n: jnp.zeros((D, n), F32)
    pad_r = HEAD_W - QK_NOPE - QK_ROPE
    w_in_m = jnp.concatenate([w_in[:, :o], zl(QK_NOPE), w_kpe, zl(pad_r),
                              zl(QK_NOPE), _swap_halves(w_kpe), zl(pad_r)], axis=1).astype(BF16)
    wq = jnp.pad(w_uq, ((0, 0), (0, 0), (0, pad_r))).reshape(Q_LORA, H * HEAD_W).astype(BF16)
    wqs = jnp.pad(_swap_halves(w_qpe), ((0, 0), (0, 0), (QK_NOPE, pad_r))).reshape(Q_LORA, H * HEAD_W).astype(BF16)
    wk = jnp.pad(mla_w_uk[0], ((0, 0), (0, 0), (0, HEAD_W - QK_NOPE))).reshape(KV_LORA, H * HEAD_W).astype(BF16)
    wv = mla_w_uv[0].reshape(KV_LORA, H * V_HEAD).astype(BF16)
    mha_w = (row(norm_pre_mix[0]), w_in_m, row(mla_q_norm[0]), wq, wqs, row(mla_kv_norm[0]), wk, wv)
    cosk, sink = _rope_tables_head(N_META + jnp.arange(T, dtype=jnp.int32))
    q_p, k_p, v_p, ckv_p, kpe_p = _mha_proj(x_prompt, cosk, sink, mha_w, tr=512)
    cosm, sinm = _rope_tables_head(jnp.arange(N_META, dtype=jnp.int32))
    _, k_m, v_m, _, _ = _mha_proj(meta_tokens[None], cosm, sinm, mha_w, tr=N_META)
    mpad = ((0, 0), (0, 0), (0, LANES - N_META), (0, 0))
    attn_p = _mha_flash(q_p, k_p, v_p, jnp.pad(k_m, mpad), jnp.pad(v_m, mpad), tq=512)

    wup0, wdn0 = mlp_w_up[0].astype(BF16), mlp_w_down[0].astype(BF16)
    mlp0 = (row(norm_pre_mlp[0]), wup0, wdn0, row(norm_post_mlp[0]))
    xp2 = _mlp(x_prompt.reshape(B * T, D), *mlp0, tm=512, attn=(attn_p.reshape(B * T, D), w_o, g_post0))
    xs2 = _mlp(xs1, *mlp0, tm=xs1.shape[0])

    hk = GLA_HEADS * GLA_DK
    hv = GLA_HEADS * GLA_DV
    gw = gla_w_in[0]
    gw_p = jnp.concatenate([gw, jnp.zeros((D, LANES - GATE_RANK), F32)], axis=1).astype(BF16)
    wg_p = jnp.concatenate([gla_w_gate[0], jnp.zeros((LANES - GATE_RANK, hk), F32)], axis=0).astype(BF16)
    gla_pw = (row(norm_pre_mix[1]), gw_p, wg_p, row(gla_b_gate[0]))
    gn = row(gla_norm[0])
    gwo = gla_w_o[0].astype(BF16)
    g_post1 = row(norm_post_mix[1])

    q_s, k_s, v_s, r_s, la_s = _gla_proj(xs2, *gla_pw, tr=xs2.shape[0])
    tcol = lambda a, n, l: jnp.transpose(a.reshape(n, l, hk), (0, 2, 1))
    o_gs, s_gs = _gla_tokens(tcol(q_s[:n_s], DB, S), tcol(k_s[:n_s], DB, S), tcol(la_s[:n_s], DB, S),
                             v_s[:n_s].reshape(DB, S, hv), state_gla[0])
    zero_state = jnp.zeros((1, GLA_HEADS, GLA_DK, GLA_DV), F32)
    o_gm, s_gm = _gla_tokens(tcol(q_s[n_s:], 1, N_META), tcol(k_s[n_s:], 1, N_META), tcol(la_s[n_s:], 1, N_META),
                             v_s[n_s:].reshape(1, N_META, hv), zero_state)
    o_gsmall = jnp.concatenate([o_gs.reshape(n_s, hv), o_gm.reshape(N_META, hv)], axis=0)
    xs3 = _gla_out_small(o_gsmall, r_s, xs2, gn, gwo, g_post1)

    q_p, k_p, v_p, r_p, la_p = _gla_proj(xp2, *gla_pw, tr=512)
    r3 = lambda a: a.reshape(B, T, a.shape[-1])
    xp3, s_gp = _gla_prompt(r3(q_p), r3(k_p), r3(v_p), r3(r_p), r3(la_p), xp2.reshape(B, T, D), s_gm[0],
                            gn, gwo, g_post1, rows=256)

    mlp1 = (row(norm_pre_mlp[1]), mlp_w_up[1].astype(BF16), mlp_w_down[1].astype(BF16), row(norm_post_mlp[1]))
    y_prompt = _mlp(xp3.reshape(B * T, D), *mlp1, tm=512).reshape(B, T, D)
    xs4 = _mlp(xs3, *mlp1, tm=xs3.shape[0])
    y_sample = xs4[:n_s].reshape(DB, S, D)

    bmeta = lambda a: jnp.broadcast_to(a[None], (B,) + a.shape)
    new_ckv_prompt = jnp.concatenate([bmeta(ckv_s[n_s:]), ckv_p], axis=1)[None]
    new_kpe_prompt = jnp.concatenate([bmeta(kpe_s[n_s:]), kpe_p], axis=1)[None]
    new_ckv_sample = ckv_s[:n_s].reshape(1, DB, S, KV_LORA)
    new_kpe_sample = kpe_s[:n_s].reshape(1, DB, S, QK_ROPE)
    return (y_prompt, y_sample, new_ckv_prompt, new_kpe_prompt, new_ckv_sample, new_kpe_sample,
            s_gp[None], s_gs[None])
```

```python
import functools

import jax
import jax.numpy as jnp
from jax import lax
from jax.experimental import pallas as pl
from jax.experimental.pallas import tpu as pltpu

F32 = jnp.float32
BF16 = jnp.bfloat16

N_META = 16
MLA_HEADS = 16
Q_LORA = 256
KV_LORA = 256
QK_NOPE = 64
QK_ROPE = 32
V_HEAD = 64
MLA_SCALE = (QK_NOPE + QK_ROPE) ** -0.5
ROPE_BASE = 10000.0
GLA_HEADS = 4
GLA_DK = 128
GLA_DV = 256
GLA_SCALE = GLA_DK ** -0.5
GATE_RANK = 16
GATE_TAU = 16.0
GLA_CHUNK = 64
EPS = 1e-6

LANES = 128
VMEM_LIMIT = 56 * 1024 * 1024


def _cparams(*sem):
    return pltpu.CompilerParams(dimension_semantics=sem, vmem_limit_bytes=VMEM_LIMIT)


def _rms(x, w):
    return x * lax.rsqrt(jnp.mean(x * x, axis=-1, keepdims=True) + EPS) * w


def _dot(a, b):
    return jnp.dot(a, b, preferred_element_type=F32)


def _dot_nt(a, b):
    return lax.dot_general(a, b, (((1,), (1,)), ((), ())), preferred_element_type=F32)


def _dot_tn(a, b):
    return lax.dot_general(a, b, (((0,), (0,)), ((), ())), preferred_element_type=F32)


def _full(shape):
    n = len(shape)
    return pl.BlockSpec(shape, lambda *_: (0,) * n)


def _fold_kernel(a_ref, b_ref, o_ref):
    o_ref[...] = lax.dot_general(a_ref[0], b_ref[0], (((1,), (1,)), ((), ())),
                                 precision=lax.Precision.HIGHEST,
                                 preferred_element_type=F32).astype(o_ref.dtype)


def _fold_qk(w_nope, w_uk):
    return pl.pallas_call(
        _fold_kernel,
        grid=(MLA_HEADS,),
        in_specs=[pl.BlockSpec((1, Q_LORA, QK_NOPE), lambda h: (h, 0, 0)),
                  pl.BlockSpec((1, KV_LORA, QK_NOPE), lambda h: (h, 0, 0))],
        out_specs=pl.BlockSpec((Q_LORA, KV_LORA), lambda h: (0, h)),
        out_shape=jax.ShapeDtypeStruct((Q_LORA, MLA_HEADS * KV_LORA), BF16),
        compiler_params=_cparams("arbitrary"),
        name="fold_qk",
    )(w_nope, w_uk)


def _mla_proj_kernel(x_ref, cos_ref, sin_ref, gpre_ref, win_ref, qn_ref, wql_ref, wqp_ref, wqps_ref,
                     kvn_ref, qlat_ref, qpe_ref, ckv_ref, kpe_ref, klat_ref, kpeb_ref):
    h = _rms(x_ref[...], gpre_ref[...]).astype(BF16)
    a = _dot(h, win_ref[...])
    cqn = _rms(a[:, :Q_LORA], qn_ref[...]).astype(BF16)
    ckv = _rms(a[:, Q_LORA:Q_LORA + KV_LORA], kvn_ref[...])
    cos = cos_ref[...]
    sin = sin_ref[...]
    o = Q_LORA + KV_LORA
    kpe = a[:, o:o + QK_ROPE] * cos[:, :QK_ROPE] + a[:, o + LANES:o + LANES + QK_ROPE] * sin[:, :QK_ROPE]
    qlat = _dot(cqn, wql_ref[...]) * MLA_SCALE
    qp = _dot(cqn, wqp_ref[...])
    qps = _dot(cqn, wqps_ref[...])
    nl = MLA_HEADS * QK_ROPE // LANES
    qpe = jnp.concatenate(
        [(qp[:, j * LANES:(j + 1) * LANES] * cos + qps[:, j * LANES:(j + 1) * LANES] * sin) * MLA_SCALE
         for j in range(nl)], axis=1)
    qlat_ref[...] = qlat.astype(BF16)
    qpe_ref[...] = qpe.astype(BF16)
    ckv_ref[...] = ckv
    kpe_ref[...] = kpe
    klat_ref[...] = ckv.astype(BF16)
    kpeb_ref[...] = kpe.astype(BF16)


def _mla_proj_small(x, cos, sin, w):
    R, D = x.shape
    H = MLA_HEADS
    in_specs = [_full((R, D)), _full((R, LANES)), _full((R, LANES))] + [_full(a.shape) for a in w]
    shapes = [((R, H * KV_LORA), BF16), ((R, H * QK_ROPE), BF16), ((R, KV_LORA), F32),
              ((R, QK_ROPE), F32), ((R, KV_LORA), BF16), ((R, QK_ROPE), BF16)]
    return pl.pallas_call(
        _mla_proj_kernel, grid=(1,), in_specs=in_specs,
        out_specs=[_full(s) for s, _ in shapes],
        out_shape=[jax.ShapeDtypeStruct(s, d) for s, d in shapes],
        compiler_params=_cparams("arbitrary"), name="mla_proj_small",
    )(x, cos, sin, *w)


HEAD_W = LANES


def _mha_proj_kernel(x_ref, cosk_ref, sink_ref, gpre_ref, win_ref, qn_ref, wq_ref, wqs_ref, kvn_ref,
                     wk_ref, wv_ref, q_ref, k_ref, v_ref, ckv_ref, kpe_ref):
    h = _rms(x_ref[0], gpre_ref[...]).astype(BF16)
    a = _dot(h, win_ref[...])
    cqn = _rms(a[:, :Q_LORA], qn_ref[...]).astype(BF16)
    ckv = _rms(a[:, Q_LORA:Q_LORA + KV_LORA], kvn_ref[...])
    cosk = cosk_ref[...]
    sink = sink_ref[...]
    o = Q_LORA + KV_LORA
    kpe = a[:, o:o + HEAD_W] * cosk + a[:, o + HEAD_W:o + 2 * HEAD_W] * sink
    ckv_ref[0] = ckv
    kpe_ref[0] = kpe[:, QK_NOPE:QK_NOPE + QK_ROPE]
    ckv_b = ckv.astype(BF16)
    k_all = _dot(ckv_b, wk_ref[...])
    v_all = _dot(ckv_b, wv_ref[...])
    q_raw = _dot(cqn, wq_ref[...])
    q_swp = _dot(cqn, wqs_ref[...])
    lane = lax.broadcasted_iota(jnp.int32, (1, HEAD_W), 1)
    cosq = cosk + jnp.where(lane < QK_NOPE, 1.0, 0.0)
    for hd in range(MLA_HEADS):
        sl = slice(hd * HEAD_W, (hd + 1) * HEAD_W)
        k_ref[0, hd] = (k_all[:, sl] + kpe).astype(BF16)
        q_ref[0, hd] = ((q_raw[:, sl] * cosq + q_swp[:, sl] * sink) * MLA_SCALE).astype(BF16)
    for p in range(MLA_HEADS * V_HEAD // LANES):
        v_ref[0, p] = v_all[:, p * LANES:(p + 1) * LANES].astype(BF16)


def _mha_proj(x, cosk, sink, w, tr):
    B, T, D = x.shape
    H = MLA_HEADS
    nv = H * V_HEAD // LANES
    in_specs = [pl.BlockSpec((1, tr, D), lambda b, i: (b, i, 0)),
                pl.BlockSpec((tr, HEAD_W), lambda b, i: (i, 0)),
                pl.BlockSpec((tr, HEAD_W), lambda b, i: (i, 0))] + [_full(a.shape) for a in w]
    out_shape = [jax.ShapeDtypeStruct((B, H, T, HEAD_W), BF16),
                 jax.ShapeDtypeStruct((B, H, T, HEAD_W), BF16),
                 jax.ShapeDtypeStruct((B, nv, T, LANES), BF16),
                 jax.ShapeDtypeStruct((B, T, KV_LORA), F32),
                 jax.ShapeDtypeStruct((B, T, QK_ROPE), F32)]
    out_specs = [pl.BlockSpec((1, H, tr, HEAD_W), lambda b, i: (b, 0, i, 0)),
                 pl.BlockSpec((1, H, tr, HEAD_W), lambda b, i: (b, 0, i, 0)),
                 pl.BlockSpec((1, nv, tr, LANES), lambda b, i: (b, 0, i, 0)),
                 pl.BlockSpec((1, tr, KV_LORA), lambda b, i: (b, i, 0)),
                 pl.BlockSpec((1, tr, QK_ROPE), lambda b, i: (b, i, 0))]
    return pl.pallas_call(
        _mha_proj_kernel, grid=(B, T // tr), in_specs=in_specs, out_specs=out_specs, out_shape=out_shape,
        compiler_params=_cparams("arbitrary", "arbitrary"), name="mha_proj",
    )(x, cosk, sink, *w)


def _mha_flash_kernel(q_ref, k_ref, v_ref, km_ref, vm_ref, o_ref, m_sc, l_sc, acc_sc, *, tq):
    i = pl.program_id(2)
    nh = q_ref.shape[1]

    def scores(hh, kb, mask):
        s = _dot_nt(q_ref[0, hh], kb)
        return s if mask is None else jnp.where(mask, s, -jnp.inf)

    def apply(hh, s, vb, first):
        parts = [s[:, c * LANES:(c + 1) * LANES] for c in range(s.shape[1] // LANES)]
        mrow = jnp.max(functools.reduce(jnp.maximum, parts), axis=-1, keepdims=True)
        if first:
            m_new = jnp.broadcast_to(mrow, (tq, LANES))
        else:
            m_old = m_sc[hh]
            m_new = jnp.maximum(m_old, mrow)
        ps = [jnp.exp(part - m_new) for part in parts]
        psum = functools.reduce(jnp.add, ps)
        p = ps[0] if len(ps) == 1 else jnp.concatenate(ps, axis=1)
        pv = _dot(p.astype(BF16), vb)
        if first:
            l_sc[hh] = psum
            acc_sc[hh] = pv
        else:
            alpha = jnp.exp(m_old - m_new)
            l_sc[hh] = alpha * l_sc[hh] + psum
            acc_sc[hh] = alpha * acc_sc[hh] + pv
        m_sc[hh] = m_new

    nm = km_ref.shape[2]
    meta_mask = lax.broadcasted_iota(jnp.int32, (tq, nm), 1) < N_META
    hpg = nh // v_ref.shape[1]

    def step(kblock, vblock, mask, first):
        ss = [scores(hh, kblock(hh), mask) for hh in range(nh)]
        for hh in range(nh):
            apply(hh, ss[hh], vblock(hh // hpg), first)

    step(lambda hh: km_ref[0, hh], lambda g: vm_ref[0, g], meta_mask, True)

    def body(j, carry):
        off = pl.multiple_of(j * tq, tq)
        step(lambda hh: k_ref[0, hh, pl.ds(off, tq), :], lambda g: v_ref[0, g, pl.ds(off, tq), :], None, False)
        return carry

    lax.fori_loop(0, i, body, 0)

    off = pl.multiple_of(i * tq, tq)
    causal = (lax.broadcasted_iota(jnp.int32, (tq, tq), 1) <= lax.broadcasted_iota(jnp.int32, (tq, tq), 0))
    step(lambda hh: k_ref[0, hh, pl.ds(off, tq), :], lambda g: v_ref[0, g, pl.ds(off, tq), :], causal, False)

    lane = lax.broadcasted_iota(jnp.int32, (tq, LANES), 1)
    for g in range(nh // hpg):
        out = None
        for r in range(hpg):
            hh = g * hpg + r
            o = acc_sc[hh] / jnp.sum(l_sc[hh], axis=-1, keepdims=True)
            out = o if out is None else jnp.where(lane >= r * V_HEAD, o, out)
        o_ref[0, :, g * LANES:(g + 1) * LANES] = out.astype(o_ref.dtype)


MHA_GROUPS_PER_STEP = 2


def _mha_flash(q, k, v, km, vm, tq):
    B, H, T, W = q.shape
    nv = v.shape[1]
    ng = MHA_GROUPS_PER_STEP
    nh = H // nv * ng
    return pl.pallas_call(
        functools.partial(_mha_flash_kernel, tq=tq), grid=(B, nv // ng, T // tq),
        in_specs=[pl.BlockSpec((1, nh, tq, W), lambda b, p, i: (b, p, i, 0)),
                  pl.BlockSpec((1, nh, T, W), lambda b, p, i: (b, p, 0, 0)),
                  pl.BlockSpec((1, ng, T, LANES), lambda b, p, i: (b, p, 0, 0)),
                  pl.BlockSpec((1, nh) + km.shape[2:], lambda b, p, i: (0, p, 0, 0)),
                  pl.BlockSpec((1, ng) + vm.shape[2:], lambda b, p, i: (0, p, 0, 0))],
        out_specs=pl.BlockSpec((1, tq, ng * LANES), lambda b, p, i: (b, i, p)),
        out_shape=jax.ShapeDtypeStruct((B, T, nv * LANES), BF16),
        scratch_shapes=[pltpu.VMEM((nh, tq, LANES), F32), pltpu.VMEM((nh, tq, LANES), F32),
                        pltpu.VMEM((nh, tq, LANES), F32)],
        compiler_params=_cparams("arbitrary", "arbitrary", "arbitrary"), name="mha_flash",
    )(q, k, v, km, vm)


def _meta_attn_kernel(qlat_ref, qpe_ref, kl_ref, kp_ref, o_ref):
    kl = kl_ref[...]
    s = _dot_nt(qlat_ref[...], kl) + _dot_nt(qpe_ref[...], kp_ref[...])
    r, n = s.shape
    tok = lax.broadcasted_iota(jnp.int32, (r // MLA_HEADS, MLA_HEADS, n), 0).reshape(r, n)
    col = lax.broadcasted_iota(jnp.int32, (r, n), 1)
    s = jnp.where(col <= tok, s, -jnp.inf)
    p = jnp.exp(s - jnp.max(s, axis=-1, keepdims=True))
    l = jnp.sum(p, axis=-1, keepdims=True)
    o_ref[...] = _dot(p.astype(BF16), kl) / l


def _meta_attn(qlat, qpe, kl, kp):
    r = qlat.shape[0]
    return pl.pallas_call(
        _meta_attn_kernel, grid=(1,),
        in_specs=[_full(qlat.shape), _full(qpe.shape), _full(kl.shape), _full(kp.shape)],
        out_specs=_full((r, KV_LORA)), out_shape=jax.ShapeDtypeStruct((r, KV_LORA), F32),
        compiler_params=_cparams("arbitrary"), name="meta_attn",
    )(qlat, qpe, kl, kp)


def _decode_kernel(pt_ref, qlat_ref, qpe_ref, knl_ref, knp_ref, ckv_hbm, kpt_hbm, o_ref,
                   ckv_buf, kpt_buf, sem, m_sc, l_sc, acc_sc, *, layer, npages, gp, nbuf, n_new):
    s = pl.program_id(0)
    nseq = pl.num_programs(0)
    ngroups = npages // gp
    page = ckv_buf.shape[1] // gp
    ql = qlat_ref[...]
    qp = qpe_ref[...]
    rq = ql.shape[0]

    def group_copies(seq, g):
        slot = g % nbuf
        cps = []
        for p in range(gp):
            pid = pt_ref[seq * npages + g * gp + p]
            cps.append(pltpu.make_async_copy(ckv_hbm.at[layer, pid],
                                             ckv_buf.at[slot, pl.ds(p * page, page), :], sem.at[slot]))
            cps.append(pltpu.make_async_copy(kpt_hbm.at[layer, pid],
                                             kpt_buf.at[slot, :, pl.ds(p * page, page)], sem.at[slot]))
        return cps

    @pl.when(s == 0)
    def _():
        for g in range(nbuf - 1):
            for cp in group_copies(0, g):
                cp.start()

    m_sc[...] = jnp.full(m_sc.shape, -jnp.inf, F32)
    l_sc[...] = jnp.zeros(l_sc.shape, F32)
    acc_sc[...] = jnp.zeros(acc_sc.shape, F32)

    def accumulate(kl, s):
        parts = [s[:, c * LANES:(c + 1) * LANES] for c in range(s.shape[1] // LANES)]
        m_old = m_sc[...]
        m_new = jnp.maximum(m_old, jnp.max(functools.reduce(jnp.maximum, parts), axis=-1, keepdims=True))
        alpha = jnp.exp(m_old - m_new)
        ps = [jnp.exp(part - m_new) for part in parts]
        p = ps[0] if len(ps) == 1 else jnp.concatenate(ps, axis=1)
        l_sc[...] = alpha * l_sc[...] + functools.reduce(jnp.add, ps)
        acc_sc[...] = (jnp.concatenate([alpha] * (KV_LORA // LANES), axis=1) * acc_sc[...]
                       + _dot(p.astype(BF16), kl))
        m_sc[...] = m_new

    def scores(g):
        for cp in group_copies(s, g):
            cp.wait()
        slot = g % nbuf
        kl = ckv_buf[slot].astype(BF16)
        return kl, _dot_nt(ql, kl) + _dot(qp, kpt_buf[slot].astype(BF16))

    cur = scores(0)
    for g in range(ngroups):
        nxt = g + nbuf - 1
        if nxt < ngroups:
            for cp in group_copies(s, nxt):
                cp.start()
        else:
            @pl.when(s + 1 < nseq)
            def _():
                for cp in group_copies(s + 1, nxt - ngroups):
                    cp.start()
        ahead = scores(g + 1) if g + 1 < ngroups else None
        accumulate(*cur)
        cur = ahead

    kn = knl_ref[0]
    sn = _dot_nt(ql, kn) + _dot_nt(qp, knp_ref[0])
    n = sn.shape[1]
    tok = lax.broadcasted_iota(jnp.int32, (n_new, rq // n_new, n), 0).reshape(rq, n)
    col = lax.broadcasted_iota(jnp.int32, (rq, n), 1)
    accumulate(kn, jnp.where(col <= tok, sn, -jnp.inf))
    o_ref[...] = acc_sc[...] / jnp.sum(l_sc[...], axis=-1, keepdims=True)


DECODE_GROUP_PAGES = 16
DECODE_RING_SLOTS = 4


def _decode_attn(page_table, qlat, qpe, cache_ckv, cache_kpt, layer, knl, knp):
    nseq, npages = page_table.shape
    page = cache_ckv.shape[2]
    n_new = qlat.shape[0] // (nseq * MLA_HEADS)
    rq = n_new * MLA_HEADS
    gp, nbuf = DECODE_GROUP_PAGES, DECODE_RING_SLOTS
    assert npages % gp == 0 and (npages // gp) % nbuf == 0
    pt = page_table.reshape(-1)
    in_specs = [pl.BlockSpec((rq, KV_LORA), lambda s, pt_ref: (s, 0)),
                pl.BlockSpec((rq, QK_ROPE), lambda s, pt_ref: (s, 0)),
                pl.BlockSpec((1,) + knl.shape[1:], lambda s, pt_ref: (s, 0, 0)),
                pl.BlockSpec((1,) + knp.shape[1:], lambda s, pt_ref: (s, 0, 0)),
                pl.BlockSpec(memory_space=pl.ANY), pl.BlockSpec(memory_space=pl.ANY)]
    grid_spec = pltpu.PrefetchScalarGridSpec(
        num_scalar_prefetch=1, grid=(nseq,), in_specs=in_specs,
        out_specs=pl.BlockSpec((rq, KV_LORA), lambda s, pt_ref: (s, 0)),
        scratch_shapes=[pltpu.VMEM((nbuf, gp * page, KV_LORA), F32), pltpu.VMEM((nbuf, QK_ROPE, gp * page), F32),
                        pltpu.SemaphoreType.DMA((nbuf,)),
                        pltpu.VMEM((rq, LANES), F32), pltpu.VMEM((rq, LANES), F32),
                        pltpu.VMEM((rq, KV_LORA), F32)])
    return pl.pallas_call(
        functools.partial(_decode_kernel, layer=layer, npages=npages, gp=gp, nbuf=nbuf, n_new=n_new),
        grid_spec=grid_spec,
        out_shape=jax.ShapeDtypeStruct((nseq * rq, KV_LORA), F32),
        compiler_params=_cparams("arbitrary"), name="decode_attn",
    )(pt, qlat, qpe, knl, knp, cache_ckv, cache_kpt)


def _mla_out_kernel(o_ref, x_ref, wuv_ref, wo_ref, gpost_ref, y_ref, ocat_sc):
    for hd in range(MLA_HEADS):
        oh = _dot(o_ref[:, hd * KV_LORA:(hd + 1) * KV_LORA].astype(BF16), wuv_ref[hd])
        ocat_sc[:, hd * V_HEAD:(hd + 1) * V_HEAD] = oh
    m = _dot(ocat_sc[...].astype(BF16), wo_ref[...])
    y_ref[...] = x_ref[...] + _rms(m, gpost_ref[...])


def _mla_out_small(o, x, wuv, wo, gpost):
    R, D = x.shape
    return pl.pallas_call(
        _mla_out_kernel, grid=(1,),
        in_specs=[_full(o.shape), _full(x.shape), _full(wuv.shape), _full(wo.shape), _full(gpost.shape)],
        out_specs=_full((R, D)), out_shape=jax.ShapeDtypeStruct((R, D), F32),
        scratch_shapes=[pltpu.VMEM((R, MLA_HEADS * V_HEAD), F32)],
        compiler_params=_cparams("arbitrary"), name="mla_out_small",
    )(o, x, wuv, wo, gpost)


def _mlp_kernel(*refs, fc, attn):
    if attn:
        a_ref, x_ref, wo_ref, gmix_ref, gpre_ref, wup_ref, wdn_ref, gpost_ref, o_ref = refs
        x = x_ref[...] + _rms(_dot(a_ref[...], wo_ref[...]), gmix_ref[...])
    else:
        x_ref, gpre_ref, wup_ref, wdn_ref, gpost_ref, o_ref = refs
        x = x_ref[...]
    h = _rms(x, gpre_ref[...]).astype(BF16)
    dff = wup_ref.shape[1]
    acc = None
    for c in range(dff // fc):
        u = _dot(h, wup_ref[:, c * fc:(c + 1) * fc])
        u = jnp.square(jnp.maximum(u, 0.0)).astype(BF16)
        d = _dot(u, wdn_ref[c * fc:(c + 1) * fc, :])
        acc = d if acc is None else acc + d
    o_ref[...] = x + _rms(acc, gpost_ref[...])


def _mlp(x, gpre, wup, wdn, gpost, tm, fc=1024, attn=None):
    N, D = x.shape
    const = lambda a: pl.BlockSpec(a.shape, lambda i: (0, 0), pipeline_mode=pl.Buffered(1))
    rows = lambda: pl.BlockSpec((tm, D), lambda i: (i, 0))
    args = [x, gpre, wup, wdn, gpost]
    in_specs = [rows(), const(gpre), const(wup), const(wdn), const(gpost)]
    if attn is not None:
        a, wo, gmix = attn
        args = [a, x, wo, gmix] + args[1:]
        in_specs = [rows(), rows(), const(wo), const(gmix)] + in_specs[1:]
    return pl.pallas_call(
        functools.partial(_mlp_kernel, fc=fc, attn=attn is not None), grid=(N // tm,),
        in_specs=in_specs, out_specs=rows(),
        out_shape=jax.ShapeDtypeStruct((N, D), F32),
        compiler_params=_cparams("arbitrary"), name="mlp",
    )(*args)


def _gla_proj_kernel(x_ref, gpre_ref, win_ref, wg_ref, bg_ref, q_ref, k_ref, v_ref, r_ref, la_ref):
    hk = GLA_HEADS * GLA_DK
    hv = GLA_HEADS * GLA_DV
    h = _rms(x_ref[...], gpre_ref[...]).astype(BF16)
    a = _dot(h, win_ref[...])
    q_ref[...] = a[:, :hk] * GLA_SCALE
    k_ref[...] = a[:, hk:2 * hk]
    v_ref[...] = a[:, 2 * hk:2 * hk + hv]
    r_ref[...] = a[:, 2 * hk + hv:2 * hk + 2 * hv]
    gd = a[:, 2 * hk + 2 * hv:].astype(BF16)
    z = _dot(gd, wg_ref[...]) + bg_ref[...]
    la_ref[...] = (jnp.minimum(z, 0.0) - jnp.log(1.0 + jnp.exp(-jnp.abs(z)))) * (1.0 / GATE_TAU)


def _gla_proj(x, gpre, win, wg, bg, tr):
    N, D = x.shape
    hk = GLA_HEADS * GLA_DK
    hv = GLA_HEADS * GLA_DV
    row = lambda w: pl.BlockSpec((tr, w), lambda i: (i, 0))
    return pl.pallas_call(
        _gla_proj_kernel, grid=(N // tr,),
        in_specs=[row(D), _full(gpre.shape), _full(win.shape), _full(wg.shape), _full(bg.shape)],
        out_specs=[row(hk), row(hk), row(hv), row(hv), row(hk)],
        out_shape=[jax.ShapeDtypeStruct((N, w), F32) for w in (hk, hk, hv, hv, hk)],
        compiler_params=_cparams("arbitrary"), name="gla_proj",
    )(x, gpre, win, wg, bg)


def _gla_gate(o, r, gn):
    return _rms(o, gn) * (r / (1.0 + jnp.exp(-r)))


def _cumsum_rows(x, chunk):
    pos = lax.broadcasted_iota(jnp.int32, x.shape, 0) % chunk
    d = 1
    while d < chunk:
        x = x + jnp.where(pos >= d, pltpu.roll(x, d, axis=0), 0.0)
        d *= 2
    return x


def _gla_prompt_kernel(q_ref, k_ref, v_ref, r_ref, la_ref, x_ref, s0_ref, gn_ref, wo_ref, gpost_ref,
                       y_ref, sfin_ref, s_sc, ocat_sc, *, rows):
    c_sz = GLA_CHUNK
    nc = rows // c_sz

    @pl.when(pl.program_id(1) == 0)
    def _():
        s_sc[...] = s0_ref[...]

    b = _cumsum_rows(la_ref[0], c_sz)
    b_last = [b[(c + 1) * c_sz - 1:(c + 1) * c_sz, :] for c in range(nc)]
    k = k_ref[0]
    q_in = (q_ref[0] * jnp.exp(b)).astype(BF16)
    k_in = (k * jnp.exp(-b)).astype(BF16)
    k_dec = jnp.concatenate([k[c * c_sz:(c + 1) * c_sz] * jnp.exp(b_last[c] - b[c * c_sz:(c + 1) * c_sz])
                             for c in range(nc)], axis=0).astype(BF16)
    dec = [jnp.exp(bl) for bl in b_last]
    ri = lax.broadcasted_iota(jnp.int32, (rows, rows), 0)
    ci = lax.broadcasted_iota(jnp.int32, (rows, rows), 1)
    tril = jnp.where(ci <= ri, ci, -1) >= (ri // c_sz) * c_sz
    gn = gn_ref[...]
    for hd in range(GLA_HEADS):
        ks = slice(hd * GLA_DK, (hd + 1) * GLA_DK)
        vs = slice(hd * GLA_DV, (hd + 1) * GLA_DV)
        v = v_ref[0, :, vs].astype(BF16)
        a = jnp.where(tril, _dot_nt(q_in[:, ks], k_in[:, ks]), 0.0).astype(BF16)
        o_intra = _dot(a, v)
        kvs = [_dot_tn(k_dec[c * c_sz:(c + 1) * c_sz, ks], v[c * c_sz:(c + 1) * c_sz]) for c in range(nc)]
        s = s_sc[hd]
        o_inter = []
        for c in range(nc):
            o_inter.append(_dot(q_in[c * c_sz:(c + 1) * c_sz, ks], s.astype(BF16)))
            dcol = jnp.transpose(jnp.broadcast_to(dec[c][:, ks], (GLA_DK, GLA_DK)))
            s = jnp.concatenate([dcol] * (GLA_DV // GLA_DK), axis=1) * s + kvs[c]
        s_sc[hd] = s
        o = o_intra + jnp.concatenate(o_inter, axis=0)
        ocat_sc[:, vs] = _gla_gate(o, r_ref[0, :, vs], gn)
    m = _dot(ocat_sc[...].astype(BF16), wo_ref[...])
    y_ref[0] = x_ref[0] + _rms(m, gpost_ref[...])

    @pl.when(pl.program_id(1) == pl.num_programs(1) - 1)
    def _():
        sfin_ref[0] = s_sc[...]


def _gla_prompt(q, k, v, r, la, x, s0, gn, wo, gpost, rows):
    B, T, D = x.shape
    hk = GLA_HEADS * GLA_DK
    hv = GLA_HEADS * GLA_DV
    blk = lambda w: pl.BlockSpec((1, rows, w), lambda b, i: (b, i, 0))
    return pl.pallas_call(
        functools.partial(_gla_prompt_kernel, rows=rows), grid=(B, T // rows),
        in_specs=[blk(hk), blk(hk), blk(hv), blk(hv), blk(hk), blk(D), _full(s0.shape), _full(gn.shape),
                  _full(wo.shape), _full(gpost.shape)],
        out_specs=[blk(D), pl.BlockSpec((1,) + s0.shape, lambda b, i: (b, 0, 0, 0))],
        out_shape=[jax.ShapeDtypeStruct((B, T, D), F32), jax.ShapeDtypeStruct((B,) + s0.shape, F32)],
        scratch_shapes=[pltpu.VMEM(s0.shape, F32), pltpu.VMEM((rows, hv), F32)],
        compiler_params=_cparams("arbitrary", "arbitrary"), name="gla_prompt",
    )(q, k, v, r, la, x, s0, gn, wo, gpost)


GLA_SHORT_ROWS = 16


def _gla_tokens_kernel(q_ref, k_ref, la_ref, v_ref, s0_ref, o_ref, sfin_ref):
    nblk, rows, _ = q_ref.shape
    zk = jnp.zeros((LANES - rows, GLA_DK), BF16)
    zv = jnp.zeros((LANES - rows, GLA_DV), BF16)
    tril = (lax.broadcasted_iota(jnp.int32, (rows, LANES), 1) <= lax.broadcasted_iota(jnp.int32, (rows, LANES), 0))
    for n in range(nblk):
        b = _cumsum_rows(la_ref[n], rows)
        b_last = b[rows - 1:rows, :]
        k = k_ref[n]
        q_in = (q_ref[n] * jnp.exp(b)).astype(BF16)
        k_in = (k * jnp.exp(-b)).astype(BF16)
        k_dec = (k * jnp.exp(b_last - b)).astype(BF16)
        dec = jnp.exp(b_last)
        for hd in range(GLA_HEADS):
            ks = slice(hd * GLA_DK, (hd + 1) * GLA_DK)
            vs = slice(hd * GLA_DV, (hd + 1) * GLA_DV)
            v = jnp.concatenate([v_ref[n, :, vs].astype(BF16), zv], axis=0)
            a = jnp.where(tril, _dot_nt(q_in[:, ks], jnp.concatenate([k_in[:, ks], zk], axis=0)), 0.0)
            s = s0_ref[n, hd]
            o_ref[n, :, vs] = _dot(a.astype(BF16), v) + _dot(q_in[:, ks], s.astype(BF16))
            dcol = jnp.transpose(jnp.broadcast_to(dec[:, ks], (GLA_DK, GLA_DK)))
            sfin_ref[n, hd] = (jnp.concatenate([dcol] * (GLA_DV // GLA_DK), axis=1) * s
                               + _dot_tn(jnp.concatenate([k_dec[:, ks], zk], axis=0), v))


def _gla_tokens(q, k, la, v, s0, nblk):
    nseq, rows, hk = q.shape
    hv = v.shape[-1]
    col = pl.BlockSpec((nblk, rows, hk), lambda s: (s, 0, 0))
    val = pl.BlockSpec((nblk, rows, hv), lambda s: (s, 0, 0))
    st = pl.BlockSpec((nblk,) + s0.shape[1:], lambda s: (s, 0, 0, 0))
    return pl.pallas_call(
        _gla_tokens_kernel, grid=(nseq // nblk,),
        in_specs=[col, col, col, val, st], out_specs=[val, st],
        out_shape=[jax.ShapeDtypeStruct((nseq, rows, hv), F32), jax.ShapeDtypeStruct(s0.shape, F32)],
        compiler_params=_cparams("arbitrary"), name="gla_tokens",
    )(q, k, la, v, s0)


def _gla_out_kernel(o_ref, r_ref, x_ref, gn_ref, wo_ref, gpost_ref, y_ref, ocat_sc):
    gn = gn_ref[...]
    for hd in range(GLA_HEADS):
        vs = slice(hd * GLA_DV, (hd + 1) * GLA_DV)
        ocat_sc[:, vs] = _gla_gate(o_ref[:, vs], r_ref[:, vs], gn)
    m = _dot(ocat_sc[...].astype(BF16), wo_ref[...])
    y_ref[...] = x_ref[...] + _rms(m, gpost_ref[...])


def _gla_out_small(o, r, x, gn, wo, gpost):
    R, D = x.shape
    return pl.pallas_call(
        _gla_out_kernel, grid=(1,),
        in_specs=[_full(o.shape), _full(r.shape), _full(x.shape), _full(gn.shape), _full(wo.shape),
                  _full(gpost.shape)],
        out_specs=_full((R, D)), out_shape=jax.ShapeDtypeStruct((R, D), F32),
        scratch_shapes=[pltpu.VMEM(o.shape, F32)],
        compiler_params=_cparams("arbitrary"), name="gla_out_small",
    )(o, r, x, gn, wo, gpost)


def _rope_tables(pos):
    half = QK_ROPE // 2
    inv = ROPE_BASE ** (-jnp.arange(half, dtype=F32) / half)
    ang = pos.astype(F32)[:, None] * inv[None, :]
    c, s = jnp.cos(ang), jnp.sin(ang)
    reps = LANES // QK_ROPE
    return (jnp.tile(jnp.concatenate([c, c], axis=1), (1, reps)),
            jnp.tile(jnp.concatenate([-s, s], axis=1), (1, reps)))


def _rope_tables_head(pos):
    half = QK_ROPE // 2
    inv = ROPE_BASE ** (-jnp.arange(half, dtype=F32) / half)
    ang = pos.astype(F32)[:, None] * inv[None, :]
    c, s = jnp.cos(ang), jnp.sin(ang)
    z = lambda n: jnp.zeros((pos.shape[0], n), F32)
    tail = HEAD_W - QK_NOPE - QK_ROPE
    return (jnp.concatenate([z(QK_NOPE), c, c, z(tail)], axis=1),
            jnp.concatenate([z(QK_NOPE), -s, s, z(tail)], axis=1))


def _swap_halves(w):
    half = QK_ROPE // 2
    return jnp.concatenate([w[..., half:], w[..., :half]], axis=-1)


def kernel(x_prompt, x_sample, cache_ckv, cache_kpe, state_gla, page_table, meta_tokens, norm_pre_mix, norm_post_mix, norm_pre_mlp, norm_post_mlp, mla_w_in, mla_q_norm, mla_w_uq, mla_kv_norm, mla_w_uk, mla_w_uv, mla_w_o, gla_w_in, gla_w_gate, gla_b_gate, gla_norm, gla_w_o, mlp_w_up, mlp_w_down):
    B, T, D = x_prompt.shape
    DB, S, _ = x_sample.shape
    H = MLA_HEADS
    n_s = DB * S
    past_len = page_table.shape[1] * cache_ckv.shape[2]
    row = lambda a: a.reshape(1, -1)

    w_in = mla_w_in[0]
    o = Q_LORA + KV_LORA
    w_kpe = w_in[:, o:]
    zpad = jnp.zeros((D, LANES - QK_ROPE), F32)
    w_in_p = jnp.concatenate([w_in[:, :o], w_kpe, zpad, _swap_halves(w_kpe), zpad], axis=1).astype(BF16)
    w_uq = mla_w_uq[0].reshape(Q_LORA, H, QK_NOPE + QK_ROPE)
    w_nope = jnp.transpose(w_uq[:, :, :QK_NOPE], (1, 0, 2))
    w_uk = jnp.transpose(mla_w_uk[0], (1, 0, 2))
    w_qlat = _fold_qk(w_nope, w_uk)
    w_qpe = w_uq[:, :, QK_NOPE:]
    w_qp = w_qpe.reshape(Q_LORA, H * QK_ROPE).astype(BF16)
    w_qps = _swap_halves(w_qpe).reshape(Q_LORA, H * QK_ROPE).astype(BF16)
    proj_w = (row(norm_pre_mix[0]), w_in_p, row(mla_q_norm[0]), w_qlat, w_qp, w_qps, row(mla_kv_norm[0]))
    w_uv = jnp.transpose(mla_w_uv[0], (1, 0, 2)).astype(BF16)
    w_o = mla_w_o[0].astype(BF16)
    g_post0 = row(norm_post_mix[0])

    x_small = jnp.concatenate([x_sample.reshape(n_s, D), meta_tokens], axis=0)
    pos_small = jnp.concatenate([past_len + jnp.tile(jnp.arange(S, dtype=jnp.int32), DB),
                                 jnp.arange(N_META, dtype=jnp.int32)])
    cos_s, sin_s = _rope_tables(pos_small)
    qlat_s, qpe_s, ckv_s, kpe_s, klat_s, kpeb_s = _mla_proj_small(x_small, cos_s, sin_s, proj_w)

    npad = LANES
    kml = jnp.pad(klat_s[n_s:], ((0, npad - N_META), (0, 0)))
    kmp = jnp.pad(kpeb_s[n_s:], ((0, npad - N_META), (0, 0)))
    o_meta = _meta_attn(qlat_s[n_s:].reshape(N_META * H, KV_LORA), qpe_s[n_s:].reshape(N_META * H, QK_ROPE),
                        kml, kmp)
    knl = jnp.pad(klat_s[:n_s].reshape(DB, S, KV_LORA), ((0, 0), (0, npad - S), (0, 0)))
    knp = jnp.pad(kpeb_s[:n_s].reshape(DB, S, QK_ROPE), ((0, 0), (0, npad - S), (0, 0)))
    o_samp = _decode_attn(page_table, qlat_s[:n_s].reshape(n_s * H, KV_LORA),
                          qpe_s[:n_s].reshape(n_s * H, QK_ROPE), cache_ckv, jnp.swapaxes(cache_kpe, 2, 3),
                          0, knl, knp)
    o_small = jnp.concatenate([o_samp.reshape(n_s, H * KV_LORA), o_meta.reshape(N_META, H * KV_LORA)], axis=0)
    xs1 = _mla_out_small(o_small, x_small, w_uv, w_o, g_post0)

    zl = lambda n: jnp.zeros((D, n), F32)
    pad_r = HEAD_W - QK_NOPE - QK_ROPE
    w_in_m = jnp.concatenate([w_in[:, :o], zl(QK_NOPE), w_kpe, zl(pad_r),
                              zl(QK_NOPE), _swap_halves(w_kpe), zl(pad_r)], axis=1).astype(BF16)
    wq = jnp.pad(w_uq, ((0, 0), (0, 0), (0, pad_r))).reshape(Q_LORA, H * HEAD_W).astype(BF16)
    wqs = jnp.pad(_swap_halves(w_qpe), ((0, 0), (0, 0), (QK_NOPE, pad_r))).reshape(Q_LORA, H * HEAD_W).astype(BF16)
    wk = jnp.pad(mla_w_uk[0], ((0, 0), (0, 0), (0, HEAD_W - QK_NOPE))).reshape(KV_LORA, H * HEAD_W).astype(BF16)
    wv = mla_w_uv[0].reshape(KV_LORA, H * V_HEAD).astype(BF16)
    mha_w = (row(norm_pre_mix[0]), w_in_m, row(mla_q_norm[0]), wq, wqs, row(mla_kv_norm[0]), wk, wv)
    cosk, sink = _rope_tables_head(N_META + jnp.arange(T, dtype=jnp.int32))
    q_p, k_p, v_p, ckv_p, kpe_p = _mha_proj(x_prompt, cosk, sink, mha_w, tr=512)
    cosm, sinm = _rope_tables_head(jnp.arange(N_META, dtype=jnp.int32))
    _, k_m, v_m, _, _ = _mha_proj(meta_tokens[None], cosm, sinm, mha_w, tr=N_META)
    mpad = ((0, 0), (0, 0), (0, LANES - N_META), (0, 0))
    attn_p = _mha_flash(q_p, k_p, v_p, jnp.pad(k_m, mpad), jnp.pad(v_m, mpad), tq=512)

    wup0, wdn0 = mlp_w_up[0].astype(BF16), mlp_w_down[0].astype(BF16)
    mlp0 = (row(norm_pre_mlp[0]), wup0, wdn0, row(norm_post_mlp[0]))
    xp2 = _mlp(x_prompt.reshape(B * T, D), *mlp0, tm=512, attn=(attn_p.reshape(B * T, D), w_o, g_post0))
    xs2 = _mlp(xs1, *mlp0, tm=xs1.shape[0])

    hk = GLA_HEADS * GLA_DK
    hv = GLA_HEADS * GLA_DV
    gw = gla_w_in[0]
    gw_p = jnp.concatenate([gw, jnp.zeros((D, LANES - GATE_RANK), F32)], axis=1).astype(BF16)
    wg_p = jnp.concatenate([gla_w_gate[0], jnp.zeros((LANES - GATE_RANK, hk), F32)], axis=0).astype(BF16)
    gla_pw = (row(norm_pre_mix[1]), gw_p, wg_p, row(gla_b_gate[0]))
    gn = row(gla_norm[0])
    gwo = gla_w_o[0].astype(BF16)
    g_post1 = row(norm_post_mix[1])

    q_s, k_s, v_s, r_s, la_s = _gla_proj(xs2, *gla_pw, tr=xs2.shape[0])
    def seqs(a, n, l):
        return jnp.pad(a.reshape(n, l, a.shape[-1]), ((0, 0), (0, GLA_SHORT_ROWS - l), (0, 0)))

    o_gs, s_gs = _gla_tokens(*(seqs(a[:n_s], DB, S) for a in (q_s, k_s, la_s, v_s)), state_gla[0], nblk=4)
    zero_state = jnp.zeros((1, GLA_HEADS, GLA_DK, GLA_DV), F32)
    o_gm, s_gm = _gla_tokens(*(seqs(a[n_s:], 1, N_META) for a in (q_s, k_s, la_s, v_s)), zero_state, nblk=1)
    o_gsmall = jnp.concatenate([o_gs[:, :S].reshape(n_s, hv), o_gm[:, :N_META].reshape(N_META, hv)], axis=0)
    xs3 = _gla_out_small(o_gsmall, r_s, xs2, gn, gwo, g_post1)

    q_p, k_p, v_p, r_p, la_p = _gla_proj(xp2, *gla_pw, tr=512)
    r3 = lambda a: a.reshape(B, T, a.shape[-1])
    xp3, s_gp = _gla_prompt(r3(q_p), r3(k_p), r3(v_p), r3(r_p), r3(la_p), xp2.reshape(B, T, D), s_gm[0],
                            gn, gwo, g_post1, rows=256)

    mlp1 = (row(norm_pre_mlp[1]), mlp_w_up[1].astype(BF16), mlp_w_down[1].astype(BF16), row(norm_post_mlp[1]))
    y_prompt = _mlp(xp3.reshape(B * T, D), *mlp1, tm=512).reshape(B, T, D)
    xs4 = _mlp(xs3, *mlp1, tm=xs3.shape[0])
    y_sample = xs4[:n_s].reshape(DB, S, D)

    bmeta = lambda a: jnp.broadcast_to(a[None], (B,) + a.shape)
    new_ckv_prompt = jnp.concatenate([bmeta(ckv_s[n_s:]), ckv_p], axis=1)[None]
    new_kpe_prompt = jnp.concatenate([bmeta(kpe_s[n_s:]), kpe_p], axis=1)[None]
    new_ckv_sample = ckv_s[:n_s].reshape(1, DB, S, KV_LORA)
    new_kpe_sample = kpe_s[:n_s].reshape(1, DB, S, QK_ROPE)
    return (y_prompt, y_sample, new_ckv_prompt, new_kpe_prompt, new_ckv_sample, new_kpe_sample,
            s_gp[None], s_gs[None])
```

```python
import functools

import jax
import jax.numpy as jnp
import numpy as np
from jax import lax
from jax.experimental import pallas as pl
from jax.experimental.pallas import tpu as pltpu

F32 = jnp.float32
BF16 = jnp.bfloat16

N_META = 16
MLA_HEADS = 16
Q_LORA = 256
KV_LORA = 256
QK_NOPE = 64
QK_ROPE = 32
V_HEAD = 64
MLA_SCALE = (QK_NOPE + QK_ROPE) ** -0.5
ROPE_BASE = 10000.0
GLA_HEADS = 4
GLA_DK = 128
GLA_DV = 256
GLA_SCALE = GLA_DK ** -0.5
GATE_RANK = 16
GATE_TAU = 16.0
GLA_CHUNK = 64
EPS = 1e-6
LOG2E = 1.4426950408889634

LANES = 128
VMEM_LIMIT = 56 * 1024 * 1024


def _cparams(*sem):
    return pltpu.CompilerParams(dimension_semantics=sem, vmem_limit_bytes=VMEM_LIMIT)


def _rms(x, w):
    return x * lax.rsqrt(jnp.mean(x * x, axis=-1, keepdims=True) + EPS) * w


def _dot(a, b):
    return jnp.dot(a, b, preferred_element_type=F32)


def _dot_nt(a, b):
    return lax.dot_general(a, b, (((1,), (1,)), ((), ())), preferred_element_type=F32)


def _dot_tn(a, b):
    return lax.dot_general(a, b, (((0,), (0,)), ((), ())), preferred_element_type=F32)


def _full(shape):
    n = len(shape)
    return pl.BlockSpec(shape, lambda *_: (0,) * n)


def _fold_kernel(a_ref, b_ref, o_ref):
    o_ref[...] = lax.dot_general(a_ref[0], b_ref[0], (((1,), (1,)), ((), ())),
                                 precision=lax.Precision.HIGHEST,
                                 preferred_element_type=F32).astype(o_ref.dtype)


def _fold_qk(w_nope, w_uk):
    return pl.pallas_call(
        _fold_kernel,
        grid=(MLA_HEADS,),
        in_specs=[pl.BlockSpec((1, Q_LORA, QK_NOPE), lambda h: (h, 0, 0)),
                  pl.BlockSpec((1, KV_LORA, QK_NOPE), lambda h: (h, 0, 0))],
        out_specs=pl.BlockSpec((Q_LORA, KV_LORA), lambda h: (0, h)),
        out_shape=jax.ShapeDtypeStruct((Q_LORA, MLA_HEADS * KV_LORA), BF16),
        compiler_params=_cparams("arbitrary"),
        name="fold_qk",
    )(w_nope, w_uk)


def _mla_proj_kernel(x_ref, cos_ref, sin_ref, gpre_ref, win_ref, qn_ref, wql_ref, wqp_ref, wqps_ref,
                     kvn_ref, qlat_ref, qpe_ref, ckv_ref, kpe_ref, klat_ref, kpeb_ref):
    h = _rms(x_ref[...], gpre_ref[...]).astype(BF16)
    a = _dot(h, win_ref[...])
    cqn = _rms(a[:, :Q_LORA], qn_ref[...]).astype(BF16)
    ckv = _rms(a[:, Q_LORA:Q_LORA + KV_LORA], kvn_ref[...])
    cos = cos_ref[...]
    sin = sin_ref[...]
    o = Q_LORA + KV_LORA
    kpe = a[:, o:o + QK_ROPE] * cos[:, :QK_ROPE] + a[:, o + LANES:o + LANES + QK_ROPE] * sin[:, :QK_ROPE]
    qlat = _dot(cqn, wql_ref[...]) * MLA_SCALE
    qp = _dot(cqn, wqp_ref[...])
    qps = _dot(cqn, wqps_ref[...])
    nl = MLA_HEADS * QK_ROPE // LANES
    qpe = jnp.concatenate(
        [(qp[:, j * LANES:(j + 1) * LANES] * cos + qps[:, j * LANES:(j + 1) * LANES] * sin) * MLA_SCALE
         for j in range(nl)], axis=1)
    qlat_ref[...] = qlat.astype(BF16)
    qpe_ref[...] = qpe.astype(BF16)
    ckv_ref[...] = ckv
    kpe_ref[...] = kpe
    klat_ref[...] = ckv.astype(BF16)
    kpeb_ref[...] = kpe.astype(BF16)


def _mla_proj_small(x, cos, sin, w):
    R, D = x.shape
    H = MLA_HEADS
    in_specs = [_full((R, D)), _full((R, LANES)), _full((R, LANES))] + [_full(a.shape) for a in w]
    shapes = [((R, H * KV_LORA), BF16), ((R, H * QK_ROPE), BF16), ((R, KV_LORA), F32),
              ((R, QK_ROPE), F32), ((R, KV_LORA), BF16), ((R, QK_ROPE), BF16)]
    return pl.pallas_call(
        _mla_proj_kernel, grid=(1,), in_specs=in_specs,
        out_specs=[_full(s) for s, _ in shapes],
        out_shape=[jax.ShapeDtypeStruct(s, d) for s, d in shapes],
        compiler_params=_cparams("arbitrary"), name="mla_proj_small",
    )(x, cos, sin, *w)


HEAD_W = LANES


def _mha_proj_kernel(x_ref, cosk_ref, sink_ref, gpre_ref, win_ref, qn_ref, wq_ref, wqs_ref, kvn_ref,
                     wk_ref, wv_ref, q_ref, k_ref, v_ref, ckv_ref, kpe_ref):
    h = _rms(x_ref[0], gpre_ref[...]).astype(BF16)
    a = _dot(h, win_ref[...])
    cqn = _rms(a[:, :Q_LORA], qn_ref[...]).astype(BF16)
    ckv = _rms(a[:, Q_LORA:Q_LORA + KV_LORA], kvn_ref[...])
    cosk = cosk_ref[...]
    sink = sink_ref[...]
    o = Q_LORA + KV_LORA
    kpe = a[:, o:o + HEAD_W] * cosk + a[:, o + HEAD_W:o + 2 * HEAD_W] * sink
    ckv_ref[0] = ckv
    kpe_ref[0] = kpe[:, QK_NOPE:QK_NOPE + QK_ROPE]
    ckv_b = ckv.astype(BF16)
    k_all = _dot(ckv_b, wk_ref[...])
    v_all = _dot(ckv_b, wv_ref[...])
    q_raw = _dot(cqn, wq_ref[...])
    q_swp = _dot(cqn, wqs_ref[...])
    lane = lax.broadcasted_iota(jnp.int32, (1, HEAD_W), 1)
    cosq = cosk + jnp.where(lane < QK_NOPE, 1.0, 0.0)
    for hd in range(MLA_HEADS):
        sl = slice(hd * HEAD_W, (hd + 1) * HEAD_W)
        k_ref[0, hd] = (k_all[:, sl] + kpe).astype(BF16)
        q_ref[0, hd] = ((q_raw[:, sl] * cosq + q_swp[:, sl] * sink) * (MLA_SCALE * LOG2E)).astype(BF16)
    for p in range(MLA_HEADS * V_HEAD // LANES):
        v_ref[0, p] = v_all[:, p * LANES:(p + 1) * LANES].astype(BF16)


def _mha_proj(x, cosk, sink, w, tr):
    B, T, D = x.shape
    H = MLA_HEADS
    nv = H * V_HEAD // LANES
    in_specs = [pl.BlockSpec((1, tr, D), lambda b, i: (b, i, 0)),
                pl.BlockSpec((tr, HEAD_W), lambda b, i: (i, 0)),
                pl.BlockSpec((tr, HEAD_W), lambda b, i: (i, 0))] + [_full(a.shape) for a in w]
    out_shape = [jax.ShapeDtypeStruct((B, H, T, HEAD_W), BF16),
                 jax.ShapeDtypeStruct((B, H, T, HEAD_W), BF16),
                 jax.ShapeDtypeStruct((B, nv, T, LANES), BF16),
                 jax.ShapeDtypeStruct((B, T, KV_LORA), F32),
                 jax.ShapeDtypeStruct((B, T, QK_ROPE), F32)]
    out_specs = [pl.BlockSpec((1, H, tr, HEAD_W), lambda b, i: (b, 0, i, 0)),
                 pl.BlockSpec((1, H, tr, HEAD_W), lambda b, i: (b, 0, i, 0)),
                 pl.BlockSpec((1, nv, tr, LANES), lambda b, i: (b, 0, i, 0)),
                 pl.BlockSpec((1, tr, KV_LORA), lambda b, i: (b, i, 0)),
                 pl.BlockSpec((1, tr, QK_ROPE), lambda b, i: (b, i, 0))]
    return pl.pallas_call(
        _mha_proj_kernel, grid=(B, T // tr), in_specs=in_specs, out_specs=out_specs, out_shape=out_shape,
        compiler_params=_cparams("arbitrary", "arbitrary"), name="mha_proj",
    )(x, cosk, sink, *w)


def _mha_flash_kernel(q_ref, k_ref, v_ref, km_ref, vm_ref, o_ref, m_sc, l_sc, acc_sc, *, tq):
    i = pl.program_id(2)
    nh = q_ref.shape[1]

    def scores(hh, kb, mask):
        s = _dot_nt(q_ref[0, hh], kb)
        return s if mask is None else jnp.where(mask, s, -jnp.inf)

    def apply(hh, s, vb, first):
        parts = [s[:, c * LANES:(c + 1) * LANES] for c in range(s.shape[1] // LANES)]
        mrow = jnp.max(functools.reduce(jnp.maximum, parts), axis=-1, keepdims=True)
        if first:
            m_new = jnp.broadcast_to(mrow, (tq, LANES))
        else:
            m_old = m_sc[hh]
            m_new = jnp.maximum(m_old, mrow)
        ps = [jnp.exp2(part - m_new) for part in parts]
        psum = functools.reduce(jnp.add, ps)
        p = ps[0] if len(ps) == 1 else jnp.concatenate(ps, axis=1)
        pv = _dot(p.astype(BF16), vb)
        if first:
            l_sc[hh] = psum
            acc_sc[hh] = pv
        else:
            alpha = jnp.exp2(m_old - m_new)
            l_sc[hh] = alpha * l_sc[hh] + psum
            acc_sc[hh] = alpha * acc_sc[hh] + pv
        m_sc[hh] = m_new

    nm = km_ref.shape[2]
    hpg = nh // v_ref.shape[1]

    def step(kblock, vblock, mask, first):
        ss = [scores(hh, kblock(hh), mask) for hh in range(nh)]
        for hh in range(nh):
            apply(hh, ss[hh], vblock(hh // hpg), first)

    off = pl.multiple_of(i * tq, tq)
    col = lax.broadcasted_iota(jnp.int32, (1, tq + nm), 1)
    need = jnp.where(col < tq, col, jnp.where(col - tq < N_META, -1, tq))
    mask = lax.broadcasted_iota(jnp.int32, (tq, tq + nm), 0) >= need
    step(lambda hh: jnp.concatenate([k_ref[0, hh, pl.ds(off, tq), :], km_ref[0, hh]], axis=0),
         lambda g: jnp.concatenate([v_ref[0, g, pl.ds(off, tq), :], vm_ref[0, g]], axis=0), mask, True)

    def body(j, carry):
        offj = pl.multiple_of(j * tq, tq)
        step(lambda hh: k_ref[0, hh, pl.ds(offj, tq), :], lambda g: v_ref[0, g, pl.ds(offj, tq), :], None, False)
        return carry

    lax.fori_loop(0, i, body, 0)

    lane = lax.broadcasted_iota(jnp.int32, (tq, LANES), 1)
    for g in range(nh // hpg):
        out = None
        for r in range(hpg):
            hh = g * hpg + r
            o = acc_sc[hh] / jnp.sum(l_sc[hh], axis=-1, keepdims=True)
            out = o if out is None else jnp.where(lane >= r * V_HEAD, o, out)
        o_ref[0, :, g * LANES:(g + 1) * LANES] = out.astype(o_ref.dtype)


MHA_GROUPS_PER_STEP = 2


def _mha_flash(q, k, v, km, vm, tq):
    B, H, T, W = q.shape
    nv = v.shape[1]
    ng = MHA_GROUPS_PER_STEP
    nh = H // nv * ng
    return pl.pallas_call(
        functools.partial(_mha_flash_kernel, tq=tq), grid=(B, nv // ng, T // tq),
        in_specs=[pl.BlockSpec((1, nh, tq, W), lambda b, p, i: (b, p, i, 0)),
                  pl.BlockSpec((1, nh, T, W), lambda b, p, i: (b, p, 0, 0)),
                  pl.BlockSpec((1, ng, T, LANES), lambda b, p, i: (b, p, 0, 0)),
                  pl.BlockSpec((1, nh) + km.shape[2:], lambda b, p, i: (0, p, 0, 0)),
                  pl.BlockSpec((1, ng) + vm.shape[2:], lambda b, p, i: (0, p, 0, 0))],
        out_specs=pl.BlockSpec((1, tq, ng * LANES), lambda b, p, i: (b, i, p)),
        out_shape=jax.ShapeDtypeStruct((B, T, nv * LANES), BF16),
        scratch_shapes=[pltpu.VMEM((nh, tq, LANES), F32), pltpu.VMEM((nh, tq, LANES), F32),
                        pltpu.VMEM((nh, tq, LANES), F32)],
        compiler_params=_cparams("arbitrary", "arbitrary", "arbitrary"), name="mha_flash",
    )(q, k, v, km, vm)


def _meta_attn_kernel(qlat_ref, qpe_ref, kl_ref, kp_ref, o_ref):
    kl = kl_ref[...]
    s = _dot_nt(qlat_ref[...], kl) + _dot_nt(qpe_ref[...], kp_ref[...])
    r, n = s.shape
    tok = lax.broadcasted_iota(jnp.int32, (r // MLA_HEADS, MLA_HEADS, n), 0).reshape(r, n)
    col = lax.broadcasted_iota(jnp.int32, (r, n), 1)
    s = jnp.where(col <= tok, s, -jnp.inf)
    p = jnp.exp(s - jnp.max(s, axis=-1, keepdims=True))
    l = jnp.sum(p, axis=-1, keepdims=True)
    o_ref[...] = _dot(p.astype(BF16), kl) / l


def _meta_attn(qlat, qpe, kl, kp):
    r = qlat.shape[0]
    return pl.pallas_call(
        _meta_attn_kernel, grid=(1,),
        in_specs=[_full(qlat.shape), _full(qpe.shape), _full(kl.shape), _full(kp.shape)],
        out_specs=_full((r, KV_LORA)), out_shape=jax.ShapeDtypeStruct((r, KV_LORA), F32),
        compiler_params=_cparams("arbitrary"), name="meta_attn",
    )(qlat, qpe, kl, kp)


def _decode_kernel(pt_ref, qlat_ref, qpe_ref, knl_ref, knp_ref, ckv_hbm, kpt_hbm, o_ref,
                   ckv_buf, kpt_buf, sem, m_sc, l_sc, acc_sc, *, layer, npages, gp, nbuf, n_new):
    s = pl.program_id(0)
    nseq = pl.num_programs(0)
    ngroups = npages // gp
    page = ckv_buf.shape[1] // gp
    ql = qlat_ref[...]
    qp = qpe_ref[...]
    rq = ql.shape[0]

    def group_copies(seq, g):
        slot = g % nbuf
        cps = []
        for p in range(gp):
            pid = pt_ref[seq * npages + g * gp + p]
            cps.append(pltpu.make_async_copy(ckv_hbm.at[layer, pid],
                                             ckv_buf.at[slot, pl.ds(p * page, page), :], sem.at[slot]))
            cps.append(pltpu.make_async_copy(kpt_hbm.at[layer, pid],
                                             kpt_buf.at[slot, :, pl.ds(p * page, page)], sem.at[slot]))
        return cps

    @pl.when(s == 0)
    def _():
        for g in range(nbuf - 1):
            for cp in group_copies(0, g):
                cp.start()

    m_sc[...] = jnp.full(m_sc.shape, -jnp.inf, F32)
    l_sc[...] = jnp.zeros(l_sc.shape, F32)
    acc_sc[...] = jnp.zeros(acc_sc.shape, F32)

    def accumulate(kl, s):
        parts = [s[:, c * LANES:(c + 1) * LANES] for c in range(s.shape[1] // LANES)]
        m_old = m_sc[...]
        m_new = jnp.maximum(m_old, jnp.max(functools.reduce(jnp.maximum, parts), axis=-1, keepdims=True))
        alpha = jnp.exp(m_old - m_new)
        ps = [jnp.exp(part - m_new) for part in parts]
        p = ps[0] if len(ps) == 1 else jnp.concatenate(ps, axis=1)
        l_sc[...] = alpha * l_sc[...] + functools.reduce(jnp.add, ps)
        acc_sc[...] = (jnp.concatenate([alpha] * (KV_LORA // LANES), axis=1) * acc_sc[...]
                       + _dot(p.astype(BF16), kl))
        m_sc[...] = m_new

    def scores(g):
        for cp in group_copies(s, g):
            cp.wait()
        slot = g % nbuf
        kl = ckv_buf[slot].astype(BF16)
        return kl, _dot_nt(ql, kl) + _dot(qp, kpt_buf[slot].astype(BF16))

    cur = scores(0)
    for g in range(ngroups):
        nxt = g + nbuf - 1
        if nxt < ngroups:
            for cp in group_copies(s, nxt):
                cp.start()
        else:
            @pl.when(s + 1 < nseq)
            def _():
                for cp in group_copies(s + 1, nxt - ngroups):
                    cp.start()
        ahead = scores(g + 1) if g + 1 < ngroups else None
        accumulate(*cur)
        cur = ahead

    kn = knl_ref[0]
    sn = _dot_nt(ql, kn) + _dot_nt(qp, knp_ref[0])
    n = sn.shape[1]
    tok = lax.broadcasted_iota(jnp.int32, (n_new, rq // n_new, n), 0).reshape(rq, n)
    col = lax.broadcasted_iota(jnp.int32, (rq, n), 1)
    accumulate(kn, jnp.where(col <= tok, sn, -jnp.inf))
    o_ref[...] = acc_sc[...] / jnp.sum(l_sc[...], axis=-1, keepdims=True)


DECODE_GROUP_PAGES = 16
DECODE_RING_SLOTS = 4


def _decode_attn(page_table, qlat, qpe, cache_ckv, cache_kpt, layer, knl, knp):
    nseq, npages = page_table.shape
    page = cache_ckv.shape[2]
    n_new = qlat.shape[0] // (nseq * MLA_HEADS)
    rq = n_new * MLA_HEADS
    gp, nbuf = DECODE_GROUP_PAGES, DECODE_RING_SLOTS
    assert npages % gp == 0 and (npages // gp) % nbuf == 0
    pt = page_table.reshape(-1)
    in_specs = [pl.BlockSpec((rq, KV_LORA), lambda s, pt_ref: (s, 0)),
                pl.BlockSpec((rq, QK_ROPE), lambda s, pt_ref: (s, 0)),
                pl.BlockSpec((1,) + knl.shape[1:], lambda s, pt_ref: (s, 0, 0)),
                pl.BlockSpec((1,) + knp.shape[1:], lambda s, pt_ref: (s, 0, 0)),
                pl.BlockSpec(memory_space=pl.ANY), pl.BlockSpec(memory_space=pl.ANY)]
    grid_spec = pltpu.PrefetchScalarGridSpec(
        num_scalar_prefetch=1, grid=(nseq,), in_specs=in_specs,
        out_specs=pl.BlockSpec((rq, KV_LORA), lambda s, pt_ref: (s, 0)),
        scratch_shapes=[pltpu.VMEM((nbuf, gp * page, KV_LORA), F32), pltpu.VMEM((nbuf, QK_ROPE, gp * page), F32),
                        pltpu.SemaphoreType.DMA((nbuf,)),
                        pltpu.VMEM((rq, LANES), F32), pltpu.VMEM((rq, LANES), F32),
                        pltpu.VMEM((rq, KV_LORA), F32)])
    return pl.pallas_call(
        functools.partial(_decode_kernel, layer=layer, npages=npages, gp=gp, nbuf=nbuf, n_new=n_new),
        grid_spec=grid_spec,
        out_shape=jax.ShapeDtypeStruct((nseq * rq, KV_LORA), F32),
        compiler_params=_cparams("arbitrary"), name="decode_attn",
    )(pt, qlat, qpe, knl, knp, cache_ckv, cache_kpt)


def _mla_out_kernel(o_ref, x_ref, wuv_ref, wo_ref, gpost_ref, y_ref, ocat_sc):
    for hd in range(MLA_HEADS):
        oh = _dot(o_ref[:, hd * KV_LORA:(hd + 1) * KV_LORA].astype(BF16), wuv_ref[hd])
        ocat_sc[:, hd * V_HEAD:(hd + 1) * V_HEAD] = oh
    m = _dot(ocat_sc[...].astype(BF16), wo_ref[...])
    y_ref[...] = x_ref[...] + _rms(m, gpost_ref[...])


def _mla_out_small(o, x, wuv, wo, gpost):
    R, D = x.shape
    return pl.pallas_call(
        _mla_out_kernel, grid=(1,),
        in_specs=[_full(o.shape), _full(x.shape), _full(wuv.shape), _full(wo.shape), _full(gpost.shape)],
        out_specs=_full((R, D)), out_shape=jax.ShapeDtypeStruct((R, D), F32),
        scratch_shapes=[pltpu.VMEM((R, MLA_HEADS * V_HEAD), F32)],
        compiler_params=_cparams("arbitrary"), name="mla_out_small",
    )(o, x, wuv, wo, gpost)


def _mlp_kernel(*refs, fc, attn):
    if attn:
        a_ref, x_ref, wo_ref, gmix_ref, gpre_ref, wup_ref, wdn_ref, gpost_ref, o_ref = refs
        x = x_ref[...] + _rms(_dot(a_ref[...], wo_ref[...]), gmix_ref[...])
    else:
        x_ref, gpre_ref, wup_ref, wdn_ref, gpost_ref, o_ref = refs
        x = x_ref[...]
    h = _rms(x, gpre_ref[...]).astype(BF16)
    dff = wup_ref.shape[1]
    acc = None
    for c in range(dff // fc):
        u = _dot(h, wup_ref[:, c * fc:(c + 1) * fc])
        u = jnp.square(jnp.maximum(u, 0.0)).astype(BF16)
        d = _dot(u, wdn_ref[c * fc:(c + 1) * fc, :])
        acc = d if acc is None else acc + d
    o_ref[...] = x + _rms(acc, gpost_ref[...])


def _mlp(x, gpre, wup, wdn, gpost, layer, tm, fc=1024, attn=None):
    N, D = x.shape
    const = lambda a: pl.BlockSpec(a.shape, lambda i: (0, 0), pipeline_mode=pl.Buffered(1))
    stacked = lambda a: pl.BlockSpec((None,) + a.shape[1:], lambda i: (layer, 0, 0),
                                     pipeline_mode=pl.Buffered(1))
    rows = lambda: pl.BlockSpec((tm, D), lambda i: (i, 0))
    args = [x, gpre, wup, wdn, gpost]
    in_specs = [rows(), const(gpre), stacked(wup), stacked(wdn), const(gpost)]
    if attn is not None:
        a, wo, gmix = attn
        args = [a, x, wo, gmix] + args[1:]
        in_specs = [rows(), rows(), const(wo), const(gmix)] + in_specs[1:]
    return pl.pallas_call(
        functools.partial(_mlp_kernel, fc=fc, attn=attn is not None), grid=(N // tm,),
        in_specs=in_specs, out_specs=rows(),
        out_shape=jax.ShapeDtypeStruct((N, D), F32),
        compiler_params=_cparams("arbitrary"), name="mlp",
    )(*args)


def _gla_proj_kernel(x_ref, gpre_ref, win_ref, wg_ref, bg_ref, q_ref, k_ref, v_ref, r_ref, la_ref):
    hk = GLA_HEADS * GLA_DK
    hv = GLA_HEADS * GLA_DV
    h = _rms(x_ref[...], gpre_ref[...]).astype(BF16)
    a = _dot(h, win_ref[...])
    q_ref[...] = a[:, :hk] * GLA_SCALE
    k_ref[...] = a[:, hk:2 * hk]
    v_ref[...] = a[:, 2 * hk:2 * hk + hv]
    r_ref[...] = a[:, 2 * hk + hv:2 * hk + 2 * hv]
    gd = a[:, 2 * hk + 2 * hv:].astype(BF16)
    z = _dot(gd, wg_ref[...]) + bg_ref[...]
    la_ref[...] = (jnp.minimum(z, 0.0) - jnp.log(1.0 + jnp.exp(-jnp.abs(z)))) * (1.0 / GATE_TAU)


def _gla_proj(x, gpre, win, wg, bg, tr):
    N, D = x.shape
    hk = GLA_HEADS * GLA_DK
    hv = GLA_HEADS * GLA_DV
    row = lambda w: pl.BlockSpec((tr, w), lambda i: (i, 0))
    return pl.pallas_call(
        _gla_proj_kernel, grid=(N // tr,),
        in_specs=[row(D), _full(gpre.shape), _full(win.shape), _full(wg.shape), _full(bg.shape)],
        out_specs=[row(hk), row(hk), row(hv), row(hv), row(hk)],
        out_shape=[jax.ShapeDtypeStruct((N, w), F32) for w in (hk, hk, hv, hv, hk)],
        compiler_params=_cparams("arbitrary"), name="gla_proj",
    )(x, gpre, win, wg, bg)


def _gla_gate(o, r, gn):
    return _rms(o, gn) * (r / (1.0 + jnp.exp(-r)))


def _cumsum_rows(x, chunk):
    pos = lax.broadcasted_iota(jnp.int32, x.shape, 0) % chunk
    d = 1
    while d < chunk:
        x = x + jnp.where(pos >= d, pltpu.roll(x, d, axis=0), 0.0)
        d *= 2
    return x


def _gla_prompt_kernel(q_ref, k_ref, v_ref, r_ref, la_ref, x_ref, s0_ref, gn_ref, wo_ref, gpost_ref,
                       y_ref, sfin_ref, s_sc, ocat_sc, *, rows):
    c_sz = GLA_CHUNK
    nc = rows // c_sz

    @pl.when(pl.program_id(1) == 0)
    def _():
        s_sc[...] = s0_ref[...]

    b = _cumsum_rows(la_ref[0], c_sz)
    b_last = [b[(c + 1) * c_sz - 1:(c + 1) * c_sz, :] for c in range(nc)]
    k = k_ref[0]
    q_in = (q_ref[0] * jnp.exp(b)).astype(BF16)
    k_in = (k * jnp.exp(-b)).astype(BF16)
    k_dec = jnp.concatenate([k[c * c_sz:(c + 1) * c_sz] * jnp.exp(b_last[c] - b[c * c_sz:(c + 1) * c_sz])
                             for c in range(nc)], axis=0).astype(BF16)
    dec = [jnp.exp(bl) for bl in b_last]
    ri = lax.broadcasted_iota(jnp.int32, (rows, rows), 0)
    ci = lax.broadcasted_iota(jnp.int32, (rows, rows), 1)
    tril = jnp.where(ci <= ri, ci, -1) >= (ri // c_sz) * c_sz
    gn = gn_ref[...]
    for hd in range(GLA_HEADS):
        ks = slice(hd * GLA_DK, (hd + 1) * GLA_DK)
        vs = slice(hd * GLA_DV, (hd + 1) * GLA_DV)
        v = v_ref[0, :, vs].astype(BF16)
        a = jnp.where(tril, _dot_nt(q_in[:, ks], k_in[:, ks]), 0.0).astype(BF16)
        o_intra = _dot(a, v)
        kvs = [_dot_tn(k_dec[c * c_sz:(c + 1) * c_sz, ks], v[c * c_sz:(c + 1) * c_sz]) for c in range(nc)]
        s = s_sc[hd]
        o_inter = []
        for c in range(nc):
            o_inter.append(_dot(q_in[c * c_sz:(c + 1) * c_sz, ks], s.astype(BF16)))
            dcol = jnp.transpose(jnp.broadcast_to(dec[c][:, ks], (GLA_DK, GLA_DK)))
            s = jnp.concatenate([dcol] * (GLA_DV // GLA_DK), axis=1) * s + kvs[c]
        s_sc[hd] = s
        o = o_intra + jnp.concatenate(o_inter, axis=0)
        ocat_sc[:, vs] = _gla_gate(o, r_ref[0, :, vs], gn)
    m = _dot(ocat_sc[...].astype(BF16), wo_ref[...])
    y_ref[0] = x_ref[0] + _rms(m, gpost_ref[...])

    @pl.when(pl.program_id(1) == pl.num_programs(1) - 1)
    def _():
        sfin_ref[0] = s_sc[...]


def _gla_prompt(q, k, v, r, la, x, s0, gn, wo, gpost, rows):
    B, T, D = x.shape
    hk = GLA_HEADS * GLA_DK
    hv = GLA_HEADS * GLA_DV
    blk = lambda w: pl.BlockSpec((1, rows, w), lambda b, i: (b, i, 0))
    return pl.pallas_call(
        functools.partial(_gla_prompt_kernel, rows=rows), grid=(B, T // rows),
        in_specs=[blk(hk), blk(hk), blk(hv), blk(hv), blk(hk), blk(D), _full(s0.shape), _full(gn.shape),
                  _full(wo.shape), _full(gpost.shape)],
        out_specs=[blk(D), pl.BlockSpec((1,) + s0.shape, lambda b, i: (b, 0, 0, 0))],
        out_shape=[jax.ShapeDtypeStruct((B, T, D), F32), jax.ShapeDtypeStruct((B,) + s0.shape, F32)],
        scratch_shapes=[pltpu.VMEM(s0.shape, F32), pltpu.VMEM((rows, hv), F32)],
        compiler_params=_cparams("arbitrary", "arbitrary"), name="gla_prompt",
    )(q, k, v, r, la, x, s0, gn, wo, gpost)


GLA_SHORT_ROWS = 16


def _gla_tokens_kernel(q_ref, k_ref, la_ref, v_ref, s0_ref, o_ref, sfin_ref):
    nblk, rows, _ = q_ref.shape
    zk = jnp.zeros((LANES - rows, GLA_DK), BF16)
    zv = jnp.zeros((LANES - rows, GLA_DV), BF16)
    tril = (lax.broadcasted_iota(jnp.int32, (rows, LANES), 1) <= lax.broadcasted_iota(jnp.int32, (rows, LANES), 0))
    for n in range(nblk):
        b = _cumsum_rows(la_ref[n], rows)
        b_last = b[rows - 1:rows, :]
        k = k_ref[n]
        q_in = (q_ref[n] * jnp.exp(b)).astype(BF16)
        k_in = (k * jnp.exp(-b)).astype(BF16)
        k_dec = (k * jnp.exp(b_last - b)).astype(BF16)
        dec = jnp.exp(b_last)
        for hd in range(GLA_HEADS):
            ks = slice(hd * GLA_DK, (hd + 1) * GLA_DK)
            vs = slice(hd * GLA_DV, (hd + 1) * GLA_DV)
            v = jnp.concatenate([v_ref[n, :, vs].astype(BF16), zv], axis=0)
            a = jnp.where(tril, _dot_nt(q_in[:, ks], jnp.concatenate([k_in[:, ks], zk], axis=0)), 0.0)
            s = s0_ref[n, hd]
            o_ref[n, :, vs] = _dot(a.astype(BF16), v) + _dot(q_in[:, ks], s.astype(BF16))
            dcol = jnp.transpose(jnp.broadcast_to(dec[:, ks], (GLA_DK, GLA_DK)))
            sfin_ref[n, hd] = (jnp.concatenate([dcol] * (GLA_DV // GLA_DK), axis=1) * s
                               + _dot_tn(jnp.concatenate([k_dec[:, ks], zk], axis=0), v))


def _gla_tokens(q, k, la, v, s0, nblk):
    nseq, rows, hk = q.shape
    hv = v.shape[-1]
    col = pl.BlockSpec((nblk, rows, hk), lambda s: (s, 0, 0))
    val = pl.BlockSpec((nblk, rows, hv), lambda s: (s, 0, 0))
    st = pl.BlockSpec((nblk,) + s0.shape[1:], lambda s: (s, 0, 0, 0))
    return pl.pallas_call(
        _gla_tokens_kernel, grid=(nseq // nblk,),
        in_specs=[col, col, col, val, st], out_specs=[val, st],
        out_shape=[jax.ShapeDtypeStruct((nseq, rows, hv), F32), jax.ShapeDtypeStruct(s0.shape, F32)],
        compiler_params=_cparams("arbitrary"), name="gla_tokens",
    )(q, k, la, v, s0)


def _gla_out_kernel(o_ref, r_ref, x_ref, gn_ref, wo_ref, gpost_ref, y_ref, ocat_sc):
    gn = gn_ref[...]
    for hd in range(GLA_HEADS):
        vs = slice(hd * GLA_DV, (hd + 1) * GLA_DV)
        ocat_sc[:, vs] = _gla_gate(o_ref[:, vs], r_ref[:, vs], gn)
    m = _dot(ocat_sc[...].astype(BF16), wo_ref[...])
    y_ref[...] = x_ref[...] + _rms(m, gpost_ref[...])


def _gla_out_small(o, r, x, gn, wo, gpost):
    R, D = x.shape
    return pl.pallas_call(
        _gla_out_kernel, grid=(1,),
        in_specs=[_full(o.shape), _full(r.shape), _full(x.shape), _full(gn.shape), _full(wo.shape),
                  _full(gpost.shape)],
        out_specs=_full((R, D)), out_shape=jax.ShapeDtypeStruct((R, D), F32),
        scratch_shapes=[pltpu.VMEM(o.shape, F32)],
        compiler_params=_cparams("arbitrary"), name="gla_out_small",
    )(o, r, x, gn, wo, gpost)


def _rope_cos_sin(pos):
    half = QK_ROPE // 2
    inv = ROPE_BASE ** (-np.arange(half, dtype=np.float64) / half)
    ang = np.asarray(pos, np.float64)[:, None] * inv[None, :]
    return np.cos(ang), np.sin(ang)


def _rope_tables(pos):
    c, s = _rope_cos_sin(pos)
    reps = LANES // QK_ROPE
    return (jnp.asarray(np.tile(np.concatenate([c, c], axis=1), (1, reps)), F32),
            jnp.asarray(np.tile(np.concatenate([-s, s], axis=1), (1, reps)), F32))


def _rope_tables_head(pos):
    c, s = _rope_cos_sin(pos)
    z = lambda n: np.zeros((len(pos), n))
    tail = HEAD_W - QK_NOPE - QK_ROPE
    return (jnp.asarray(np.concatenate([z(QK_NOPE), c, c, z(tail)], axis=1), F32),
            jnp.asarray(np.concatenate([z(QK_NOPE), -s, s, z(tail)], axis=1), F32))


def _swap_halves(w):
    half = QK_ROPE // 2
    return jnp.concatenate([w[..., half:], w[..., :half]], axis=-1)


def kernel(x_prompt, x_sample, cache_ckv, cache_kpe, state_gla, page_table, meta_tokens, norm_pre_mix, norm_post_mix, norm_pre_mlp, norm_post_mlp, mla_w_in, mla_q_norm, mla_w_uq, mla_kv_norm, mla_w_uk, mla_w_uv, mla_w_o, gla_w_in, gla_w_gate, gla_b_gate, gla_norm, gla_w_o, mlp_w_up, mlp_w_down):
    B, T, D = x_prompt.shape
    DB, S, _ = x_sample.shape
    H = MLA_HEADS
    n_s = DB * S
    past_len = page_table.shape[1] * cache_ckv.shape[2]
    row = lambda a: a.reshape(1, -1)

    w_in = mla_w_in[0]
    o = Q_LORA + KV_LORA
    w_kpe = w_in[:, o:]
    zpad = jnp.zeros((D, LANES - QK_ROPE), F32)
    w_in_p = jnp.concatenate([w_in[:, :o], w_kpe, zpad, _swap_halves(w_kpe), zpad], axis=1).astype(BF16)
    w_uq = mla_w_uq[0].reshape(Q_LORA, H, QK_NOPE + QK_ROPE)
    w_nope = jnp.transpose(w_uq[:, :, :QK_NOPE], (1, 0, 2))
    w_uk = jnp.transpose(mla_w_uk[0], (1, 0, 2))
    w_qlat = _fold_qk(w_nope, w_uk)
    w_qpe = w_uq[:, :, QK_NOPE:]
    w_qp = w_qpe.reshape(Q_LORA, H * QK_ROPE).astype(BF16)
    w_qps = _swap_halves(w_qpe).reshape(Q_LORA, H * QK_ROPE).astype(BF16)
    proj_w = (row(norm_pre_mix[0]), w_in_p, row(mla_q_norm[0]), w_qlat, w_qp, w_qps, row(mla_kv_norm[0]))
    w_uv = jnp.transpose(mla_w_uv[0], (1, 0, 2)).astype(BF16)
    w_o = mla_w_o[0].astype(BF16)
    g_post0 = row(norm_post_mix[0])

    x_small = jnp.concatenate([x_sample.reshape(n_s, D), meta_tokens], axis=0)
    pos_small = np.concatenate([past_len + np.tile(np.arange(S), DB), np.arange(N_META)])
    cos_s, sin_s = _rope_tables(pos_small)
    qlat_s, qpe_s, ckv_s, kpe_s, klat_s, kpeb_s = _mla_proj_small(x_small, cos_s, sin_s, proj_w)

    npad = LANES
    kml = jnp.pad(klat_s[n_s:], ((0, npad - N_META), (0, 0)))
    kmp = jnp.pad(kpeb_s[n_s:], ((0, npad - N_META), (0, 0)))
    o_meta = _meta_attn(qlat_s[n_s:].reshape(N_META * H, KV_LORA), qpe_s[n_s:].reshape(N_META * H, QK_ROPE),
                        kml, kmp)
    knl = jnp.pad(klat_s[:n_s].reshape(DB, S, KV_LORA), ((0, 0), (0, npad - S), (0, 0)))
    knp = jnp.pad(kpeb_s[:n_s].reshape(DB, S, QK_ROPE), ((0, 0), (0, npad - S), (0, 0)))
    o_samp = _decode_attn(page_table, qlat_s[:n_s].reshape(n_s * H, KV_LORA),
                          qpe_s[:n_s].reshape(n_s * H, QK_ROPE), cache_ckv, jnp.swapaxes(cache_kpe, 2, 3),
                          0, knl, knp)
    o_small = jnp.concatenate([o_samp.reshape(n_s, H * KV_LORA), o_meta.reshape(N_META, H * KV_LORA)], axis=0)
    xs1 = _mla_out_small(o_small, x_small, w_uv, w_o, g_post0)

    zl = lambda n: jnp.zeros((D, n), F32)
    pad_r = HEAD_W - QK_NOPE - QK_ROPE
    w_in_m = jnp.concatenate([w_in[:, :o], zl(QK_NOPE), w_kpe, zl(pad_r),
                              zl(QK_NOPE), _swap_halves(w_kpe), zl(pad_r)], axis=1).astype(BF16)
    wq = jnp.pad(w_uq, ((0, 0), (0, 0), (0, pad_r))).reshape(Q_LORA, H * HEAD_W).astype(BF16)
    wqs = jnp.pad(_swap_halves(w_qpe), ((0, 0), (0, 0), (QK_NOPE, pad_r))).reshape(Q_LORA, H * HEAD_W).astype(BF16)
    wk = jnp.pad(mla_w_uk[0], ((0, 0), (0, 0), (0, HEAD_W - QK_NOPE))).reshape(KV_LORA, H * HEAD_W).astype(BF16)
    wv = mla_w_uv[0].reshape(KV_LORA, H * V_HEAD).astype(BF16)
    mha_w = (row(norm_pre_mix[0]), w_in_m, row(mla_q_norm[0]), wq, wqs, row(mla_kv_norm[0]), wk, wv)
    cosk, sink = _rope_tables_head(N_META + np.arange(T))
    q_p, k_p, v_p, ckv_p, kpe_p = _mha_proj(x_prompt, cosk, sink, mha_w, tr=512)
    cosm, sinm = _rope_tables_head(np.arange(N_META))
    _, k_m, v_m, _, _ = _mha_proj(meta_tokens[None], cosm, sinm, mha_w, tr=N_META)
    mpad = ((0, 0), (0, 0), (0, LANES - N_META), (0, 0))
    attn_p = _mha_flash(q_p, k_p, v_p, jnp.pad(k_m, mpad), jnp.pad(v_m, mpad), tq=512)

    wup_b, wdn_b = mlp_w_up.astype(BF16), mlp_w_down.astype(BF16)
    mlp0 = (row(norm_pre_mlp[0]), wup_b, wdn_b, row(norm_post_mlp[0]), 0)
    xp2 = _mlp(x_prompt.reshape(B * T, D), *mlp0, tm=512, attn=(attn_p.reshape(B * T, D), w_o, g_post0))
    xs2 = _mlp(xs1, *mlp0, tm=xs1.shape[0])

    hk = GLA_HEADS * GLA_DK
    hv = GLA_HEADS * GLA_DV
    gw = gla_w_in[0]
    gw_p = jnp.concatenate([gw, jnp.zeros((D, LANES - GATE_RANK), F32)], axis=1).astype(BF16)
    wg_p = jnp.concatenate([gla_w_gate[0], jnp.zeros((LANES - GATE_RANK, hk), F32)], axis=0).astype(BF16)
    gla_pw = (row(norm_pre_mix[1]), gw_p, wg_p, row(gla_b_gate[0]))
    gn = row(gla_norm[0])
    gwo = gla_w_o[0].astype(BF16)
    g_post1 = row(norm_post_mix[1])

    q_s, k_s, v_s, r_s, la_s = _gla_proj(xs2, *gla_pw, tr=xs2.shape[0])
    def seqs(a, n, l):
        return jnp.pad(a.reshape(n, l, a.shape[-1]), ((0, 0), (0, GLA_SHORT_ROWS - l), (0, 0)))

    o_gs, s_gs = _gla_tokens(*(seqs(a[:n_s], DB, S) for a in (q_s, k_s, la_s, v_s)), state_gla[0], nblk=4)
    zero_state = jnp.zeros((1, GLA_HEADS, GLA_DK, GLA_DV), F32)
    o_gm, s_gm = _gla_tokens(*(seqs(a[n_s:], 1, N_META) for a in (q_s, k_s, la_s, v_s)), zero_state, nblk=1)
    o_gsmall = jnp.concatenate([o_gs[:, :S].reshape(n_s, hv), o_gm[:, :N_META].reshape(N_META, hv)], axis=0)
    xs3 = _gla_out_small(o_gsmall, r_s, xs2, gn, gwo, g_post1)

    q_p, k_p, v_p, r_p, la_p = _gla_proj(xp2, *gla_pw, tr=512)
    r3 = lambda a: a.reshape(B, T, a.shape[-1])
    xp3, s_gp = _gla_prompt(r3(q_p), r3(k_p), r3(v_p), r3(r_p), r3(la_p), xp2.reshape(B, T, D), s_gm[0],
                            gn, gwo, g_post1, rows=256)

    mlp1 = (row(norm_pre_mlp[1]), wup_b, wdn_b, row(norm_post_mlp[1]), 1)
    y_prompt = _mlp(xp3.reshape(B * T, D), *mlp1, tm=512).reshape(B, T, D)
    xs4 = _mlp(xs3, *mlp1, tm=xs3.shape[0])
    y_sample = xs4[:n_s].reshape(DB, S, D)

    bmeta = lambda a: jnp.broadcast_to(a[None], (B,) + a.shape)
    new_ckv_prompt = jnp.concatenate([bmeta(ckv_s[n_s:]), ckv_p], axis=1)[None]
    new_kpe_prompt = jnp.concatenate([bmeta(kpe_s[n_s:]), kpe_p], axis=1)[None]
    new_ckv_sample = ckv_s[:n_s].reshape(1, DB, S, KV_LORA)
    new_kpe_sample = kpe_s[:n_s].reshape(1, DB, S, QK_ROPE)
    return (y_prompt, y_sample, new_ckv_prompt, new_kpe_prompt, new_ckv_sample, new_kpe_sample,
            s_gp[None], s_gs[None])
```

```python
import functools

import jax
import jax.numpy as jnp
import numpy as np
from jax import lax
from jax.experimental import pallas as pl
from jax.experimental.pallas import tpu as pltpu

F32 = jnp.float32
BF16 = jnp.bfloat16

N_META = 16
MLA_HEADS = 16
Q_LORA = 256
KV_LORA = 256
QK_NOPE = 64
QK_ROPE = 32
V_HEAD = 64
MLA_SCALE = (QK_NOPE + QK_ROPE) ** -0.5
ROPE_BASE = 10000.0
GLA_HEADS = 4
GLA_DK = 128
GLA_DV = 256
GLA_SCALE = GLA_DK ** -0.5
GATE_RANK = 16
GATE_TAU = 16.0
GLA_CHUNK = 64
EPS = 1e-6
LOG2E = 1.4426950408889634

LANES = 128
VMEM_LIMIT = 56 * 1024 * 1024


def _cparams(*sem):
    return pltpu.CompilerParams(dimension_semantics=sem, vmem_limit_bytes=VMEM_LIMIT)


def _rms(x, w):
    return x * lax.rsqrt(jnp.mean(x * x, axis=-1, keepdims=True) + EPS) * w


def _dot(a, b):
    return jnp.dot(a, b, preferred_element_type=F32)


def _dot_nt(a, b):
    return lax.dot_general(a, b, (((1,), (1,)), ((), ())), preferred_element_type=F32)


def _dot_tn(a, b):
    return lax.dot_general(a, b, (((0,), (0,)), ((), ())), preferred_element_type=F32)


def _full(shape):
    n = len(shape)
    return pl.BlockSpec(shape, lambda *_: (0,) * n)


def _fold_kernel(a_ref, b_ref, o_ref):
    o_ref[...] = lax.dot_general(a_ref[0], b_ref[0], (((1,), (1,)), ((), ())),
                                 precision=lax.Precision.HIGHEST,
                                 preferred_element_type=F32).astype(o_ref.dtype)


def _fold_qk(w_nope, w_uk):
    return pl.pallas_call(
        _fold_kernel,
        grid=(MLA_HEADS,),
        in_specs=[pl.BlockSpec((1, Q_LORA, QK_NOPE), lambda h: (h, 0, 0)),
                  pl.BlockSpec((1, KV_LORA, QK_NOPE), lambda h: (h, 0, 0))],
        out_specs=pl.BlockSpec((Q_LORA, KV_LORA), lambda h: (0, h)),
        out_shape=jax.ShapeDtypeStruct((Q_LORA, MLA_HEADS * KV_LORA), BF16),
        compiler_params=_cparams("arbitrary"),
        name="fold_qk",
    )(w_nope, w_uk)


def _mla_proj_kernel(x_ref, cos_ref, sin_ref, gpre_ref, win_ref, qn_ref, wql_ref, wqp_ref, wqps_ref,
                     kvn_ref, qlat_ref, qpe_ref, ckv_ref, kpe_ref, klat_ref, kpeb_ref):
    h = _rms(x_ref[...], gpre_ref[...]).astype(BF16)
    a = _dot(h, win_ref[...])
    cqn = _rms(a[:, :Q_LORA], qn_ref[...]).astype(BF16)
    ckv = _rms(a[:, Q_LORA:Q_LORA + KV_LORA], kvn_ref[...])
    cos = cos_ref[...]
    sin = sin_ref[...]
    o = Q_LORA + KV_LORA
    kpe = a[:, o:o + QK_ROPE] * cos[:, :QK_ROPE] + a[:, o + LANES:o + LANES + QK_ROPE] * sin[:, :QK_ROPE]
    qlat = _dot(cqn, wql_ref[...]) * MLA_SCALE
    qp = _dot(cqn, wqp_ref[...])
    qps = _dot(cqn, wqps_ref[...])
    nl = MLA_HEADS * QK_ROPE // LANES
    qpe = jnp.concatenate(
        [(qp[:, j * LANES:(j + 1) * LANES] * cos + qps[:, j * LANES:(j + 1) * LANES] * sin) * MLA_SCALE
         for j in range(nl)], axis=1)
    qlat_ref[...] = qlat.astype(BF16)
    qpe_ref[...] = qpe.astype(BF16)
    ckv_ref[...] = ckv
    kpe_ref[...] = kpe
    klat_ref[...] = ckv.astype(BF16)
    kpeb_ref[...] = kpe.astype(BF16)


def _mla_proj_small(x, cos, sin, w):
    R, D = x.shape
    H = MLA_HEADS
    in_specs = [_full((R, D)), _full((R, LANES)), _full((R, LANES))] + [_full(a.shape) for a in w]
    shapes = [((R, H * KV_LORA), BF16), ((R, H * QK_ROPE), BF16), ((R, KV_LORA), F32),
              ((R, QK_ROPE), F32), ((R, KV_LORA), BF16), ((R, QK_ROPE), BF16)]
    return pl.pallas_call(
        _mla_proj_kernel, grid=(1,), in_specs=in_specs,
        out_specs=[_full(s) for s, _ in shapes],
        out_shape=[jax.ShapeDtypeStruct(s, d) for s, d in shapes],
        compiler_params=_cparams("arbitrary"), name="mla_proj_small",
    )(x, cos, sin, *w)


HEAD_W = LANES


def _mha_proj_kernel(x_ref, cosk_ref, sink_ref, gpre_ref, win_ref, qn_ref, wq_ref, wqs_ref, kvn_ref,
                     wk_ref, wv_ref, q_ref, k_ref, v_ref, ckv_ref, kpe_ref):
    h = _rms(x_ref[0], gpre_ref[...]).astype(BF16)
    a = _dot(h, win_ref[...])
    cqn = _rms(a[:, :Q_LORA], qn_ref[...]).astype(BF16)
    ckv = _rms(a[:, Q_LORA:Q_LORA + KV_LORA], kvn_ref[...])
    cosk = cosk_ref[...]
    sink = sink_ref[...]
    o = Q_LORA + KV_LORA
    kpe = a[:, o:o + HEAD_W] * cosk + a[:, o + HEAD_W:o + 2 * HEAD_W] * sink
    ckv_ref[0] = ckv
    kpe_ref[0] = kpe[:, QK_NOPE:QK_NOPE + QK_ROPE]
    ckv_b = ckv.astype(BF16)
    k_all = _dot(ckv_b, wk_ref[...])
    v_all = _dot(ckv_b, wv_ref[...])
    q_raw = _dot(cqn, wq_ref[...])
    q_swp = _dot(cqn, wqs_ref[...])
    lane = lax.broadcasted_iota(jnp.int32, (1, HEAD_W), 1)
    cosq = cosk + jnp.where(lane < QK_NOPE, 1.0, 0.0)
    for hd in range(MLA_HEADS):
        sl = slice(hd * HEAD_W, (hd + 1) * HEAD_W)
        k_ref[0, hd] = (k_all[:, sl] + kpe).astype(BF16)
        q_ref[0, hd] = ((q_raw[:, sl] * cosq + q_swp[:, sl] * sink) * (MLA_SCALE * LOG2E)).astype(BF16)
    for p in range(MLA_HEADS * V_HEAD // LANES):
        v_ref[0, p] = v_all[:, p * LANES:(p + 1) * LANES].astype(BF16)


def _mha_proj(x, cosk, sink, w, tr):
    B, T, D = x.shape
    H = MLA_HEADS
    nv = H * V_HEAD // LANES
    in_specs = [pl.BlockSpec((1, tr, D), lambda b, i: (b, i, 0)),
                pl.BlockSpec((tr, HEAD_W), lambda b, i: (i, 0)),
                pl.BlockSpec((tr, HEAD_W), lambda b, i: (i, 0))] + [_full(a.shape) for a in w]
    out_shape = [jax.ShapeDtypeStruct((B, H, T, HEAD_W), BF16),
                 jax.ShapeDtypeStruct((B, H, T, HEAD_W), BF16),
                 jax.ShapeDtypeStruct((B, nv, T, LANES), BF16),
                 jax.ShapeDtypeStruct((B, T, KV_LORA), F32),
                 jax.ShapeDtypeStruct((B, T, QK_ROPE), F32)]
    out_specs = [pl.BlockSpec((1, H, tr, HEAD_W), lambda b, i: (b, 0, i, 0)),
                 pl.BlockSpec((1, H, tr, HEAD_W), lambda b, i: (b, 0, i, 0)),
                 pl.BlockSpec((1, nv, tr, LANES), lambda b, i: (b, 0, i, 0)),
                 pl.BlockSpec((1, tr, KV_LORA), lambda b, i: (b, i, 0)),
                 pl.BlockSpec((1, tr, QK_ROPE), lambda b, i: (b, i, 0))]
    return pl.pallas_call(
        _mha_proj_kernel, grid=(B, T // tr), in_specs=in_specs, out_specs=out_specs, out_shape=out_shape,
        compiler_params=_cparams("arbitrary", "arbitrary"), name="mha_proj",
    )(x, cosk, sink, *w)


def _mha_flash_kernel(q_ref, k_ref, v_ref, km_ref, vm_ref, o_ref, m_sc, l_sc, acc_sc, *, tq):
    i = pl.program_id(2)
    nh = q_ref.shape[1]

    def scores(hh, rows, kb, mask):
        s = _dot_nt(q_ref[0, hh, rows], kb)
        return s if mask is None else jnp.where(mask, s, -jnp.inf)

    def apply(hh, rows, s, vb, first):
        parts = [s[:, c * LANES:(c + 1) * LANES] for c in range(s.shape[1] // LANES)]
        mrow = jnp.max(functools.reduce(jnp.maximum, parts), axis=-1, keepdims=True)
        if first:
            m_new = jnp.broadcast_to(mrow, (s.shape[0], LANES))
        else:
            m_old = m_sc[hh, rows]
            m_new = jnp.maximum(m_old, mrow)
        ps = [jnp.exp2(part - m_new) for part in parts]
        psum = functools.reduce(jnp.add, ps)
        p = ps[0] if len(ps) == 1 else jnp.concatenate(ps, axis=1)
        pv = _dot(p.astype(BF16), vb)
        if first:
            l_sc[hh, rows] = psum
            acc_sc[hh, rows] = pv
        else:
            alpha = jnp.exp2(m_old - m_new)
            l_sc[hh, rows] = alpha * l_sc[hh, rows] + psum
            acc_sc[hh, rows] = alpha * acc_sc[hh, rows] + pv
        m_sc[hh, rows] = m_new

    nm = km_ref.shape[2]
    hpg = nh // v_ref.shape[1]

    def step(pieces, first):
        ss = [[scores(hh, rows, kb(hh), mask) for rows, kb, _, mask in pieces] for hh in range(nh)]
        for hh in range(nh):
            for (rows, _, vb, _), s in zip(pieces, ss[hh]):
                apply(hh, rows, s, vb(hh // hpg), first)

    off = pl.multiple_of(i * tq, tq)
    hq = tq // 2

    def diag_piece(first_row, nkeys):
        rows = slice(first_row, first_row + hq)
        col = lax.broadcasted_iota(jnp.int32, (1, nkeys + nm), 1)
        need = jnp.where(col < first_row, -1,
                         jnp.where(col < nkeys, col - first_row, jnp.where(col - nkeys < N_META, -1, tq)))
        mask = lax.broadcasted_iota(jnp.int32, (hq, nkeys + nm), 0) >= need
        kb = lambda hh: jnp.concatenate([k_ref[0, hh, pl.ds(off, nkeys), :], km_ref[0, hh]], axis=0)
        vb = lambda g: jnp.concatenate([v_ref[0, g, pl.ds(off, nkeys), :], vm_ref[0, g]], axis=0)
        return rows, kb, vb, mask

    step([diag_piece(0, hq), diag_piece(hq, tq)], True)

    def body(j, carry):
        offj = pl.multiple_of(j * tq, tq)
        step([(slice(None), lambda hh: k_ref[0, hh, pl.ds(offj, tq), :],
               lambda g: v_ref[0, g, pl.ds(offj, tq), :], None)], False)
        return carry

    lax.fori_loop(0, i, body, 0)

    lane = lax.broadcasted_iota(jnp.int32, (tq, LANES), 1)
    for g in range(nh // hpg):
        out = None
        for r in range(hpg):
            hh = g * hpg + r
            o = acc_sc[hh] / jnp.sum(l_sc[hh], axis=-1, keepdims=True)
            out = o if out is None else jnp.where(lane >= r * V_HEAD, o, out)
        o_ref[0, :, g * LANES:(g + 1) * LANES] = out.astype(o_ref.dtype)


MHA_GROUPS_PER_STEP = 4


def _mha_flash(q, k, v, km, vm, tq):
    B, H, T, W = q.shape
    nv = v.shape[1]
    ng = MHA_GROUPS_PER_STEP
    nh = H // nv * ng
    return pl.pallas_call(
        functools.partial(_mha_flash_kernel, tq=tq), grid=(B, nv // ng, T // tq),
        in_specs=[pl.BlockSpec((1, nh, tq, W), lambda b, p, i: (b, p, i, 0)),
                  pl.BlockSpec((1, nh, T, W), lambda b, p, i: (b, p, 0, 0)),
                  pl.BlockSpec((1, ng, T, LANES), lambda b, p, i: (b, p, 0, 0)),
                  pl.BlockSpec((1, nh) + km.shape[2:], lambda b, p, i: (0, p, 0, 0)),
                  pl.BlockSpec((1, ng) + vm.shape[2:], lambda b, p, i: (0, p, 0, 0))],
        out_specs=pl.BlockSpec((1, tq, ng * LANES), lambda b, p, i: (b, i, p)),
        out_shape=jax.ShapeDtypeStruct((B, T, nv * LANES), BF16),
        scratch_shapes=[pltpu.VMEM((nh, tq, LANES), F32), pltpu.VMEM((nh, tq, LANES), F32),
                        pltpu.VMEM((nh, tq, LANES), F32)],
        compiler_params=_cparams("arbitrary", "arbitrary", "arbitrary"), name="mha_flash",
    )(q, k, v, km, vm)


def _meta_attn_kernel(qlat_ref, qpe_ref, kl_ref, kp_ref, o_ref):
    kl = kl_ref[...]
    s = _dot_nt(qlat_ref[...], kl) + _dot_nt(qpe_ref[...], kp_ref[...])
    r, n = s.shape
    tok = lax.broadcasted_iota(jnp.int32, (r // MLA_HEADS, MLA_HEADS, n), 0).reshape(r, n)
    col = lax.broadcasted_iota(jnp.int32, (r, n), 1)
    s = jnp.where(col <= tok, s, -jnp.inf)
    p = jnp.exp(s - jnp.max(s, axis=-1, keepdims=True))
    l = jnp.sum(p, axis=-1, keepdims=True)
    o_ref[...] = _dot(p.astype(BF16), kl) / l


def _meta_attn(qlat, qpe, kl, kp):
    r = qlat.shape[0]
    return pl.pallas_call(
        _meta_attn_kernel, grid=(1,),
        in_specs=[_full(qlat.shape), _full(qpe.shape), _full(kl.shape), _full(kp.shape)],
        out_specs=_full((r, KV_LORA)), out_shape=jax.ShapeDtypeStruct((r, KV_LORA), F32),
        compiler_params=_cparams("arbitrary"), name="meta_attn",
    )(qlat, qpe, kl, kp)


def _decode_kernel(pt_ref, qlat_ref, qpe_ref, knl_ref, knp_ref, ckv_hbm, kpt_hbm, o_ref,
                   ckv_buf, kpt_buf, sem, m_sc, l_sc, acc_sc, *, layer, npages, gp, nbuf, n_new):
    s = pl.program_id(0)
    nseq = pl.num_programs(0)
    ngroups = npages // gp
    page = ckv_buf.shape[1] // gp
    ql = qlat_ref[...]
    qp = qpe_ref[...]
    rq = ql.shape[0]

    def group_copies(seq, g):
        slot = g % nbuf
        cps = []
        for p in range(gp):
            pid = pt_ref[seq * npages + g * gp + p]
            cps.append(pltpu.make_async_copy(ckv_hbm.at[layer, pid],
                                             ckv_buf.at[slot, pl.ds(p * page, page), :], sem.at[slot]))
            cps.append(pltpu.make_async_copy(kpt_hbm.at[layer, pid],
                                             kpt_buf.at[slot, :, pl.ds(p * page, page)], sem.at[slot]))
        return cps

    @pl.when(s == 0)
    def _():
        for g in range(nbuf - 1):
            for cp in group_copies(0, g):
                cp.start()

    m_sc[...] = jnp.full(m_sc.shape, -jnp.inf, F32)
    l_sc[...] = jnp.zeros(l_sc.shape, F32)
    acc_sc[...] = jnp.zeros(acc_sc.shape, F32)

    def accumulate(kl, s):
        parts = [s[:, c * LANES:(c + 1) * LANES] for c in range(s.shape[1] // LANES)]
        m_old = m_sc[...]
        m_new = jnp.maximum(m_old, jnp.max(functools.reduce(jnp.maximum, parts), axis=-1, keepdims=True))
        alpha = jnp.exp(m_old - m_new)
        ps = [jnp.exp(part - m_new) for part in parts]
        p = ps[0] if len(ps) == 1 else jnp.concatenate(ps, axis=1)
        l_sc[...] = alpha * l_sc[...] + functools.reduce(jnp.add, ps)
        acc_sc[...] = (jnp.concatenate([alpha] * (KV_LORA // LANES), axis=1) * acc_sc[...]
                       + _dot(p.astype(BF16), kl))
        m_sc[...] = m_new

    def scores(g):
        for cp in group_copies(s, g):
            cp.wait()
        slot = g % nbuf
        kl = ckv_buf[slot].astype(BF16)
        return kl, _dot_nt(ql, kl) + _dot(qp, kpt_buf[slot].astype(BF16))

    cur = scores(0)
    for g in range(ngroups):
        nxt = g + nbuf - 1
        if nxt < ngroups:
            for cp in group_copies(s, nxt):
                cp.start()
        else:
            @pl.when(s + 1 < nseq)
            def _():
                for cp in group_copies(s + 1, nxt - ngroups):
                    cp.start()
        ahead = scores(g + 1) if g + 1 < ngroups else None
        accumulate(*cur)
        cur = ahead

    kn = knl_ref[0]
    sn = _dot_nt(ql, kn) + _dot_nt(qp, knp_ref[0])
    n = sn.shape[1]
    tok = lax.broadcasted_iota(jnp.int32, (n_new, rq // n_new, n), 0).reshape(rq, n)
    col = lax.broadcasted_iota(jnp.int32, (rq, n), 1)
    accumulate(kn, jnp.where(col <= tok, sn, -jnp.inf))
    o_ref[...] = acc_sc[...] / jnp.sum(l_sc[...], axis=-1, keepdims=True)


DECODE_GROUP_PAGES = 16
DECODE_RING_SLOTS = 4


def _decode_attn(page_table, qlat, qpe, cache_ckv, cache_kpt, layer, knl, knp):
    nseq, npages = page_table.shape
    page = cache_ckv.shape[2]
    n_new = qlat.shape[0] // (nseq * MLA_HEADS)
    rq = n_new * MLA_HEADS
    gp, nbuf = DECODE_GROUP_PAGES, DECODE_RING_SLOTS
    assert npages % gp == 0 and (npages // gp) % nbuf == 0
    pt = page_table.reshape(-1)
    in_specs = [pl.BlockSpec((rq, KV_LORA), lambda s, pt_ref: (s, 0)),
                pl.BlockSpec((rq, QK_ROPE), lambda s, pt_ref: (s, 0)),
                pl.BlockSpec((1,) + knl.shape[1:], lambda s, pt_ref: (s, 0, 0)),
                pl.BlockSpec((1,) + knp.shape[1:], lambda s, pt_ref: (s, 0, 0)),
                pl.BlockSpec(memory_space=pl.ANY), pl.BlockSpec(memory_space=pl.ANY)]
    grid_spec = pltpu.PrefetchScalarGridSpec(
        num_scalar_prefetch=1, grid=(nseq,), in_specs=in_specs,
        out_specs=pl.BlockSpec((rq, KV_LORA), lambda s, pt_ref: (s, 0)),
        scratch_shapes=[pltpu.VMEM((nbuf, gp * page, KV_LORA), F32), pltpu.VMEM((nbuf, QK_ROPE, gp * page), F32),
                        pltpu.SemaphoreType.DMA((nbuf,)),
                        pltpu.VMEM((rq, LANES), F32), pltpu.VMEM((rq, LANES), F32),
                        pltpu.VMEM((rq, KV_LORA), F32)])
    return pl.pallas_call(
        functools.partial(_decode_kernel, layer=layer, npages=npages, gp=gp, nbuf=nbuf, n_new=n_new),
        grid_spec=grid_spec,
        out_shape=jax.ShapeDtypeStruct((nseq * rq, KV_LORA), F32),
        compiler_params=_cparams("arbitrary"), name="decode_attn",
    )(pt, qlat, qpe, knl, knp, cache_ckv, cache_kpt)


def _mla_out_kernel(o_ref, x_ref, wuv_ref, wo_ref, gpost_ref, y_ref, ocat_sc):
    for hd in range(MLA_HEADS):
        oh = _dot(o_ref[:, hd * KV_LORA:(hd + 1) * KV_LORA].astype(BF16), wuv_ref[hd])
        ocat_sc[:, hd * V_HEAD:(hd + 1) * V_HEAD] = oh
    m = _dot(ocat_sc[...].astype(BF16), wo_ref[...])
    y_ref[...] = x_ref[...] + _rms(m, gpost_ref[...])


def _mla_out_small(o, x, wuv, wo, gpost):
    R, D = x.shape
    return pl.pallas_call(
        _mla_out_kernel, grid=(1,),
        in_specs=[_full(o.shape), _full(x.shape), _full(wuv.shape), _full(wo.shape), _full(gpost.shape)],
        out_specs=_full((R, D)), out_shape=jax.ShapeDtypeStruct((R, D), F32),
        scratch_shapes=[pltpu.VMEM((R, MLA_HEADS * V_HEAD), F32)],
        compiler_params=_cparams("arbitrary"), name="mla_out_small",
    )(o, x, wuv, wo, gpost)


def _mlp_kernel(*refs, fc, attn):
    if attn:
        a_ref, x_ref, wo_ref, gmix_ref, gpre_ref, wup_ref, wdn_ref, gpost_ref, o_ref = refs
        x = x_ref[...] + _rms(_dot(a_ref[...], wo_ref[...]), gmix_ref[...])
    else:
        x_ref, gpre_ref, wup_ref, wdn_ref, gpost_ref, o_ref = refs
        x = x_ref[...]
    h = _rms(x, gpre_ref[...]).astype(BF16)
    dff = wup_ref.shape[1]
    acc = None
    for c in range(dff // fc):
        u = _dot(h, wup_ref[:, c * fc:(c + 1) * fc])
        u = jnp.square(jnp.maximum(u, 0.0)).astype(BF16)
        d = _dot(u, wdn_ref[c * fc:(c + 1) * fc, :])
        acc = d if acc is None else acc + d
    o_ref[...] = x + _rms(acc, gpost_ref[...])


def _mlp(x, gpre, wup, wdn, gpost, layer, tm, fc=1024, attn=None):
    N, D = x.shape
    const = lambda a: pl.BlockSpec(a.shape, lambda i: (0, 0), pipeline_mode=pl.Buffered(1))
    stacked = lambda a: pl.BlockSpec((None,) + a.shape[1:], lambda i: (layer, 0, 0),
                                     pipeline_mode=pl.Buffered(1))
    rows = lambda: pl.BlockSpec((tm, D), lambda i: (i, 0))
    args = [x, gpre, wup, wdn, gpost]
    in_specs = [rows(), const(gpre), stacked(wup), stacked(wdn), const(gpost)]
    if attn is not None:
        a, wo, gmix = attn
        args = [a, x, wo, gmix] + args[1:]
        in_specs = [rows(), rows(), const(wo), const(gmix)] + in_specs[1:]
    return pl.pallas_call(
        functools.partial(_mlp_kernel, fc=fc, attn=attn is not None), grid=(N // tm,),
        in_specs=in_specs, out_specs=rows(),
        out_shape=jax.ShapeDtypeStruct((N, D), F32),
        compiler_params=_cparams("arbitrary"), name="mlp",
    )(*args)


def _gla_proj_kernel(x_ref, gpre_ref, win_ref, wg_ref, bg_ref, q_ref, k_ref, v_ref, r_ref, la_ref):
    hk = GLA_HEADS * GLA_DK
    hv = GLA_HEADS * GLA_DV
    h = _rms(x_ref[...], gpre_ref[...]).astype(BF16)
    a = _dot(h, win_ref[...])
    q_ref[...] = a[:, :hk] * GLA_SCALE
    k_ref[...] = a[:, hk:2 * hk]
    v_ref[...] = a[:, 2 * hk:2 * hk + hv].astype(v_ref.dtype)
    r_ref[...] = a[:, 2 * hk + hv:2 * hk + 2 * hv]
    gd = a[:, 2 * hk + 2 * hv:].astype(BF16)
    z = _dot(gd, wg_ref[...]) + bg_ref[...]
    la_ref[...] = (jnp.minimum(z, 0.0) - jnp.log(1.0 + jnp.exp(-jnp.abs(z)))) * (1.0 / GATE_TAU)


def _gla_proj(x, gpre, win, wg, bg, tr):
    N, D = x.shape
    hk = GLA_HEADS * GLA_DK
    hv = GLA_HEADS * GLA_DV
    row = lambda w: pl.BlockSpec((tr, w), lambda i: (i, 0))
    return pl.pallas_call(
        _gla_proj_kernel, grid=(N // tr,),
        in_specs=[row(D), _full(gpre.shape), _full(win.shape), _full(wg.shape), _full(bg.shape)],
        out_specs=[row(hk), row(hk), row(hv), row(hv), row(hk)],
        out_shape=[jax.ShapeDtypeStruct((N, w), d)
                   for w, d in ((hk, F32), (hk, F32), (hv, BF16), (hv, F32), (hk, F32))],
        compiler_params=_cparams("arbitrary"), name="gla_proj",
    )(x, gpre, win, wg, bg)


def _gla_gate(o, r, gn):
    return _rms(o, gn) * (r / (1.0 + jnp.exp(-r)))


def _cumsum_rows(x, chunk):
    pos = lax.broadcasted_iota(jnp.int32, x.shape, 0) % chunk
    d = 1
    while d < chunk:
        x = x + jnp.where(pos >= d, pltpu.roll(x, d, axis=0), 0.0)
        d *= 2
    return x


def _gla_prompt_kernel(q_ref, k_ref, v_ref, r_ref, la_ref, x_ref, s0_ref, gn_ref, wo_ref, gpost_ref,
                       y_ref, sfin_ref, s_sc, ocat_sc, *, rows):
    c_sz = GLA_CHUNK
    nc = rows // c_sz

    @pl.when(pl.program_id(1) == 0)
    def _():
        s_sc[...] = s0_ref[...]

    b = _cumsum_rows(la_ref[0], c_sz)
    b_last = [b[(c + 1) * c_sz - 1:(c + 1) * c_sz, :] for c in range(nc)]
    k = k_ref[0]
    q_in = (q_ref[0] * jnp.exp(b)).astype(BF16)
    k_in = (k * jnp.exp(-b)).astype(BF16)
    k_dec = jnp.concatenate([k[c * c_sz:(c + 1) * c_sz] * jnp.exp(b_last[c] - b[c * c_sz:(c + 1) * c_sz])
                             for c in range(nc)], axis=0).astype(BF16)
    dec = [jnp.exp(bl) for bl in b_last]
    ri = lax.broadcasted_iota(jnp.int32, (rows, rows), 0)
    ci = lax.broadcasted_iota(jnp.int32, (rows, rows), 1)
    tril = jnp.where(ci <= ri, ci, -1) >= (ri // c_sz) * c_sz
    gn = gn_ref[...]
    for hd in range(GLA_HEADS):
        ks = slice(hd * GLA_DK, (hd + 1) * GLA_DK)
        vs = slice(hd * GLA_DV, (hd + 1) * GLA_DV)
        v = v_ref[0, :, vs].astype(BF16)
        a = jnp.where(tril, _dot_nt(q_in[:, ks], k_in[:, ks]), 0.0).astype(BF16)
        o_intra = _dot(a, v)
        kvs = [_dot_tn(k_dec[c * c_sz:(c + 1) * c_sz, ks], v[c * c_sz:(c + 1) * c_sz]) for c in range(nc)]
        s = s_sc[hd]
        o_inter = []
        for c in range(nc):
            o_inter.append(_dot(q_in[c * c_sz:(c + 1) * c_sz, ks], s.astype(BF16)))
            dcol = jnp.transpose(jnp.broadcast_to(dec[c][:, ks], (GLA_DK, GLA_DK)))
            s = jnp.concatenate([dcol] * (GLA_DV // GLA_DK), axis=1) * s + kvs[c]
        s_sc[hd] = s
        o = o_intra + jnp.concatenate(o_inter, axis=0)
        ocat_sc[:, vs] = _gla_gate(o, r_ref[0, :, vs], gn)
    m = _dot(ocat_sc[...].astype(BF16), wo_ref[...])
    y_ref[0] = x_ref[0] + _rms(m, gpost_ref[...])

    @pl.when(pl.program_id(1) == pl.num_programs(1) - 1)
    def _():
        sfin_ref[0] = s_sc[...]


def _gla_prompt(q, k, v, r, la, x, s0, gn, wo, gpost, rows):
    B, T, D = x.shape
    hk = GLA_HEADS * GLA_DK
    hv = GLA_HEADS * GLA_DV
    blk = lambda w: pl.BlockSpec((1, rows, w), lambda b, i: (b, i, 0))
    return pl.pallas_call(
        functools.partial(_gla_prompt_kernel, rows=rows), grid=(B, T // rows),
        in_specs=[blk(hk), blk(hk), blk(hv), blk(hv), blk(hk), blk(D), _full(s0.shape), _full(gn.shape),
                  _full(wo.shape), _full(gpost.shape)],
        out_specs=[blk(D), pl.BlockSpec((1,) + s0.shape, lambda b, i: (b, 0, 0, 0))],
        out_shape=[jax.ShapeDtypeStruct((B, T, D), F32), jax.ShapeDtypeStruct((B,) + s0.shape, F32)],
        scratch_shapes=[pltpu.VMEM(s0.shape, F32), pltpu.VMEM((rows, hv), F32)],
        compiler_params=_cparams("arbitrary", "arbitrary"), name="gla_prompt",
    )(q, k, v, r, la, x, s0, gn, wo, gpost)


GLA_SHORT_ROWS = 16


def _gla_tokens_kernel(q_ref, k_ref, la_ref, v_ref, s0_ref, o_ref, sfin_ref):
    nblk, rows, _ = q_ref.shape
    zk = jnp.zeros((LANES - rows, GLA_DK), BF16)
    zv = jnp.zeros((LANES - rows, GLA_DV), BF16)
    tril = (lax.broadcasted_iota(jnp.int32, (rows, LANES), 1) <= lax.broadcasted_iota(jnp.int32, (rows, LANES), 0))
    for n in range(nblk):
        b = _cumsum_rows(la_ref[n], rows)
        b_last = b[rows - 1:rows, :]
        k = k_ref[n]
        q_in = (q_ref[n] * jnp.exp(b)).astype(BF16)
        k_in = (k * jnp.exp(-b)).astype(BF16)
        k_dec = (k * jnp.exp(b_last - b)).astype(BF16)
        dec = jnp.exp(b_last)
        for hd in range(GLA_HEADS):
            ks = slice(hd * GLA_DK, (hd + 1) * GLA_DK)
            vs = slice(hd * GLA_DV, (hd + 1) * GLA_DV)
            v = jnp.concatenate([v_ref[n, :, vs].astype(BF16), zv], axis=0)
            a = jnp.where(tril, _dot_nt(q_in[:, ks], jnp.concatenate([k_in[:, ks], zk], axis=0)), 0.0)
            s = s0_ref[n, hd]
            o_ref[n, :, vs] = _dot(a.astype(BF16), v) + _dot(q_in[:, ks], s.astype(BF16))
            dcol = jnp.transpose(jnp.broadcast_to(dec[:, ks], (GLA_DK, GLA_DK)))
            sfin_ref[n, hd] = (jnp.concatenate([dcol] * (GLA_DV // GLA_DK), axis=1) * s
                               + _dot_tn(jnp.concatenate([k_dec[:, ks], zk], axis=0), v))


def _gla_tokens(q, k, la, v, s0, nblk):
    nseq, rows, hk = q.shape
    hv = v.shape[-1]
    col = pl.BlockSpec((nblk, rows, hk), lambda s: (s, 0, 0))
    val = pl.BlockSpec((nblk, rows, hv), lambda s: (s, 0, 0))
    st = pl.BlockSpec((nblk,) + s0.shape[1:], lambda s: (s, 0, 0, 0))
    return pl.pallas_call(
        _gla_tokens_kernel, grid=(nseq // nblk,),
        in_specs=[col, col, col, val, st], out_specs=[val, st],
        out_shape=[jax.ShapeDtypeStruct((nseq, rows, hv), F32), jax.ShapeDtypeStruct(s0.shape, F32)],
        compiler_params=_cparams("arbitrary"), name="gla_tokens",
    )(q, k, la, v, s0)


def _gla_out_kernel(o_ref, r_ref, x_ref, gn_ref, wo_ref, gpost_ref, y_ref, ocat_sc):
    gn = gn_ref[...]
    for hd in range(GLA_HEADS):
        vs = slice(hd * GLA_DV, (hd + 1) * GLA_DV)
        ocat_sc[:, vs] = _gla_gate(o_ref[:, vs], r_ref[:, vs], gn)
    m = _dot(ocat_sc[...].astype(BF16), wo_ref[...])
    y_ref[...] = x_ref[...] + _rms(m, gpost_ref[...])


def _gla_out_small(o, r, x, gn, wo, gpost):
    R, D = x.shape
    return pl.pallas_call(
        _gla_out_kernel, grid=(1,),
        in_specs=[_full(o.shape), _full(r.shape), _full(x.shape), _full(gn.shape), _full(wo.shape),
                  _full(gpost.shape)],
        out_specs=_full((R, D)), out_shape=jax.ShapeDtypeStruct((R, D), F32),
        scratch_shapes=[pltpu.VMEM(o.shape, F32)],
        compiler_params=_cparams("arbitrary"), name="gla_out_small",
    )(o, r, x, gn, wo, gpost)


def _rope_cos_sin(pos):
    half = QK_ROPE // 2
    inv = ROPE_BASE ** (-np.arange(half, dtype=np.float64) / half)
    ang = np.asarray(pos, np.float64)[:, None] * inv[None, :]
    return np.cos(ang), np.sin(ang)


def _rope_tables(pos):
    c, s = _rope_cos_sin(pos)
    reps = LANES // QK_ROPE
    return (jnp.asarray(np.tile(np.concatenate([c, c], axis=1), (1, reps)), F32),
            jnp.asarray(np.tile(np.concatenate([-s, s], axis=1), (1, reps)), F32))


def _rope_tables_head(pos):
    c, s = _rope_cos_sin(pos)
    z = lambda n: np.zeros((len(pos), n))
    tail = HEAD_W - QK_NOPE - QK_ROPE
    return (jnp.asarray(np.concatenate([z(QK_NOPE), c, c, z(tail)], axis=1), F32),
            jnp.asarray(np.concatenate([z(QK_NOPE), -s, s, z(tail)], axis=1), F32))


def _swap_halves(w):
    half = QK_ROPE // 2
    return jnp.concatenate([w[..., half:], w[..., :half]], axis=-1)


def kernel(x_prompt, x_sample, cache_ckv, cache_kpe, state_gla, page_table, meta_tokens, norm_pre_mix, norm_post_mix, norm_pre_mlp, norm_post_mlp, mla_w_in, mla_q_norm, mla_w_uq, mla_kv_norm, mla_w_uk, mla_w_uv, mla_w_o, gla_w_in, gla_w_gate, gla_b_gate, gla_norm, gla_w_o, mlp_w_up, mlp_w_down):
    B, T, D = x_prompt.shape
    DB, S, _ = x_sample.shape
    H = MLA_HEADS
    n_s = DB * S
    past_len = page_table.shape[1] * cache_ckv.shape[2]
    row = lambda a: a.reshape(1, -1)

    w_in = mla_w_in[0]
    o = Q_LORA + KV_LORA
    w_kpe = w_in[:, o:]
    zpad = jnp.zeros((D, LANES - QK_ROPE), F32)
    w_in_p = jnp.concatenate([w_in[:, :o], w_kpe, zpad, _swap_halves(w_kpe), zpad], axis=1).astype(BF16)
    w_uq = mla_w_uq[0].reshape(Q_LORA, H, QK_NOPE + QK_ROPE)
    w_nope = jnp.transpose(w_uq[:, :, :QK_NOPE], (1, 0, 2))
    w_uk = jnp.transpose(mla_w_uk[0], (1, 0, 2))
    w_qlat = _fold_qk(w_nope, w_uk)
    w_qpe = w_uq[:, :, QK_NOPE:]
    w_qp = w_qpe.reshape(Q_LORA, H * QK_ROPE).astype(BF16)
    w_qps = _swap_halves(w_qpe).reshape(Q_LORA, H * QK_ROPE).astype(BF16)
    proj_w = (row(norm_pre_mix[0]), w_in_p, row(mla_q_norm[0]), w_qlat, w_qp, w_qps, row(mla_kv_norm[0]))
    w_uv = jnp.transpose(mla_w_uv[0], (1, 0, 2)).astype(BF16)
    w_o = mla_w_o[0].astype(BF16)
    g_post0 = row(norm_post_mix[0])

    x_small = jnp.concatenate([x_sample.reshape(n_s, D), meta_tokens], axis=0)
    pos_small = np.concatenate([past_len + np.tile(np.arange(S), DB), np.arange(N_META)])
    cos_s, sin_s = _rope_tables(pos_small)
    qlat_s, qpe_s, ckv_s, kpe_s, klat_s, kpeb_s = _mla_proj_small(x_small, cos_s, sin_s, proj_w)

    npad = LANES
    kml = jnp.pad(klat_s[n_s:], ((0, npad - N_META), (0, 0)))
    kmp = jnp.pad(kpeb_s[n_s:], ((0, npad - N_META), (0, 0)))
    o_meta = _meta_attn(qlat_s[n_s:].reshape(N_META * H, KV_LORA), qpe_s[n_s:].reshape(N_META * H, QK_ROPE),
                        kml, kmp)
    knl = jnp.pad(klat_s[:n_s].reshape(DB, S, KV_LORA), ((0, 0), (0, npad - S), (0, 0)))
    knp = jnp.pad(kpeb_s[:n_s].reshape(DB, S, QK_ROPE), ((0, 0), (0, npad - S), (0, 0)))
    o_samp = _decode_attn(page_table, qlat_s[:n_s].reshape(n_s * H, KV_LORA),
                          qpe_s[:n_s].reshape(n_s * H, QK_ROPE), cache_ckv, jnp.swapaxes(cache_kpe, 2, 3),
                          0, knl, knp)
    o_small = jnp.concatenate([o_samp.reshape(n_s, H * KV_LORA), o_meta.reshape(N_META, H * KV_LORA)], axis=0)
    xs1 = _mla_out_small(o_small, x_small, w_uv, w_o, g_post0)

    zl = lambda n: jnp.zeros((D, n), F32)
    pad_r = HEAD_W - QK_NOPE - QK_ROPE
    w_in_m = jnp.concatenate([w_in[:, :o], zl(QK_NOPE), w_kpe, zl(pad_r),
                              zl(QK_NOPE), _swap_halves(w_kpe), zl(pad_r)], axis=1).astype(BF16)
    wq = jnp.pad(w_uq, ((0, 0), (0, 0), (0, pad_r))).reshape(Q_LORA, H * HEAD_W).astype(BF16)
    wqs = jnp.pad(_swap_halves(w_qpe), ((0, 0), (0, 0), (QK_NOPE, pad_r))).reshape(Q_LORA, H * HEAD_W).astype(BF16)
    wk = jnp.pad(mla_w_uk[0], ((0, 0), (0, 0), (0, HEAD_W - QK_NOPE))).reshape(KV_LORA, H * HEAD_W).astype(BF16)
    wv = mla_w_uv[0].reshape(KV_LORA, H * V_HEAD).astype(BF16)
    mha_w = (row(norm_pre_mix[0]), w_in_m, row(mla_q_norm[0]), wq, wqs, row(mla_kv_norm[0]), wk, wv)
    cosk, sink = _rope_tables_head(N_META + np.arange(T))
    q_p, k_p, v_p, ckv_p, kpe_p = _mha_proj(x_prompt, cosk, sink, mha_w, tr=512)
    cosm, sinm = _rope_tables_head(np.arange(N_META))
    _, k_m, v_m, _, _ = _mha_proj(meta_tokens[None], cosm, sinm, mha_w, tr=N_META)
    mpad = ((0, 0), (0, 0), (0, LANES - N_META), (0, 0))
    attn_p = _mha_flash(q_p, k_p, v_p, jnp.pad(k_m, mpad), jnp.pad(v_m, mpad), tq=512)

    wup_b, wdn_b = mlp_w_up.astype(BF16), mlp_w_down.astype(BF16)
    mlp0 = (row(norm_pre_mlp[0]), wup_b, wdn_b, row(norm_post_mlp[0]), 0)
    xp2 = _mlp(x_prompt.reshape(B * T, D), *mlp0, tm=512, attn=(attn_p.reshape(B * T, D), w_o, g_post0))
    xs2 = _mlp(xs1, *mlp0, tm=xs1.shape[0])

    hk = GLA_HEADS * GLA_DK
    hv = GLA_HEADS * GLA_DV
    gw = gla_w_in[0]
    gw_p = jnp.concatenate([gw, jnp.zeros((D, LANES - GATE_RANK), F32)], axis=1).astype(BF16)
    wg_p = jnp.concatenate([gla_w_gate[0], jnp.zeros((LANES - GATE_RANK, hk), F32)], axis=0).astype(BF16)
    gla_pw = (row(norm_pre_mix[1]), gw_p, wg_p, row(gla_b_gate[0]))
    gn = row(gla_norm[0])
    gwo = gla_w_o[0].astype(BF16)
    g_post1 = row(norm_post_mix[1])

    q_s, k_s, v_s, r_s, la_s = _gla_proj(xs2, *gla_pw, tr=xs2.shape[0])
    def seqs(a, n, l):
        return jnp.pad(a.reshape(n, l, a.shape[-1]), ((0, 0), (0, GLA_SHORT_ROWS - l), (0, 0)))

    o_gs, s_gs = _gla_tokens(*(seqs(a[:n_s], DB, S) for a in (q_s, k_s, la_s, v_s)), state_gla[0], nblk=4)
    zero_state = jnp.zeros((1, GLA_HEADS, GLA_DK, GLA_DV), F32)
    o_gm, s_gm = _gla_tokens(*(seqs(a[n_s:], 1, N_META) for a in (q_s, k_s, la_s, v_s)), zero_state, nblk=1)
    o_gsmall = jnp.concatenate([o_gs[:, :S].reshape(n_s, hv), o_gm[:, :N_META].reshape(N_META, hv)], axis=0)
    xs3 = _gla_out_small(o_gsmall, r_s, xs2, gn, gwo, g_post1)

    q_p, k_p, v_p, r_p, la_p = _gla_proj(xp2, *gla_pw, tr=512)
    r3 = lambda a: a.reshape(B, T, a.shape[-1])
    xp3, s_gp = _gla_prompt(r3(q_p), r3(k_p), r3(v_p), r3(r_p), r3(la_p), xp2.reshape(B, T, D), s_gm[0],
                            gn, gwo, g_post1, rows=256)

    mlp1 = (row(norm_pre_mlp[1]), wup_b, wdn_b, row(norm_post_mlp[1]), 1)
    y_prompt = _mlp(xp3.reshape(B * T, D), *mlp1, tm=512).reshape(B, T, D)
    xs4 = _mlp(xs3, *mlp1, tm=xs3.shape[0])
    y_sample = xs4[:n_s].reshape(DB, S, D)

    bmeta = lambda a: jnp.broadcast_to(a[None], (B,) + a.shape)
    new_ckv_prompt = jnp.concatenate([bmeta(ckv_s[n_s:]), ckv_p], axis=1)[None]
    new_kpe_prompt = jnp.concatenate([bmeta(kpe_s[n_s:]), kpe_p], axis=1)[None]
    new_ckv_sample = ckv_s[:n_s].reshape(1, DB, S, KV_LORA)
    new_kpe_sample = kpe_s[:n_s].reshape(1, DB, S, QK_ROPE)
    return (y_prompt, y_sample, new_ckv_prompt, new_kpe_prompt, new_ckv_sample, new_kpe_sample,
            s_gp[None], s_gs[None])
```

```python
import functools

import jax
import jax.numpy as jnp
import numpy as np
from jax import lax
from jax.experimental import pallas as pl
from jax.experimental.pallas import tpu as pltpu

F32 = jnp.float32
BF16 = jnp.bfloat16

N_META = 16
MLA_HEADS = 16
Q_LORA = 256
KV_LORA = 256
QK_NOPE = 64
QK_ROPE = 32
V_HEAD = 64
MLA_SCALE = (QK_NOPE + QK_ROPE) ** -0.5
ROPE_BASE = 10000.0
GLA_HEADS = 4
GLA_DK = 128
GLA_DV = 256
GLA_SCALE = GLA_DK ** -0.5
GATE_RANK = 16
GATE_TAU = 16.0
GLA_CHUNK = 64
EPS = 1e-6
LOG2E = 1.4426950408889634

LANES = 128
VMEM_LIMIT = 56 * 1024 * 1024


def _cparams(*sem):
    return pltpu.CompilerParams(dimension_semantics=sem, vmem_limit_bytes=VMEM_LIMIT)


def _rms(x, w):
    return x * lax.rsqrt(jnp.mean(x * x, axis=-1, keepdims=True) + EPS) * w


def _dot(a, b):
    return jnp.dot(a, b, preferred_element_type=F32)


def _dot_nt(a, b):
    return lax.dot_general(a, b, (((1,), (1,)), ((), ())), preferred_element_type=F32)


def _dot_tn(a, b):
    return lax.dot_general(a, b, (((0,), (0,)), ((), ())), preferred_element_type=F32)


def _full(shape):
    n = len(shape)
    return pl.BlockSpec(shape, lambda *_: (0,) * n)


def _mla_proj_kernel(x_ref, cos_ref, sin_ref, gpre_ref, win_ref, qn_ref, wqn_ref, wukt_ref, wqp_ref, wqps_ref,
                     kvn_ref, qlat_ref, qpe_ref, ckv_ref, kpe_ref, klat_ref, kpeb_ref):
    h = _rms(x_ref[...], gpre_ref[...]).astype(BF16)
    a = _dot(h, win_ref[...])
    cqn = _rms(a[:, :Q_LORA], qn_ref[...]).astype(BF16)
    ckv = _rms(a[:, Q_LORA:Q_LORA + KV_LORA], kvn_ref[...])
    cos = cos_ref[...]
    sin = sin_ref[...]
    o = Q_LORA + KV_LORA
    kpe = a[:, o:o + QK_ROPE] * cos[:, :QK_ROPE] + a[:, o + LANES:o + LANES + QK_ROPE] * sin[:, :QK_ROPE]
    q_nope = _dot(cqn, wqn_ref[...])
    qp = _dot(cqn, wqp_ref[...])
    qps = _dot(cqn, wqps_ref[...])
    nl = MLA_HEADS * QK_ROPE // LANES
    qpe = jnp.concatenate(
        [(qp[:, j * LANES:(j + 1) * LANES] * cos + qps[:, j * LANES:(j + 1) * LANES] * sin) * MLA_SCALE
         for j in range(nl)], axis=1)
    for hd in range(MLA_HEADS):
        ql = _dot(q_nope[:, hd * LANES:(hd + 1) * LANES].astype(BF16), wukt_ref[hd])
        qlat_ref[:, hd * KV_LORA:(hd + 1) * KV_LORA] = (ql * MLA_SCALE).astype(BF16)
    qpe_ref[...] = qpe.astype(BF16)
    ckv_ref[...] = ckv
    kpe_ref[...] = kpe
    klat_ref[...] = ckv.astype(BF16)
    kpeb_ref[...] = kpe.astype(BF16)


def _mla_proj_small(x, cos, sin, w):
    R, D = x.shape
    H = MLA_HEADS
    in_specs = [_full((R, D)), _full((R, LANES)), _full((R, LANES))] + [_full(a.shape) for a in w]
    shapes = [((R, H * KV_LORA), BF16), ((R, H * QK_ROPE), BF16), ((R, KV_LORA), F32),
              ((R, QK_ROPE), F32), ((R, KV_LORA), BF16), ((R, QK_ROPE), BF16)]
    return pl.pallas_call(
        _mla_proj_kernel, grid=(1,), in_specs=in_specs,
        out_specs=[_full(s) for s, _ in shapes],
        out_shape=[jax.ShapeDtypeStruct(s, d) for s, d in shapes],
        compiler_params=_cparams("arbitrary"), name="mla_proj_small",
    )(x, cos, sin, *w)


HEAD_W = LANES


def _val_lane0(hd):
    return (hd % 2) * V_HEAD


def _sum_lane(hd):
    return (1 - hd % 2) * V_HEAD


def _mha_proj_kernel(x_ref, cosk_ref, sink_ref, gpre_ref, win_ref, qn_ref, wq_ref, wqs_ref, kvn_ref,
                     wk_ref, wv_ref, q_ref, k_ref, v_ref, ckv_ref, kpe_ref):
    h = _rms(x_ref[0], gpre_ref[...]).astype(BF16)
    a = _dot(h, win_ref[...])
    cqn = _rms(a[:, :Q_LORA], qn_ref[...]).astype(BF16)
    ckv = _rms(a[:, Q_LORA:Q_LORA + KV_LORA], kvn_ref[...])
    cosk = cosk_ref[...]
    sink = sink_ref[...]
    o = Q_LORA + KV_LORA
    kpe = a[:, o:o + HEAD_W] * cosk + a[:, o + HEAD_W:o + 2 * HEAD_W] * sink
    ckv_ref[0] = ckv
    kpe_ref[0] = kpe[:, QK_NOPE:QK_NOPE + QK_ROPE]
    ckv_b = ckv.astype(BF16)
    k_all = _dot(ckv_b, wk_ref[...])
    v_all = _dot(ckv_b, wv_ref[...])
    q_raw = _dot(cqn, wq_ref[...])
    q_swp = _dot(cqn, wqs_ref[...])
    lane = lax.broadcasted_iota(jnp.int32, (1, HEAD_W), 1)
    cosq = cosk + jnp.where(lane < QK_NOPE, 1.0, 0.0)
    for hd in range(MLA_HEADS):
        sl = slice(hd * HEAD_W, (hd + 1) * HEAD_W)
        k_ref[0, hd] = (k_all[:, sl] + kpe).astype(BF16)
        q_ref[0, hd] = ((q_raw[:, sl] * cosq + q_swp[:, sl] * sink) * (MLA_SCALE * LOG2E)).astype(BF16)
        v_ref[0, hd] = (v_all[:, sl] + jnp.where(lane == _sum_lane(hd), 1.0, 0.0)).astype(BF16)


def _mha_proj(x, cosk, sink, w, tr):
    B, T, D = x.shape
    H = MLA_HEADS
    in_specs = [pl.BlockSpec((1, tr, D), lambda b, i: (b, i, 0)),
                pl.BlockSpec((tr, HEAD_W), lambda b, i: (i, 0)),
                pl.BlockSpec((tr, HEAD_W), lambda b, i: (i, 0))] + [_full(a.shape) for a in w]
    out_shape = [jax.ShapeDtypeStruct((B, H, T, HEAD_W), BF16),
                 jax.ShapeDtypeStruct((B, H, T, HEAD_W), BF16),
                 jax.ShapeDtypeStruct((B, H, T, HEAD_W), BF16),
                 jax.ShapeDtypeStruct((B, T, KV_LORA), F32),
                 jax.ShapeDtypeStruct((B, T, QK_ROPE), F32)]
    out_specs = [pl.BlockSpec((1, H, tr, HEAD_W), lambda b, i: (b, 0, i, 0)),
                 pl.BlockSpec((1, H, tr, HEAD_W), lambda b, i: (b, 0, i, 0)),
                 pl.BlockSpec((1, H, tr, HEAD_W), lambda b, i: (b, 0, i, 0)),
                 pl.BlockSpec((1, tr, KV_LORA), lambda b, i: (b, i, 0)),
                 pl.BlockSpec((1, tr, QK_ROPE), lambda b, i: (b, i, 0))]
    return pl.pallas_call(
        _mha_proj_kernel, grid=(B, T // tr), in_specs=in_specs, out_specs=out_specs, out_shape=out_shape,
        compiler_params=_cparams("arbitrary", "arbitrary"), name="mha_proj",
    )(x, cosk, sink, *w)


def _mha_flash_kernel(q_ref, k_ref, v_ref, km_ref, vm_ref, o_ref, m_sc, acc_sc, *, tq):
    i = pl.program_id(2)
    nh = q_ref.shape[1]

    def scores(hh, rows, kb, mask):
        s = _dot_nt(q_ref[0, hh, rows], kb)
        return s if mask is None else jnp.where(mask, s, -jnp.inf)

    def apply(hh, rows, s, vb, first):
        parts = [s[:, c * LANES:(c + 1) * LANES] for c in range(s.shape[1] // LANES)]
        mrow = jnp.max(functools.reduce(jnp.maximum, parts), axis=-1, keepdims=True)
        if first:
            m_new = jnp.broadcast_to(mrow, (s.shape[0], LANES))
        else:
            m_old = m_sc[hh, rows]
            m_new = jnp.maximum(m_old, mrow)
        p = jnp.concatenate([jnp.exp2(part - m_new) for part in parts], axis=1)
        pv = _dot(p.astype(BF16), vb)
        if first:
            acc_sc[hh, rows] = pv
        else:
            acc_sc[hh, rows] = jnp.exp2(m_old - m_new) * acc_sc[hh, rows] + pv
        m_sc[hh, rows] = m_new

    nm = km_ref.shape[2]

    def step(pieces, first):
        ss = [[scores(hh, rows, kb(hh), mask) for rows, kb, _, mask in pieces] for hh in range(nh)]
        for hh in range(nh):
            for (rows, _, vb, _), s in zip(pieces, ss[hh]):
                apply(hh, rows, s, vb(hh), first)

    off = pl.multiple_of(i * tq, tq)
    hq = tq // 2

    def diag_piece(first_row, nkeys):
        rows = slice(first_row, first_row + hq)
        col = lax.broadcasted_iota(jnp.int32, (1, nkeys + nm), 1)
        need = jnp.where(col < first_row, -1,
                         jnp.where(col < nkeys, col - first_row, jnp.where(col - nkeys < N_META, -1, tq)))
        mask = lax.broadcasted_iota(jnp.int32, (hq, nkeys + nm), 0) >= need
        kb = lambda hh: jnp.concatenate([k_ref[0, hh, pl.ds(off, nkeys), :], km_ref[0, hh]], axis=0)
        vb = lambda g: jnp.concatenate([v_ref[0, g, pl.ds(off, nkeys), :], vm_ref[0, g]], axis=0)
        return rows, kb, vb, mask

    step([diag_piece(0, hq), diag_piece(hq, tq)], True)

    def body(j, carry):
        offj = pl.multiple_of(j * tq, tq)
        step([(slice(None), lambda hh: k_ref[0, hh, pl.ds(offj, tq), :],
               lambda g: v_ref[0, g, pl.ds(offj, tq), :], None)], False)
        return carry

    lax.fori_loop(0, i, body, 0)

    lane = lax.broadcasted_iota(jnp.int32, (tq, LANES), 1)
    for g in range(nh // 2):
        pair = []
        for hh in (2 * g, 2 * g + 1):
            acc = acc_sc[hh]
            pair.append(acc / acc[:, _sum_lane(hh):_sum_lane(hh) + 1])
        o_ref[0, :, g * LANES:(g + 1) * LANES] = jnp.where(lane < V_HEAD, pair[0], pair[1]).astype(o_ref.dtype)


MHA_HEADS_PER_STEP = 8


def _mha_flash(q, k, v, km, vm, tq):
    B, H, T, W = q.shape
    nh = MHA_HEADS_PER_STEP
    heads = lambda n: pl.BlockSpec((1, nh, n, W), lambda b, p, i: (b, p, 0, 0))
    return pl.pallas_call(
        functools.partial(_mha_flash_kernel, tq=tq), grid=(B, H // nh, T // tq),
        in_specs=[pl.BlockSpec((1, nh, tq, W), lambda b, p, i: (b, p, i, 0)), heads(T), heads(T),
                  pl.BlockSpec((1, nh) + km.shape[2:], lambda b, p, i: (0, p, 0, 0)),
                  pl.BlockSpec((1, nh) + vm.shape[2:], lambda b, p, i: (0, p, 0, 0))],
        out_specs=pl.BlockSpec((1, tq, nh * V_HEAD), lambda b, p, i: (b, i, p)),
        out_shape=jax.ShapeDtypeStruct((B, T, H * V_HEAD), BF16),
        scratch_shapes=[pltpu.VMEM((nh, tq, LANES), F32), pltpu.VMEM((nh, tq, LANES), F32)],
        compiler_params=_cparams("arbitrary", "arbitrary", "arbitrary"), name="mha_flash",
    )(q, k, v, km, vm)


def _meta_attn_kernel(qlat_ref, qpe_ref, kl_ref, kp_ref, o_ref):
    kl = kl_ref[...]
    s = _dot_nt(qlat_ref[...], kl) + _dot_nt(qpe_ref[...], kp_ref[...])
    r, n = s.shape
    tok = lax.broadcasted_iota(jnp.int32, (r // MLA_HEADS, MLA_HEADS, n), 0).reshape(r, n)
    col = lax.broadcasted_iota(jnp.int32, (r, n), 1)
    s = jnp.where(col <= tok, s, -jnp.inf)
    p = jnp.exp(s - jnp.max(s, axis=-1, keepdims=True))
    l = jnp.sum(p, axis=-1, keepdims=True)
    o_ref[...] = _dot(p.astype(BF16), kl) / l


def _meta_attn(qlat, qpe, kl, kp):
    r = qlat.shape[0]
    return pl.pallas_call(
        _meta_attn_kernel, grid=(1,),
        in_specs=[_full(qlat.shape), _full(qpe.shape), _full(kl.shape), _full(kp.shape)],
        out_specs=_full((r, KV_LORA)), out_shape=jax.ShapeDtypeStruct((r, KV_LORA), F32),
        compiler_params=_cparams("arbitrary"), name="meta_attn",
    )(qlat, qpe, kl, kp)


def _decode_kernel(pt_ref, qlat_ref, qpe_ref, knl_ref, knp_ref, ckv_hbm, kpt_hbm, o_ref,
                   ckv_buf, kpt_buf, sem, m_sc, l_sc, acc_sc, *, layer, npages, gp, nbuf, n_new):
    s = pl.program_id(0)
    nseq = pl.num_programs(0)
    ngroups = npages // gp
    page = ckv_buf.shape[1] // gp
    ql = qlat_ref[...]
    qp = qpe_ref[...]
    rq = ql.shape[0]

    def group_copies(seq, g):
        slot = g % nbuf
        cps = []
        for p in range(gp):
            pid = pt_ref[seq * npages + g * gp + p]
            cps.append(pltpu.make_async_copy(ckv_hbm.at[layer, pid],
                                             ckv_buf.at[slot, pl.ds(p * page, page), :], sem.at[slot]))
            cps.append(pltpu.make_async_copy(kpt_hbm.at[layer, pid],
                                             kpt_buf.at[slot, :, pl.ds(p * page, page)], sem.at[slot]))
        return cps

    @pl.when(s == 0)
    def _():
        for g in range(nbuf - 1):
            for cp in group_copies(0, g):
                cp.start()

    m_sc[...] = jnp.full(m_sc.shape, -jnp.inf, F32)
    l_sc[...] = jnp.zeros(l_sc.shape, F32)
    acc_sc[...] = jnp.zeros(acc_sc.shape, F32)

    def accumulate(kl, s):
        parts = [s[:, c * LANES:(c + 1) * LANES] for c in range(s.shape[1] // LANES)]
        m_old = m_sc[...]
        m_new = jnp.maximum(m_old, jnp.max(functools.reduce(jnp.maximum, parts), axis=-1, keepdims=True))
        alpha = jnp.exp(m_old - m_new)
        ps = [jnp.exp(part - m_new) for part in parts]
        p = ps[0] if len(ps) == 1 else jnp.concatenate(ps, axis=1)
        l_sc[...] = alpha * l_sc[...] + functools.reduce(jnp.add, ps)
        acc_sc[...] = (jnp.concatenate([alpha] * (KV_LORA // LANES), axis=1) * acc_sc[...]
                       + _dot(p.astype(BF16), kl))
        m_sc[...] = m_new

    def scores(g):
        for cp in group_copies(s, g):
            cp.wait()
        slot = g % nbuf
        kl = ckv_buf[slot].astype(BF16)
        return kl, _dot_nt(ql, kl) + _dot(qp, kpt_buf[slot].astype(BF16))

    cur = scores(0)
    for g in range(ngroups):
        nxt = g + nbuf - 1
        if nxt < ngroups:
            for cp in group_copies(s, nxt):
                cp.start()
        else:
            @pl.when(s + 1 < nseq)
            def _():
                for cp in group_copies(s + 1, nxt - ngroups):
                    cp.start()
        ahead = scores(g + 1) if g + 1 < ngroups else None
        accumulate(*cur)
        cur = ahead

    kn = knl_ref[0]
    sn = _dot_nt(ql, kn) + _dot_nt(qp, knp_ref[0])
    n = sn.shape[1]
    tok = lax.broadcasted_iota(jnp.int32, (n_new, rq // n_new, n), 0).reshape(rq, n)
    col = lax.broadcasted_iota(jnp.int32, (rq, n), 1)
    accumulate(kn, jnp.where(col <= tok, sn, -jnp.inf))
    o_ref[...] = acc_sc[...] / jnp.sum(l_sc[...], axis=-1, keepdims=True)


DECODE_GROUP_PAGES = 16
DECODE_RING_SLOTS = 4


def _decode_attn(page_table, qlat, qpe, cache_ckv, cache_kpt, layer, knl, knp):
    nseq, npages = page_table.shape
    page = cache_ckv.shape[2]
    n_new = qlat.shape[0] // (nseq * MLA_HEADS)
    rq = n_new * MLA_HEADS
    gp, nbuf = DECODE_GROUP_PAGES, DECODE_RING_SLOTS
    assert npages % gp == 0 and (npages // gp) % nbuf == 0
    pt = page_table.reshape(-1)
    in_specs = [pl.BlockSpec((rq, KV_LORA), lambda s, pt_ref: (s, 0)),
                pl.BlockSpec((rq, QK_ROPE), lambda s, pt_ref: (s, 0)),
                pl.BlockSpec((1,) + knl.shape[1:], lambda s, pt_ref: (s, 0, 0)),
                pl.BlockSpec((1,) + knp.shape[1:], lambda s, pt_ref: (s, 0, 0)),
                pl.BlockSpec(memory_space=pl.ANY), pl.BlockSpec(memory_space=pl.ANY)]
    grid_spec = pltpu.PrefetchScalarGridSpec(
        num_scalar_prefetch=1, grid=(nseq,), in_specs=in_specs,
        out_specs=pl.BlockSpec((rq, KV_LORA), lambda s, pt_ref: (s, 0)),
        scratch_shapes=[pltpu.VMEM((nbuf, gp * page, KV_LORA), F32), pltpu.VMEM((nbuf, QK_ROPE, gp * page), F32),
                        pltpu.SemaphoreType.DMA((nbuf,)),
                        pltpu.VMEM((rq, LANES), F32), pltpu.VMEM((rq, LANES), F32),
                        pltpu.VMEM((rq, KV_LORA), F32)])
    return pl.pallas_call(
        functools.partial(_decode_kernel, layer=layer, npages=npages, gp=gp, nbuf=nbuf, n_new=n_new),
        grid_spec=grid_spec,
        out_shape=jax.ShapeDtypeStruct((nseq * rq, KV_LORA), F32),
        compiler_params=_cparams("arbitrary"), name="decode_attn",
    )(pt, qlat, qpe, knl, knp, cache_ckv, cache_kpt)


def _mla_out_kernel(o_ref, x_ref, wuv_ref, wo_ref, gpost_ref, y_ref, ocat_sc):
    for hd in range(MLA_HEADS):
        oh = _dot(o_ref[:, hd * KV_LORA:(hd + 1) * KV_LORA].astype(BF16), wuv_ref[hd])
        ocat_sc[:, hd * V_HEAD:(hd + 1) * V_HEAD] = oh
    m = _dot(ocat_sc[...].astype(BF16), wo_ref[...])
    y_ref[...] = x_ref[...] + _rms(m, gpost_ref[...])


def _mla_out_small(o, x, wuv, wo, gpost):
    R, D = x.shape
    return pl.pallas_call(
        _mla_out_kernel, grid=(1,),
        in_specs=[_full(o.shape), _full(x.shape), _full(wuv.shape), _full(wo.shape), _full(gpost.shape)],
        out_specs=_full((R, D)), out_shape=jax.ShapeDtypeStruct((R, D), F32),
        scratch_shapes=[pltpu.VMEM((R, MLA_HEADS * V_HEAD), F32)],
        compiler_params=_cparams("arbitrary"), name="mla_out_small",
    )(o, x, wuv, wo, gpost)


def _mlp_kernel(*refs, fc, attn):
    if attn:
        a_ref, x_ref, wo_ref, gmix_ref, gpre_ref, wup_ref, wdn_ref, gpost_ref, o_ref = refs
        x = x_ref[...] + _rms(_dot(a_ref[...], wo_ref[...]), gmix_ref[...])
    else:
        x_ref, gpre_ref, wup_ref, wdn_ref, gpost_ref, o_ref = refs
        x = x_ref[...]
    h = _rms(x, gpre_ref[...]).astype(BF16)
    dff = wup_ref.shape[1]
    acc = None
    for c in range(dff // fc):
        u = _dot(h, wup_ref[:, c * fc:(c + 1) * fc])
        u = jnp.square(jnp.maximum(u, 0.0)).astype(BF16)
        d = _dot(u, wdn_ref[c * fc:(c + 1) * fc, :])
        acc = d if acc is None else acc + d
    o_ref[...] = x + _rms(acc, gpost_ref[...])


def _mlp(x, gpre, wup, wdn, gpost, layer, tm, fc=1024, attn=None):
    N, D = x.shape
    const = lambda a: pl.BlockSpec(a.shape, lambda i: (0, 0), pipeline_mode=pl.Buffered(1))
    stacked = lambda a: pl.BlockSpec((None,) + a.shape[1:], lambda i: (layer, 0, 0),
                                     pipeline_mode=pl.Buffered(1))
    rows = lambda: pl.BlockSpec((tm, D), lambda i: (i, 0))
    args = [x, gpre, wup, wdn, gpost]
    in_specs = [rows(), const(gpre), stacked(wup), stacked(wdn), const(gpost)]
    if attn is not None:
        a, wo, gmix = attn
        args = [a, x, wo, gmix] + args[1:]
        in_specs = [rows(), rows(), const(wo), const(gmix)] + in_specs[1:]
    return pl.pallas_call(
        functools.partial(_mlp_kernel, fc=fc, attn=attn is not None), grid=(N // tm,),
        in_specs=in_specs, out_specs=rows(),
        out_shape=jax.ShapeDtypeStruct((N, D), F32),
        compiler_params=_cparams("arbitrary"), name="mlp",
    )(*args)


def _gla_proj_kernel(x_ref, gpre_ref, win_ref, wg_ref, bg_ref, q_ref, k_ref, v_ref, r_ref, la_ref):
    hk = GLA_HEADS * GLA_DK
    hv = GLA_HEADS * GLA_DV
    h = _rms(x_ref[...], gpre_ref[...]).astype(BF16)
    a = _dot(h, win_ref[...])
    q_ref[...] = a[:, :hk] * GLA_SCALE
    k_ref[...] = a[:, hk:2 * hk]
    v_ref[...] = a[:, 2 * hk:2 * hk + hv].astype(v_ref.dtype)
    r_ref[...] = a[:, 2 * hk + hv:2 * hk + 2 * hv]
    gd = a[:, 2 * hk + 2 * hv:].astype(BF16)
    z = _dot(gd, wg_ref[...]) + bg_ref[...]
    la_ref[...] = (jnp.minimum(z, 0.0) - jnp.log(1.0 + jnp.exp(-jnp.abs(z)))) * (1.0 / GATE_TAU)


def _gla_proj(x, gpre, win, wg, bg, tr):
    N, D = x.shape
    hk = GLA_HEADS * GLA_DK
    hv = GLA_HEADS * GLA_DV
    row = lambda w: pl.BlockSpec((tr, w), lambda i: (i, 0))
    return pl.pallas_call(
        _gla_proj_kernel, grid=(N // tr,),
        in_specs=[row(D), _full(gpre.shape), _full(win.shape), _full(wg.shape), _full(bg.shape)],
        out_specs=[row(hk), row(hk), row(hv), row(hv), row(hk)],
        out_shape=[jax.ShapeDtypeStruct((N, w), d)
                   for w, d in ((hk, F32), (hk, F32), (hv, BF16), (hv, F32), (hk, F32))],
        compiler_params=_cparams("arbitrary"), name="gla_proj",
    )(x, gpre, win, wg, bg)


def _gla_gate(o, r, gn):
    return _rms(o, gn) * (r / (1.0 + jnp.exp(-r)))


def _cumsum_rows(x, chunk):
    pos = lax.broadcasted_iota(jnp.int32, x.shape, 0) % chunk
    d = 1
    while d < chunk:
        x = x + jnp.where(pos >= d, pltpu.roll(x, d, axis=0), 0.0)
        d *= 2
    return x


def _gla_prompt_kernel(q_ref, k_ref, v_ref, r_ref, la_ref, x_ref, s0_ref, gn_ref, wo_ref, gpost_ref,
                       y_ref, sfin_ref, s_sc, ocat_sc, *, rows):
    c_sz = GLA_CHUNK
    nc = rows // c_sz

    @pl.when(pl.program_id(1) == 0)
    def _():
        s_sc[...] = s0_ref[...]

    b = _cumsum_rows(la_ref[0], c_sz)
    b_last = [b[(c + 1) * c_sz - 1:(c + 1) * c_sz, :] for c in range(nc)]
    k = k_ref[0]
    q_in = (q_ref[0] * jnp.exp(b)).astype(BF16)
    k_in = (k * jnp.exp(-b)).astype(BF16)
    k_dec = jnp.concatenate([k[c * c_sz:(c + 1) * c_sz] * jnp.exp(b_last[c] - b[c * c_sz:(c + 1) * c_sz])
                             for c in range(nc)], axis=0).astype(BF16)
    dec = [jnp.exp(bl) for bl in b_last]
    ri = lax.broadcasted_iota(jnp.int32, (rows, rows), 0)
    ci = lax.broadcasted_iota(jnp.int32, (rows, rows), 1)
    tril = jnp.where(ci <= ri, ci, -1) >= (ri // c_sz) * c_sz
    gn = gn_ref[...]
    for hd in range(GLA_HEADS):
        ks = slice(hd * GLA_DK, (hd + 1) * GLA_DK)
        vs = slice(hd * GLA_DV, (hd + 1) * GLA_DV)
        v = v_ref[0, :, vs].astype(BF16)
        a = jnp.where(tril, _dot_nt(q_in[:, ks], k_in[:, ks]), 0.0).astype(BF16)
        o_intra = _dot(a, v)
        kvs = [_dot_tn(k_dec[c * c_sz:(c + 1) * c_sz, ks], v[c * c_sz:(c + 1) * c_sz]) for c in range(nc)]
        s = s_sc[hd]
        o_inter = []
        for c in range(nc):
            o_inter.append(_dot(q_in[c * c_sz:(c + 1) * c_sz, ks], s.astype(BF16)))
            dcol = jnp.transpose(jnp.broadcast_to(dec[c][:, ks], (GLA_DK, GLA_DK)))
            s = jnp.concatenate([dcol] * (GLA_DV // GLA_DK), axis=1) * s + kvs[c]
        s_sc[hd] = s
        o = o_intra + jnp.concatenate(o_inter, axis=0)
        ocat_sc[:, vs] = _gla_gate(o, r_ref[0, :, vs], gn)
    m = _dot(ocat_sc[...].astype(BF16), wo_ref[...])
    y_ref[0] = x_ref[0] + _rms(m, gpost_ref[...])

    @pl.when(pl.program_id(1) == pl.num_programs(1) - 1)
    def _():
        sfin_ref[0] = s_sc[...]


def _gla_prompt(q, k, v, r, la, x, s0, gn, wo, gpost, rows):
    B, T, D = x.shape
    hk = GLA_HEADS * GLA_DK
    hv = GLA_HEADS * GLA_DV
    blk = lambda w: pl.BlockSpec((1, rows, w), lambda b, i: (b, i, 0))
    return pl.pallas_call(
        functools.partial(_gla_prompt_kernel, rows=rows), grid=(B, T // rows),
        in_specs=[blk(hk), blk(hk), blk(hv), blk(hv), blk(hk), blk(D), _full(s0.shape), _full(gn.shape),
                  _full(wo.shape), _full(gpost.shape)],
        out_specs=[blk(D), pl.BlockSpec((1,) + s0.shape, lambda b, i: (b, 0, 0, 0))],
        out_shape=[jax.ShapeDtypeStruct((B, T, D), F32), jax.ShapeDtypeStruct((B,) + s0.shape, F32)],
        scratch_shapes=[pltpu.VMEM(s0.shape, F32), pltpu.VMEM((rows, hv), F32)],
        compiler_params=_cparams("arbitrary", "arbitrary"), name="gla_prompt",
    )(q, k, v, r, la, x, s0, gn, wo, gpost)


GLA_SHORT_ROWS = 16


def _gla_tokens_kernel(q_ref, k_ref, la_ref, v_ref, s0_ref, o_ref, sfin_ref):
    nblk, rows, _ = q_ref.shape
    zk = jnp.zeros((LANES - rows, GLA_DK), BF16)
    zv = jnp.zeros((LANES - rows, GLA_DV), BF16)
    tril = (lax.broadcasted_iota(jnp.int32, (rows, LANES), 1) <= lax.broadcasted_iota(jnp.int32, (rows, LANES), 0))
    for n in range(nblk):
        b = _cumsum_rows(la_ref[n], rows)
        b_last = b[rows - 1:rows, :]
        k = k_ref[n]
        q_in = (q_ref[n] * jnp.exp(b)).astype(BF16)
        k_in = (k * jnp.exp(-b)).astype(BF16)
        k_dec = (k * jnp.exp(b_last - b)).astype(BF16)
        dec = jnp.exp(b_last)
        for hd in range(GLA_HEADS):
            ks = slice(hd * GLA_DK, (hd + 1) * GLA_DK)
            vs = slice(hd * GLA_DV, (hd + 1) * GLA_DV)
            v = jnp.concatenate([v_ref[n, :, vs].astype(BF16), zv], axis=0)
            a = jnp.where(tril, _dot_nt(q_in[:, ks], jnp.concatenate([k_in[:, ks], zk], axis=0)), 0.0)
            s = s0_ref[n, hd]
            o_ref[n, :, vs] = _dot(a.astype(BF16), v) + _dot(q_in[:, ks], s.astype(BF16))
            dcol = jnp.transpose(jnp.broadcast_to(dec[:, ks], (GLA_DK, GLA_DK)))
            sfin_ref[n, hd] = (jnp.concatenate([dcol] * (GLA_DV // GLA_DK), axis=1) * s
                               + _dot_tn(jnp.concatenate([k_dec[:, ks], zk], axis=0), v))


def _gla_tokens(q, k, la, v, s0, nblk):
    nseq, rows, hk = q.shape
    hv = v.shape[-1]
    col = pl.BlockSpec((nblk, rows, hk), lambda s: (s, 0, 0))
    val = pl.BlockSpec((nblk, rows, hv), lambda s: (s, 0, 0))
    st = pl.BlockSpec((nblk,) + s0.shape[1:], lambda s: (s, 0, 0, 0))
    return pl.pallas_call(
        _gla_tokens_kernel, grid=(nseq // nblk,),
        in_specs=[col, col, col, val, st], out_specs=[val, st],
        out_shape=[jax.ShapeDtypeStruct((nseq, rows, hv), F32), jax.ShapeDtypeStruct(s0.shape, F32)],
        compiler_params=_cparams("arbitrary"), name="gla_tokens",
    )(q, k, la, v, s0)


def _gla_out_kernel(o_ref, r_ref, x_ref, gn_ref, wo_ref, gpost_ref, y_ref, ocat_sc):
    gn = gn_ref[...]
    for hd in range(GLA_HEADS):
        vs = slice(hd * GLA_DV, (hd + 1) * GLA_DV)
        ocat_sc[:, vs] = _gla_gate(o_ref[:, vs], r_ref[:, vs], gn)
    m = _dot(ocat_sc[...].astype(BF16), wo_ref[...])
    y_ref[...] = x_ref[...] + _rms(m, gpost_ref[...])


def _gla_out_small(o, r, x, gn, wo, gpost):
    R, D = x.shape
    return pl.pallas_call(
        _gla_out_kernel, grid=(1,),
        in_specs=[_full(o.shape), _full(r.shape), _full(x.shape), _full(gn.shape), _full(wo.shape),
                  _full(gpost.shape)],
        out_specs=_full((R, D)), out_shape=jax.ShapeDtypeStruct((R, D), F32),
        scratch_shapes=[pltpu.VMEM(o.shape, F32)],
        compiler_params=_cparams("arbitrary"), name="gla_out_small",
    )(o, r, x, gn, wo, gpost)


def _rope_cos_sin(pos):
    half = QK_ROPE // 2
    inv = ROPE_BASE ** (-np.arange(half, dtype=np.float64) / half)
    ang = np.asarray(pos, np.float64)[:, None] * inv[None, :]
    return np.cos(ang), np.sin(ang)


def _rope_tables(pos):
    c, s = _rope_cos_sin(pos)
    reps = LANES // QK_ROPE
    return (jnp.asarray(np.tile(np.concatenate([c, c], axis=1), (1, reps)), F32),
            jnp.asarray(np.tile(np.concatenate([-s, s], axis=1), (1, reps)), F32))


def _rope_tables_head(pos):
    c, s = _rope_cos_sin(pos)
    z = lambda n: np.zeros((len(pos), n))
    tail = HEAD_W - QK_NOPE - QK_ROPE
    return (jnp.asarray(np.concatenate([z(QK_NOPE), c, c, z(tail)], axis=1), F32),
            jnp.asarray(np.concatenate([z(QK_NOPE), -s, s, z(tail)], axis=1), F32))


def _swap_halves(w):
    half = QK_ROPE // 2
    return jnp.concatenate([w[..., half:], w[..., :half]], axis=-1)


def kernel(x_prompt, x_sample, cache_ckv, cache_kpe, state_gla, page_table, meta_tokens, norm_pre_mix, norm_post_mix, norm_pre_mlp, norm_post_mlp, mla_w_in, mla_q_norm, mla_w_uq, mla_kv_norm, mla_w_uk, mla_w_uv, mla_w_o, gla_w_in, gla_w_gate, gla_b_gate, gla_norm, gla_w_o, mlp_w_up, mlp_w_down):
    B, T, D = x_prompt.shape
    DB, S, _ = x_sample.shape
    H = MLA_HEADS
    n_s = DB * S
    past_len = page_table.shape[1] * cache_ckv.shape[2]
    row = lambda a: a.reshape(1, -1)

    w_in = mla_w_in[0]
    o = Q_LORA + KV_LORA
    w_kpe = w_in[:, o:]
    zpad = jnp.zeros((D, LANES - QK_ROPE), F32)
    w_in_p = jnp.concatenate([w_in[:, :o], w_kpe, zpad, _swap_halves(w_kpe), zpad], axis=1).astype(BF16)
    w_uq = mla_w_uq[0].reshape(Q_LORA, H, QK_NOPE + QK_ROPE)
    w_qn = jnp.pad(w_uq[:, :, :QK_NOPE], ((0, 0), (0, 0), (0, LANES - QK_NOPE))).reshape(Q_LORA, H * LANES)
    w_ukt = jnp.pad(jnp.transpose(mla_w_uk[0], (1, 2, 0)), ((0, 0), (0, LANES - QK_NOPE), (0, 0)))
    w_qpe = w_uq[:, :, QK_NOPE:]
    w_qp = w_qpe.reshape(Q_LORA, H * QK_ROPE).astype(BF16)
    w_qps = _swap_halves(w_qpe).reshape(Q_LORA, H * QK_ROPE).astype(BF16)
    proj_w = (row(norm_pre_mix[0]), w_in_p, row(mla_q_norm[0]), w_qn.astype(BF16), w_ukt.astype(BF16),
              w_qp, w_qps, row(mla_kv_norm[0]))
    w_uv = jnp.transpose(mla_w_uv[0], (1, 0, 2)).astype(BF16)
    w_o = mla_w_o[0].astype(BF16)
    g_post0 = row(norm_post_mix[0])

    x_small = jnp.concatenate([x_sample.reshape(n_s, D), meta_tokens], axis=0)
    pos_small = np.concatenate([past_len + np.tile(np.arange(S), DB), np.arange(N_META)])
    cos_s, sin_s = _rope_tables(pos_small)
    qlat_s, qpe_s, ckv_s, kpe_s, klat_s, kpeb_s = _mla_proj_small(x_small, cos_s, sin_s, proj_w)

    npad = LANES
    kml = jnp.pad(klat_s[n_s:], ((0, npad - N_META), (0, 0)))
    kmp = jnp.pad(kpeb_s[n_s:], ((0, npad - N_META), (0, 0)))
    o_meta = _meta_attn(qlat_s[n_s:].reshape(N_META * H, KV_LORA), qpe_s[n_s:].reshape(N_META * H, QK_ROPE),
                        kml, kmp)
    knl = jnp.pad(klat_s[:n_s].reshape(DB, S, KV_LORA), ((0, 0), (0, npad - S), (0, 0)))
    knp = jnp.pad(kpeb_s[:n_s].reshape(DB, S, QK_ROPE), ((0, 0), (0, npad - S), (0, 0)))
    o_samp = _decode_attn(page_table, qlat_s[:n_s].reshape(n_s * H, KV_LORA),
                          qpe_s[:n_s].reshape(n_s * H, QK_ROPE), cache_ckv, jnp.swapaxes(cache_kpe, 2, 3),
                          0, knl, knp)
    o_small = jnp.concatenate([o_samp.reshape(n_s, H * KV_LORA), o_meta.reshape(N_META, H * KV_LORA)], axis=0)
    xs1 = _mla_out_small(o_small, x_small, w_uv, w_o, g_post0)

    zl = lambda n: jnp.zeros((D, n), F32)
    pad_r = HEAD_W - QK_NOPE - QK_ROPE
    w_in_m = jnp.concatenate([w_in[:, :o], zl(QK_NOPE), w_kpe, zl(pad_r),
                              zl(QK_NOPE), _swap_halves(w_kpe), zl(pad_r)], axis=1).astype(BF16)
    wq = jnp.pad(w_uq, ((0, 0), (0, 0), (0, pad_r))).reshape(Q_LORA, H * HEAD_W).astype(BF16)
    wqs = jnp.pad(_swap_halves(w_qpe), ((0, 0), (0, 0), (QK_NOPE, pad_r))).reshape(Q_LORA, H * HEAD_W).astype(BF16)
    wk = jnp.pad(mla_w_uk[0], ((0, 0), (0, 0), (0, HEAD_W - QK_NOPE))).reshape(KV_LORA, H * HEAD_W).astype(BF16)
    wv_pair = mla_w_uv[0].reshape(KV_LORA, H // 2, 2, V_HEAD)
    wv = jnp.stack([jnp.pad(wv_pair[:, :, r], ((0, 0), (0, 0), (_val_lane0(r), HEAD_W - V_HEAD - _val_lane0(r))))
                    for r in range(2)], axis=2).reshape(KV_LORA, H * HEAD_W).astype(BF16)
    mha_w = (row(norm_pre_mix[0]), w_in_m, row(mla_q_norm[0]), wq, wqs, row(mla_kv_norm[0]), wk, wv)
    cosk, sink = _rope_tables_head(N_META + np.arange(T))
    q_p, k_p, v_p, ckv_p, kpe_p = _mha_proj(x_prompt, cosk, sink, mha_w, tr=512)
    cosm, sinm = _rope_tables_head(np.arange(N_META))
    _, k_m, v_m, _, _ = _mha_proj(meta_tokens[None], cosm, sinm, mha_w, tr=N_META)
    mpad = ((0, 0), (0, 0), (0, LANES - N_META), (0, 0))
    attn_p = _mha_flash(q_p, k_p, v_p, jnp.pad(k_m, mpad), jnp.pad(v_m, mpad), tq=512)

    wup_b, wdn_b = mlp_w_up.astype(BF16), mlp_w_down.astype(BF16)
    mlp0 = (row(norm_pre_mlp[0]), wup_b, wdn_b, row(norm_post_mlp[0]), 0)
    xp2 = _mlp(x_prompt.reshape(B * T, D), *mlp0, tm=512, attn=(attn_p.reshape(B * T, D), w_o, g_post0))
    xs2 = _mlp(xs1, *mlp0, tm=xs1.shape[0])

    hk = GLA_HEADS * GLA_DK
    hv = GLA_HEADS * GLA_DV
    gw = gla_w_in[0]
    gw_p = jnp.concatenate([gw, jnp.zeros((D, LANES - GATE_RANK), F32)], axis=1).astype(BF16)
    wg_p = jnp.concatenate([gla_w_gate[0], jnp.zeros((LANES - GATE_RANK, hk), F32)], axis=0).astype(BF16)
    gla_pw = (row(norm_pre_mix[1]), gw_p, wg_p, row(gla_b_gate[0]))
    gn = row(gla_norm[0])
    gwo = gla_w_o[0].astype(BF16)
    g_post1 = row(norm_post_mix[1])

    q_s, k_s, v_s, r_s, la_s = _gla_proj(xs2, *gla_pw, tr=xs2.shape[0])
    def seqs(a, n, l):
        return jnp.pad(a.reshape(n, l, a.shape[-1]), ((0, 0), (0, GLA_SHORT_ROWS - l), (0, 0)))

    o_gs, s_gs = _gla_tokens(*(seqs(a[:n_s], DB, S) for a in (q_s, k_s, la_s, v_s)), state_gla[0], nblk=4)
    zero_state = jnp.zeros((1, GLA_HEADS, GLA_DK, GLA_DV), F32)
    o_gm, s_gm = _gla_tokens(*(seqs(a[n_s:], 1, N_META) for a in (q_s, k_s, la_s, v_s)), zero_state, nblk=1)
    o_gsmall = jnp.concatenate([o_gs[:, :S].reshape(n_s, hv), o_gm[:, :N_META].reshape(N_META, hv)], axis=0)
    xs3 = _gla_out_small(o_gsmall, r_s, xs2, gn, gwo, g_post1)

    q_p, k_p, v_p, r_p, la_p = _gla_proj(xp2, *gla_pw, tr=512)
    r3 = lambda a: a.reshape(B, T, a.shape[-1])
    xp3, s_gp = _gla_prompt(r3(q_p), r3(k_p), r3(v_p), r3(r_p), r3(la_p), xp2.reshape(B, T, D), s_gm[0],
                            gn, gwo, g_post1, rows=256)

    mlp1 = (row(norm_pre_mlp[1]), wup_b, wdn_b, row(norm_post_mlp[1]), 1)
    y_prompt = _mlp(xp3.reshape(B * T, D), *mlp1, tm=512).reshape(B, T, D)
    xs4 = _mlp(xs3, *mlp1, tm=xs3.shape[0])
    y_sample = xs4[:n_s].reshape(DB, S, D)

    bmeta = lambda a: jnp.broadcast_to(a[None], (B,) + a.shape)
    new_ckv_prompt = jnp.concatenate([bmeta(ckv_s[n_s:]), ckv_p], axis=1)[None]
    new_kpe_prompt = jnp.concatenate([bmeta(kpe_s[n_s:]), kpe_p], axis=1)[None]
    new_ckv_sample = ckv_s[:n_s].reshape(1, DB, S, KV_LORA)
    new_kpe_sample = kpe_s[:n_s].reshape(1, DB, S, QK_ROPE)
    return (y_prompt, y_sample, new_ckv_prompt, new_kpe_prompt, new_ckv_sample, new_kpe_sample,
            s_gp[None], s_gs[None])
```

```python
import functools

import jax
import jax.numpy as jnp
import numpy as np
from jax import lax
from jax.experimental import pallas as pl
from jax.experimental.pallas import tpu as pltpu

F32 = jnp.float32
BF16 = jnp.bfloat16

N_META = 16
MLA_HEADS = 16
Q_LORA = 256
KV_LORA = 256
QK_NOPE = 64
QK_ROPE = 32
V_HEAD = 64
MLA_SCALE = (QK_NOPE + QK_ROPE) ** -0.5
ROPE_BASE = 10000.0
GLA_HEADS = 4
GLA_DK = 128
GLA_DV = 256
GLA_SCALE = GLA_DK ** -0.5
GATE_RANK = 16
GATE_TAU = 16.0
GLA_CHUNK = 64
EPS = 1e-6
LOG2E = 1.4426950408889634

LANES = 128
VMEM_LIMIT = 56 * 1024 * 1024


def _cparams(*sem):
    return pltpu.CompilerParams(dimension_semantics=sem, vmem_limit_bytes=VMEM_LIMIT)


def _rms(x, w):
    return x * lax.rsqrt(jnp.mean(x * x, axis=-1, keepdims=True) + EPS) * w


def _dot(a, b):
    return jnp.dot(a, b, preferred_element_type=F32)


def _dot_nt(a, b):
    return lax.dot_general(a, b, (((1,), (1,)), ((), ())), preferred_element_type=F32)


def _dot_tn(a, b):
    return lax.dot_general(a, b, (((0,), (0,)), ((), ())), preferred_element_type=F32)


def _full(shape):
    n = len(shape)
    return pl.BlockSpec(shape, lambda *_: (0,) * n)


def _mla_proj_kernel(x_ref, cos_ref, sin_ref, gpre_ref, win_ref, qn_ref, wqn_ref, wukt_ref, wqp_ref, wqps_ref,
                     kvn_ref, qlat_ref, qpe_ref, ckv_ref, kpe_ref, klat_ref, kpeb_ref):
    h = _rms(x_ref[...], gpre_ref[...]).astype(BF16)
    a = _dot(h, win_ref[...])
    cqn = _rms(a[:, :Q_LORA], qn_ref[...]).astype(BF16)
    ckv = _rms(a[:, Q_LORA:Q_LORA + KV_LORA], kvn_ref[...])
    cos = cos_ref[...]
    sin = sin_ref[...]
    o = Q_LORA + KV_LORA
    kpe = a[:, o:o + QK_ROPE] * cos[:, :QK_ROPE] + a[:, o + LANES:o + LANES + QK_ROPE] * sin[:, :QK_ROPE]
    q_nope = _dot(cqn, wqn_ref[...])
    qp = _dot(cqn, wqp_ref[...])
    qps = _dot(cqn, wqps_ref[...])
    nl = MLA_HEADS * QK_ROPE // LANES
    qpe = jnp.concatenate(
        [(qp[:, j * LANES:(j + 1) * LANES] * cos + qps[:, j * LANES:(j + 1) * LANES] * sin) * MLA_SCALE
         for j in range(nl)], axis=1)
    for hd in range(MLA_HEADS):
        ql = _dot(q_nope[:, hd * LANES:(hd + 1) * LANES].astype(BF16), wukt_ref[hd])
        qlat_ref[:, hd * KV_LORA:(hd + 1) * KV_LORA] = (ql * MLA_SCALE).astype(BF16)
    qpe_ref[...] = qpe.astype(BF16)
    ckv_ref[...] = ckv
    kpe_ref[...] = kpe
    klat_ref[...] = ckv.astype(BF16)
    kpeb_ref[...] = kpe.astype(BF16)


def _mla_proj_small(x, cos, sin, w):
    R, D = x.shape
    H = MLA_HEADS
    in_specs = [_full((R, D)), _full((R, LANES)), _full((R, LANES))] + [_full(a.shape) for a in w]
    shapes = [((R, H * KV_LORA), BF16), ((R, H * QK_ROPE), BF16), ((R, KV_LORA), F32),
              ((R, QK_ROPE), F32), ((R, KV_LORA), BF16), ((R, QK_ROPE), BF16)]
    return pl.pallas_call(
        _mla_proj_kernel, grid=(1,), in_specs=in_specs,
        out_specs=[_full(s) for s, _ in shapes],
        out_shape=[jax.ShapeDtypeStruct(s, d) for s, d in shapes],
        compiler_params=_cparams("arbitrary"), name="mla_proj_small",
    )(x, cos, sin, *w)


HEAD_W = LANES


def _val_lane0(hd):
    return (hd % 2) * V_HEAD


def _sum_lane(hd):
    return (1 - hd % 2) * V_HEAD


def _mha_proj_kernel(x_ref, cosk_ref, sink_ref, gpre_ref, win_ref, qn_ref, wq_ref, wqs_ref, kvn_ref,
                     wk_ref, wv_ref, q_ref, k_ref, v_ref, ckv_ref, kpe_ref):
    h = _rms(x_ref[0], gpre_ref[...]).astype(BF16)
    a = _dot(h, win_ref[...])
    cqn = _rms(a[:, :Q_LORA], qn_ref[...]).astype(BF16)
    ckv = _rms(a[:, Q_LORA:Q_LORA + KV_LORA], kvn_ref[...])
    cosk = cosk_ref[...]
    sink = sink_ref[...]
    o = Q_LORA + KV_LORA
    kpe = a[:, o:o + HEAD_W] * cosk + a[:, o + HEAD_W:o + 2 * HEAD_W] * sink
    ckv_ref[0] = ckv
    kpe_ref[0] = kpe[:, QK_NOPE:QK_NOPE + QK_ROPE]
    ckv_b = ckv.astype(BF16)
    k_all = _dot(ckv_b, wk_ref[...])
    v_all = _dot(ckv_b, wv_ref[...])
    q_raw = _dot(cqn, wq_ref[...])
    q_swp = _dot(cqn, wqs_ref[...])
    lane = lax.broadcasted_iota(jnp.int32, (1, HEAD_W), 1)
    cosq = cosk + jnp.where(lane < QK_NOPE, 1.0, 0.0)
    for hd in range(MLA_HEADS):
        sl = slice(hd * HEAD_W, (hd + 1) * HEAD_W)
        k_ref[0, hd] = (k_all[:, sl] + kpe).astype(BF16)
        q_ref[0, hd] = ((q_raw[:, sl] * cosq + q_swp[:, sl] * sink) * (MLA_SCALE * LOG2E)).astype(BF16)
        v_ref[0, hd] = (v_all[:, sl] + jnp.where(lane == _sum_lane(hd), 1.0, 0.0)).astype(BF16)


def _mha_proj(x, cosk, sink, w, tr):
    B, T, D = x.shape
    H = MLA_HEADS
    in_specs = [pl.BlockSpec((1, tr, D), lambda b, i: (b, i, 0)),
                pl.BlockSpec((tr, HEAD_W), lambda b, i: (i, 0)),
                pl.BlockSpec((tr, HEAD_W), lambda b, i: (i, 0))] + [_full(a.shape) for a in w]
    out_shape = [jax.ShapeDtypeStruct((B, H, T, HEAD_W), BF16),
                 jax.ShapeDtypeStruct((B, H, T, HEAD_W), BF16),
                 jax.ShapeDtypeStruct((B, H, T, HEAD_W), BF16),
                 jax.ShapeDtypeStruct((B, T, KV_LORA), F32),
                 jax.ShapeDtypeStruct((B, T, QK_ROPE), F32)]
    out_specs = [pl.BlockSpec((1, H, tr, HEAD_W), lambda b, i: (b, 0, i, 0)),
                 pl.BlockSpec((1, H, tr, HEAD_W), lambda b, i: (b, 0, i, 0)),
                 pl.BlockSpec((1, H, tr, HEAD_W), lambda b, i: (b, 0, i, 0)),
                 pl.BlockSpec((1, tr, KV_LORA), lambda b, i: (b, i, 0)),
                 pl.BlockSpec((1, tr, QK_ROPE), lambda b, i: (b, i, 0))]
    return pl.pallas_call(
        _mha_proj_kernel, grid=(B, T // tr), in_specs=in_specs, out_specs=out_specs, out_shape=out_shape,
        compiler_params=_cparams("arbitrary", "arbitrary"), name="mha_proj",
    )(x, cosk, sink, *w)


def _mha_flash_kernel(q_ref, k_ref, v_ref, km_ref, vm_ref, o_ref, m_sc, acc_sc, *, tq):
    i = pl.program_id(2)
    nh = q_ref.shape[1]

    def scores(hh, rows, kb, mask):
        s = _dot_nt(q_ref[0, hh, rows], kb)
        return s if mask is None else jnp.where(mask, s, -jnp.inf)

    def apply(hh, rows, s, vb, first):
        parts = [s[:, c * LANES:(c + 1) * LANES] for c in range(s.shape[1] // LANES)]
        mrow = jnp.max(functools.reduce(jnp.maximum, parts), axis=-1, keepdims=True)
        if first:
            m_new = jnp.broadcast_to(mrow, (s.shape[0], LANES))
        else:
            m_old = m_sc[hh, rows]
            m_new = jnp.maximum(m_old, mrow)
        p = jnp.concatenate([jnp.exp2(part - m_new) for part in parts], axis=1)
        pv = _dot(p.astype(BF16), vb)
        if first:
            acc_sc[hh, rows] = pv
        else:
            acc_sc[hh, rows] = jnp.exp2(m_old - m_new) * acc_sc[hh, rows] + pv
        m_sc[hh, rows] = m_new

    nm = km_ref.shape[2]

    def step(pieces, first):
        ss = [[scores(hh, rows, kb(hh), mask) for rows, kb, _, mask in pieces] for hh in range(nh)]
        for hh in range(nh):
            for (rows, _, vb, _), s in zip(pieces, ss[hh]):
                apply(hh, rows, s, vb(hh), first)

    off = pl.multiple_of(i * tq, tq)
    hq = tq // 2

    def diag_piece(first_row, nkeys):
        rows = slice(first_row, first_row + hq)
        col = lax.broadcasted_iota(jnp.int32, (1, nkeys + nm), 1)
        need = jnp.where(col < first_row, -1,
                         jnp.where(col < nkeys, col - first_row, jnp.where(col - nkeys < N_META, -1, tq)))
        mask = lax.broadcasted_iota(jnp.int32, (hq, nkeys + nm), 0) >= need
        kb = lambda hh: jnp.concatenate([k_ref[0, hh, pl.ds(off, nkeys), :], km_ref[0, hh]], axis=0)
        vb = lambda g: jnp.concatenate([v_ref[0, g, pl.ds(off, nkeys), :], vm_ref[0, g]], axis=0)
        return rows, kb, vb, mask

    step([diag_piece(0, hq), diag_piece(hq, tq)], True)

    def body(j, carry):
        offj = pl.multiple_of(j * tq, tq)
        step([(slice(None), lambda hh: k_ref[0, hh, pl.ds(offj, tq), :],
               lambda g: v_ref[0, g, pl.ds(offj, tq), :], None)], False)
        return carry

    lax.fori_loop(0, i, body, 0)

    lane = lax.broadcasted_iota(jnp.int32, (tq, LANES), 1)
    for g in range(nh // 2):
        pair = []
        for hh in (2 * g, 2 * g + 1):
            acc = acc_sc[hh]
            pair.append(acc / acc[:, _sum_lane(hh):_sum_lane(hh) + 1])
        o_ref[0, :, g * LANES:(g + 1) * LANES] = jnp.where(lane < V_HEAD, pair[0], pair[1]).astype(o_ref.dtype)


MHA_HEADS_PER_STEP = 8


def _mha_flash(q, k, v, km, vm, tq):
    B, H, T, W = q.shape
    nh = MHA_HEADS_PER_STEP
    heads = lambda n: pl.BlockSpec((1, nh, n, W), lambda b, p, i: (b, p, 0, 0))
    return pl.pallas_call(
        functools.partial(_mha_flash_kernel, tq=tq), grid=(B, H // nh, T // tq),
        in_specs=[pl.BlockSpec((1, nh, tq, W), lambda b, p, i: (b, p, i, 0)), heads(T), heads(T),
                  pl.BlockSpec((1, nh) + km.shape[2:], lambda b, p, i: (0, p, 0, 0)),
                  pl.BlockSpec((1, nh) + vm.shape[2:], lambda b, p, i: (0, p, 0, 0))],
        out_specs=pl.BlockSpec((1, tq, nh * V_HEAD), lambda b, p, i: (b, i, p)),
        out_shape=jax.ShapeDtypeStruct((B, T, H * V_HEAD), BF16),
        scratch_shapes=[pltpu.VMEM((nh, tq, LANES), F32), pltpu.VMEM((nh, tq, LANES), F32)],
        compiler_params=_cparams("arbitrary", "arbitrary", "arbitrary"), name="mha_flash",
    )(q, k, v, km, vm)


def _meta_attn_kernel(qlat_ref, qpe_ref, kl_ref, kp_ref, o_ref):
    kl = kl_ref[...]
    s = _dot_nt(qlat_ref[...], kl) + _dot_nt(qpe_ref[...], kp_ref[...])
    r, n = s.shape
    tok = lax.broadcasted_iota(jnp.int32, (r // MLA_HEADS, MLA_HEADS, n), 0).reshape(r, n)
    col = lax.broadcasted_iota(jnp.int32, (r, n), 1)
    s = jnp.where(col <= tok, s, -jnp.inf)
    p = jnp.exp(s - jnp.max(s, axis=-1, keepdims=True))
    l = jnp.sum(p, axis=-1, keepdims=True)
    o_ref[...] = _dot(p.astype(BF16), kl) / l


def _meta_attn(qlat, qpe, kl, kp):
    r = qlat.shape[0]
    return pl.pallas_call(
        _meta_attn_kernel, grid=(1,),
        in_specs=[_full(qlat.shape), _full(qpe.shape), _full(kl.shape), _full(kp.shape)],
        out_specs=_full((r, KV_LORA)), out_shape=jax.ShapeDtypeStruct((r, KV_LORA), F32),
        compiler_params=_cparams("arbitrary"), name="meta_attn",
    )(qlat, qpe, kl, kp)


def _decode_kernel(pt_ref, qlat_ref, qpe_ref, knl_ref, knp_ref, ckv_hbm, kpt_hbm, o_ref,
                   ckv_buf, kpt_buf, sem, m_sc, l_sc, acc_sc, *, layer, npages, gp, nbuf, n_new):
    s = pl.program_id(0)
    nseq = pl.num_programs(0)
    ngroups = npages // gp
    page = ckv_buf.shape[1] // gp
    ql = qlat_ref[...]
    qp = qpe_ref[...]
    rq = ql.shape[0]

    def group_copies(seq, g):
        slot = g % nbuf
        cps = []
        for p in range(gp):
            pid = pt_ref[seq * npages + g * gp + p]
            cps.append(pltpu.make_async_copy(ckv_hbm.at[layer, pid],
                                             ckv_buf.at[slot, pl.ds(p * page, page), :], sem.at[slot]))
            cps.append(pltpu.make_async_copy(kpt_hbm.at[layer, pid],
                                             kpt_buf.at[slot, :, pl.ds(p * page, page)], sem.at[slot]))
        return cps

    @pl.when(s == 0)
    def _():
        for g in range(nbuf - 1):
            for cp in group_copies(0, g):
                cp.start()

    m_sc[...] = jnp.full(m_sc.shape, -jnp.inf, F32)
    l_sc[...] = jnp.zeros(l_sc.shape, F32)
    acc_sc[...] = jnp.zeros(acc_sc.shape, F32)

    def accumulate(kl, s):
        parts = [s[:, c * LANES:(c + 1) * LANES] for c in range(s.shape[1] // LANES)]
        m_old = m_sc[...]
        m_new = jnp.maximum(m_old, jnp.max(functools.reduce(jnp.maximum, parts), axis=-1, keepdims=True))
        alpha = jnp.exp(m_old - m_new)
        ps = [jnp.exp(part - m_new) for part in parts]
        p = ps[0] if len(ps) == 1 else jnp.concatenate(ps, axis=1)
        l_sc[...] = alpha * l_sc[...] + functools.reduce(jnp.add, ps)
        acc_sc[...] = (jnp.concatenate([alpha] * (KV_LORA // LANES), axis=1) * acc_sc[...]
                       + _dot(p.astype(BF16), kl))
        m_sc[...] = m_new

    def scores(g):
        for cp in group_copies(s, g):
            cp.wait()
        slot = g % nbuf
        kl = ckv_buf[slot].astype(BF16)
        return kl, _dot_nt(ql, kl) + _dot(qp, kpt_buf[slot].astype(BF16))

    cur = scores(0)
    for g in range(ngroups):
        nxt = g + nbuf - 1
        if nxt < ngroups:
            for cp in group_copies(s, nxt):
                cp.start()
        else:
            @pl.when(s + 1 < nseq)
            def _():
                for cp in group_copies(s + 1, nxt - ngroups):
                    cp.start()
        ahead = scores(g + 1) if g + 1 < ngroups else None
        accumulate(*cur)
        cur = ahead

    kn = knl_ref[0]
    sn = _dot_nt(ql, kn) + _dot_nt(qp, knp_ref[0])
    n = sn.shape[1]
    tok = lax.broadcasted_iota(jnp.int32, (n_new, rq // n_new, n), 0).reshape(rq, n)
    col = lax.broadcasted_iota(jnp.int32, (rq, n), 1)
    accumulate(kn, jnp.where(col <= tok, sn, -jnp.inf))
    o_ref[...] = acc_sc[...] / jnp.sum(l_sc[...], axis=-1, keepdims=True)


DECODE_GROUP_PAGES = 16
DECODE_RING_SLOTS = 4


def _decode_attn(page_table, qlat, qpe, cache_ckv, cache_kpt, layer, knl, knp):
    nseq, npages = page_table.shape
    page = cache_ckv.shape[2]
    n_new = qlat.shape[0] // (nseq * MLA_HEADS)
    rq = n_new * MLA_HEADS
    gp, nbuf = DECODE_GROUP_PAGES, DECODE_RING_SLOTS
    assert npages % gp == 0 and (npages // gp) % nbuf == 0
    pt = page_table.reshape(-1)
    in_specs = [pl.BlockSpec((rq, KV_LORA), lambda s, pt_ref: (s, 0)),
                pl.BlockSpec((rq, QK_ROPE), lambda s, pt_ref: (s, 0)),
                pl.BlockSpec((1,) + knl.shape[1:], lambda s, pt_ref: (s, 0, 0)),
                pl.BlockSpec((1,) + knp.shape[1:], lambda s, pt_ref: (s, 0, 0)),
                pl.BlockSpec(memory_space=pl.ANY), pl.BlockSpec(memory_space=pl.ANY)]
    grid_spec = pltpu.PrefetchScalarGridSpec(
        num_scalar_prefetch=1, grid=(nseq,), in_specs=in_specs,
        out_specs=pl.BlockSpec((rq, KV_LORA), lambda s, pt_ref: (s, 0)),
        scratch_shapes=[pltpu.VMEM((nbuf, gp * page, KV_LORA), F32), pltpu.VMEM((nbuf, QK_ROPE, gp * page), F32),
                        pltpu.SemaphoreType.DMA((nbuf,)),
                        pltpu.VMEM((rq, LANES), F32), pltpu.VMEM((rq, LANES), F32),
                        pltpu.VMEM((rq, KV_LORA), F32)])
    return pl.pallas_call(
        functools.partial(_decode_kernel, layer=layer, npages=npages, gp=gp, nbuf=nbuf, n_new=n_new),
        grid_spec=grid_spec,
        out_shape=jax.ShapeDtypeStruct((nseq * rq, KV_LORA), F32),
        compiler_params=_cparams("arbitrary"), name="decode_attn",
    )(pt, qlat, qpe, knl, knp, cache_ckv, cache_kpt)


def _mla_out_kernel(o_ref, x_ref, wuv_ref, wo_ref, gpost_ref, y_ref, ocat_sc):
    for hd in range(MLA_HEADS):
        oh = _dot(o_ref[:, hd * KV_LORA:(hd + 1) * KV_LORA].astype(BF16), wuv_ref[hd])
        ocat_sc[:, hd * V_HEAD:(hd + 1) * V_HEAD] = oh
    m = _dot(ocat_sc[...].astype(BF16), wo_ref[...])
    y_ref[...] = x_ref[...] + _rms(m, gpost_ref[...])


def _mla_out_small(o, x, wuv, wo, gpost):
    R, D = x.shape
    return pl.pallas_call(
        _mla_out_kernel, grid=(1,),
        in_specs=[_full(o.shape), _full(x.shape), _full(wuv.shape), _full(wo.shape), _full(gpost.shape)],
        out_specs=_full((R, D)), out_shape=jax.ShapeDtypeStruct((R, D), F32),
        scratch_shapes=[pltpu.VMEM((R, MLA_HEADS * V_HEAD), F32)],
        compiler_params=_cparams("arbitrary"), name="mla_out_small",
    )(o, x, wuv, wo, gpost)


def _mlp_kernel(*refs, fc, attn):
    if attn:
        a_ref, x_ref, wo_ref, gmix_ref, gpre_ref, wup_ref, wdn_ref, gpost_ref, o_ref = refs
        x = x_ref[...] + _rms(_dot(a_ref[...], wo_ref[...]), gmix_ref[...])
    else:
        x_ref, gpre_ref, wup_ref, wdn_ref, gpost_ref, o_ref = refs
        x = x_ref[...]
    h = _rms(x, gpre_ref[...]).astype(BF16)
    dff = wup_ref.shape[1]
    acc = None
    for c in range(dff // fc):
        u = _dot(h, wup_ref[:, c * fc:(c + 1) * fc])
        u = jnp.square(jnp.maximum(u, 0.0)).astype(BF16)
        d = _dot(u, wdn_ref[c * fc:(c + 1) * fc, :])
        acc = d if acc is None else acc + d
    o_ref[...] = x + _rms(acc, gpost_ref[...])


def _mlp(x, gpre, wup, wdn, gpost, layer, tm, fc=1024, attn=None):
    N, D = x.shape
    const = lambda a: pl.BlockSpec(a.shape, lambda i: (0, 0), pipeline_mode=pl.Buffered(1))
    stacked = lambda a: pl.BlockSpec((None,) + a.shape[1:], lambda i: (layer, 0, 0),
                                     pipeline_mode=pl.Buffered(1))
    rows = lambda: pl.BlockSpec((tm, D), lambda i: (i, 0))
    args = [x, gpre, wup, wdn, gpost]
    in_specs = [rows(), const(gpre), stacked(wup), stacked(wdn), const(gpost)]
    if attn is not None:
        a, wo, gmix = attn
        args = [a, x, wo, gmix] + args[1:]
        in_specs = [rows(), rows(), const(wo), const(gmix)] + in_specs[1:]
    return pl.pallas_call(
        functools.partial(_mlp_kernel, fc=fc, attn=attn is not None), grid=(N // tm,),
        in_specs=in_specs, out_specs=rows(),
        out_shape=jax.ShapeDtypeStruct((N, D), F32),
        compiler_params=_cparams("arbitrary"), name="mlp",
    )(*args)


def _gla_project(x, gpre_ref, win_ref, wg_ref, bg_ref):
    hk = GLA_HEADS * GLA_DK
    hv = GLA_HEADS * GLA_DV
    h = _rms(x, gpre_ref[...]).astype(BF16)
    a = _dot(h, win_ref[...])
    gd = a[:, 2 * hk + 2 * hv:].astype(BF16)
    z = _dot(gd, wg_ref[...]) + bg_ref[...]
    la = (jnp.minimum(z, 0.0) - jnp.log(1.0 + jnp.exp(-jnp.abs(z)))) * (1.0 / GATE_TAU)
    return (a[:, :hk] * GLA_SCALE, a[:, hk:2 * hk], a[:, 2 * hk:2 * hk + hv].astype(BF16),
            a[:, 2 * hk + hv:2 * hk + 2 * hv], la)


def _gla_proj_kernel(x_ref, gpre_ref, win_ref, wg_ref, bg_ref, q_ref, k_ref, v_ref, r_ref, la_ref):
    q_ref[...], k_ref[...], v_ref[...], r_ref[...], la_ref[...] = _gla_project(
        x_ref[...], gpre_ref, win_ref, wg_ref, bg_ref)


def _gla_proj(x, gpre, win, wg, bg, tr):
    N, D = x.shape
    hk = GLA_HEADS * GLA_DK
    hv = GLA_HEADS * GLA_DV
    row = lambda w: pl.BlockSpec((tr, w), lambda i: (i, 0))
    return pl.pallas_call(
        _gla_proj_kernel, grid=(N // tr,),
        in_specs=[row(D), _full(gpre.shape), _full(win.shape), _full(wg.shape), _full(bg.shape)],
        out_specs=[row(hk), row(hk), row(hv), row(hv), row(hk)],
        out_shape=[jax.ShapeDtypeStruct((N, w), d)
                   for w, d in ((hk, F32), (hk, F32), (hv, BF16), (hv, F32), (hk, F32))],
        compiler_params=_cparams("arbitrary"), name="gla_proj",
    )(x, gpre, win, wg, bg)


def _gla_gate(o, r, gn):
    return _rms(o, gn) * (r / (1.0 + jnp.exp(-r)))


def _cumsum_rows(x, chunk):
    pos = lax.broadcasted_iota(jnp.int32, x.shape, 0) % chunk
    d = 1
    while d < chunk:
        x = x + jnp.where(pos >= d, pltpu.roll(x, d, axis=0), 0.0)
        d *= 2
    return x


def _gla_chunks(q, k, v, r, la, s_sc, gn):
    rows = q.shape[0]
    c_sz = GLA_CHUNK
    nc = rows // c_sz
    b = _cumsum_rows(la, c_sz)
    b_last = [b[(c + 1) * c_sz - 1:(c + 1) * c_sz, :] for c in range(nc)]
    q_in = (q * jnp.exp(b)).astype(BF16)
    k_in = (k * jnp.exp(-b)).astype(BF16)
    k_dec = jnp.concatenate([k[c * c_sz:(c + 1) * c_sz] * jnp.exp(b_last[c] - b[c * c_sz:(c + 1) * c_sz])
                             for c in range(nc)], axis=0).astype(BF16)
    dec = [jnp.exp(bl) for bl in b_last]
    ri = lax.broadcasted_iota(jnp.int32, (rows, rows), 0)
    ci = lax.broadcasted_iota(jnp.int32, (rows, rows), 1)
    tril = jnp.where(ci <= ri, ci, -1) >= (ri // c_sz) * c_sz
    outs = []
    for hd in range(GLA_HEADS):
        ks = slice(hd * GLA_DK, (hd + 1) * GLA_DK)
        vs = slice(hd * GLA_DV, (hd + 1) * GLA_DV)
        vh = v[:, vs]
        a = jnp.where(tril, _dot_nt(q_in[:, ks], k_in[:, ks]), 0.0).astype(BF16)
        o_intra = _dot(a, vh)
        kvs = [_dot_tn(k_dec[c * c_sz:(c + 1) * c_sz, ks], vh[c * c_sz:(c + 1) * c_sz]) for c in range(nc)]
        s = s_sc[hd]
        o_inter = []
        for c in range(nc):
            o_inter.append(_dot(q_in[c * c_sz:(c + 1) * c_sz, ks], s.astype(BF16)))
            dcol = jnp.transpose(jnp.broadcast_to(dec[c][:, ks], (GLA_DK, GLA_DK)))
            s = jnp.concatenate([dcol] * (GLA_DV // GLA_DK), axis=1) * s + kvs[c]
        s_sc[hd] = s
        outs.append(_gla_gate(o_intra + jnp.concatenate(o_inter, axis=0), r[:, vs], gn))
    return jnp.concatenate(outs, axis=1)


GLA_BLOCK_ROWS = 256


def _gla_prompt_kernel(x_ref, gpre_ref, win_ref, wg_ref, bg_ref, s0_ref, gn_ref, wo_ref, gpost_ref,
                       y_ref, sfin_ref, s_sc, ocat_sc):
    @pl.when(pl.program_id(1) == 0)
    def _():
        s_sc[...] = s0_ref[...]

    x = x_ref[0]
    q, k, v, r, la = _gla_project(x, gpre_ref, win_ref, wg_ref, bg_ref)
    gn = gn_ref[...]
    for r0 in range(0, x.shape[0], GLA_BLOCK_ROWS):
        sl = slice(r0, r0 + GLA_BLOCK_ROWS)
        ocat_sc[sl, :] = _gla_chunks(q[sl], k[sl], v[sl], r[sl], la[sl], s_sc, gn)
    m = _dot(ocat_sc[...].astype(BF16), wo_ref[...])
    y_ref[0] = x + _rms(m, gpost_ref[...])

    @pl.when(pl.program_id(1) == pl.num_programs(1) - 1)
    def _():
        sfin_ref[0] = s_sc[...]


def _gla_prompt(x, gpre, win, wg, bg, s0, gn, wo, gpost, rows):
    B, T, D = x.shape
    hv = GLA_HEADS * GLA_DV
    blk = pl.BlockSpec((1, rows, D), lambda b, i: (b, i, 0))
    const = lambda a: pl.BlockSpec(a.shape, lambda b, i: (0,) * a.ndim, pipeline_mode=pl.Buffered(1))
    return pl.pallas_call(
        _gla_prompt_kernel, grid=(B, T // rows),
        in_specs=[blk] + [const(a) for a in (gpre, win, wg, bg, s0, gn, wo, gpost)],
        out_specs=[blk, pl.BlockSpec((1,) + s0.shape, lambda b, i: (b, 0, 0, 0))],
        out_shape=[jax.ShapeDtypeStruct((B, T, D), F32), jax.ShapeDtypeStruct((B,) + s0.shape, F32)],
        scratch_shapes=[pltpu.VMEM(s0.shape, F32), pltpu.VMEM((rows, hv), F32)],
        compiler_params=_cparams("arbitrary", "arbitrary"), name="gla_prompt",
    )(x, gpre, win, wg, bg, s0, gn, wo, gpost)


GLA_SHORT_ROWS = 16


def _gla_tokens_kernel(q_ref, k_ref, la_ref, v_ref, s0_ref, o_ref, sfin_ref):
    nblk, rows, _ = q_ref.shape
    zk = jnp.zeros((LANES - rows, GLA_DK), BF16)
    zv = jnp.zeros((LANES - rows, GLA_DV), BF16)
    tril = (lax.broadcasted_iota(jnp.int32, (rows, LANES), 1) <= lax.broadcasted_iota(jnp.int32, (rows, LANES), 0))
    for n in range(nblk):
        b = _cumsum_rows(la_ref[n], rows)
        b_last = b[rows - 1:rows, :]
        k = k_ref[n]
        q_in = (q_ref[n] * jnp.exp(b)).astype(BF16)
        k_in = (k * jnp.exp(-b)).astype(BF16)
        k_dec = (k * jnp.exp(b_last - b)).astype(BF16)
        dec = jnp.exp(b_last)
        for hd in range(GLA_HEADS):
            ks = slice(hd * GLA_DK, (hd + 1) * GLA_DK)
            vs = slice(hd * GLA_DV, (hd + 1) * GLA_DV)
            v = jnp.concatenate([v_ref[n, :, vs].astype(BF16), zv], axis=0)
            a = jnp.where(tril, _dot_nt(q_in[:, ks], jnp.concatenate([k_in[:, ks], zk], axis=0)), 0.0)
            s = s0_ref[n, hd]
            o_ref[n, :, vs] = _dot(a.astype(BF16), v) + _dot(q_in[:, ks], s.astype(BF16))
            dcol = jnp.transpose(jnp.broadcast_to(dec[:, ks], (GLA_DK, GLA_DK)))
            sfin_ref[n, hd] = (jnp.concatenate([dcol] * (GLA_DV // GLA_DK), axis=1) * s
                               + _dot_tn(jnp.concatenate([k_dec[:, ks], zk], axis=0), v))


def _gla_tokens(q, k, la, v, s0, nblk):
    nseq, rows, hk = q.shape
    hv = v.shape[-1]
    col = pl.BlockSpec((nblk, rows, hk), lambda s: (s, 0, 0))
    val = pl.BlockSpec((nblk, rows, hv), lambda s: (s, 0, 0))
    st = pl.BlockSpec((nblk,) + s0.shape[1:], lambda s: (s, 0, 0, 0))
    return pl.pallas_call(
        _gla_tokens_kernel, grid=(nseq // nblk,),
        in_specs=[col, col, col, val, st], out_specs=[val, st],
        out_shape=[jax.ShapeDtypeStruct((nseq, rows, hv), F32), jax.ShapeDtypeStruct(s0.shape, F32)],
        compiler_params=_cparams("arbitrary"), name="gla_tokens",
    )(q, k, la, v, s0)


def _gla_out_kernel(o_ref, r_ref, x_ref, gn_ref, wo_ref, gpost_ref, y_ref, ocat_sc):
    gn = gn_ref[...]
    for hd in range(GLA_HEADS):
        vs = slice(hd * GLA_DV, (hd + 1) * GLA_DV)
        ocat_sc[:, vs] = _gla_gate(o_ref[:, vs], r_ref[:, vs], gn)
    m = _dot(ocat_sc[...].astype(BF16), wo_ref[...])
    y_ref[...] = x_ref[...] + _rms(m, gpost_ref[...])


def _gla_out_small(o, r, x, gn, wo, gpost):
    R, D = x.shape
    return pl.pallas_call(
        _gla_out_kernel, grid=(1,),
        in_specs=[_full(o.shape), _full(r.shape), _full(x.shape), _full(gn.shape), _full(wo.shape),
                  _full(gpost.shape)],
        out_specs=_full((R, D)), out_shape=jax.ShapeDtypeStruct((R, D), F32),
        scratch_shapes=[pltpu.VMEM(o.shape, F32)],
        compiler_params=_cparams("arbitrary"), name="gla_out_small",
    )(o, r, x, gn, wo, gpost)


def _rope_cos_sin(pos):
    half = QK_ROPE // 2
    inv = ROPE_BASE ** (-np.arange(half, dtype=np.float64) / half)
    ang = np.asarray(pos, np.float64)[:, None] * inv[None, :]
    return np.cos(ang), np.sin(ang)


def _rope_tables(pos):
    c, s = _rope_cos_sin(pos)
    reps = LANES // QK_ROPE
    return (jnp.asarray(np.tile(np.concatenate([c, c], axis=1), (1, reps)), F32),
            jnp.asarray(np.tile(np.concatenate([-s, s], axis=1), (1, reps)), F32))


def _rope_tables_head(pos):
    c, s = _rope_cos_sin(pos)
    z = lambda n: np.zeros((len(pos), n))
    tail = HEAD_W - QK_NOPE - QK_ROPE
    return (jnp.asarray(np.concatenate([z(QK_NOPE), c, c, z(tail)], axis=1), F32),
            jnp.asarray(np.concatenate([z(QK_NOPE), -s, s, z(tail)], axis=1), F32))


def _swap_halves(w):
    half = QK_ROPE // 2
    return jnp.concatenate([w[..., half:], w[..., :half]], axis=-1)


def kernel(x_prompt, x_sample, cache_ckv, cache_kpe, state_gla, page_table, meta_tokens, norm_pre_mix, norm_post_mix, norm_pre_mlp, norm_post_mlp, mla_w_in, mla_q_norm, mla_w_uq, mla_kv_norm, mla_w_uk, mla_w_uv, mla_w_o, gla_w_in, gla_w_gate, gla_b_gate, gla_norm, gla_w_o, mlp_w_up, mlp_w_down):
    B, T, D = x_prompt.shape
    DB, S, _ = x_sample.shape
    H = MLA_HEADS
    n_s = DB * S
    past_len = page_table.shape[1] * cache_ckv.shape[2]
    row = lambda a: a.reshape(1, -1)

    w_in = mla_w_in[0]
    o = Q_LORA + KV_LORA
    w_kpe = w_in[:, o:]
    zpad = jnp.zeros((D, LANES - QK_ROPE), F32)
    w_in_p = jnp.concatenate([w_in[:, :o], w_kpe, zpad, _swap_halves(w_kpe), zpad], axis=1).astype(BF16)
    w_uq = mla_w_uq[0].reshape(Q_LORA, H, QK_NOPE + QK_ROPE)
    w_qn = jnp.pad(w_uq[:, :, :QK_NOPE], ((0, 0), (0, 0), (0, LANES - QK_NOPE))).reshape(Q_LORA, H * LANES)
    w_ukt = jnp.pad(jnp.transpose(mla_w_uk[0], (1, 2, 0)), ((0, 0), (0, LANES - QK_NOPE), (0, 0)))
    w_qpe = w_uq[:, :, QK_NOPE:]
    w_qp = w_qpe.reshape(Q_LORA, H * QK_ROPE).astype(BF16)
    w_qps = _swap_halves(w_qpe).reshape(Q_LORA, H * QK_ROPE).astype(BF16)
    proj_w = (row(norm_pre_mix[0]), w_in_p, row(mla_q_norm[0]), w_qn.astype(BF16), w_ukt.astype(BF16),
              w_qp, w_qps, row(mla_kv_norm[0]))
    w_uv = jnp.transpose(mla_w_uv[0], (1, 0, 2)).astype(BF16)
    w_o = mla_w_o[0].astype(BF16)
    g_post0 = row(norm_post_mix[0])

    x_small = jnp.concatenate([x_sample.reshape(n_s, D), meta_tokens], axis=0)
    pos_small = np.concatenate([past_len + np.tile(np.arange(S), DB), np.arange(N_META)])
    cos_s, sin_s = _rope_tables(pos_small)
    qlat_s, qpe_s, ckv_s, kpe_s, klat_s, kpeb_s = _mla_proj_small(x_small, cos_s, sin_s, proj_w)

    npad = LANES
    kml = jnp.pad(klat_s[n_s:], ((0, npad - N_META), (0, 0)))
    kmp = jnp.pad(kpeb_s[n_s:], ((0, npad - N_META), (0, 0)))
    o_meta = _meta_attn(qlat_s[n_s:].reshape(N_META * H, KV_LORA), qpe_s[n_s:].reshape(N_META * H, QK_ROPE),
                        kml, kmp)
    knl = jnp.pad(klat_s[:n_s].reshape(DB, S, KV_LORA), ((0, 0), (0, npad - S), (0, 0)))
    knp = jnp.pad(kpeb_s[:n_s].reshape(DB, S, QK_ROPE), ((0, 0), (0, npad - S), (0, 0)))
    o_samp = _decode_attn(page_table, qlat_s[:n_s].reshape(n_s * H, KV_LORA),
                          qpe_s[:n_s].reshape(n_s * H, QK_ROPE), cache_ckv, jnp.swapaxes(cache_kpe, 2, 3),
                          0, knl, knp)
    o_small = jnp.concatenate([o_samp.reshape(n_s, H * KV_LORA), o_meta.reshape(N_META, H * KV_LORA)], axis=0)
    xs1 = _mla_out_small(o_small, x_small, w_uv, w_o, g_post0)

    zl = lambda n: jnp.zeros((D, n), F32)
    pad_r = HEAD_W - QK_NOPE - QK_ROPE
    w_in_m = jnp.concatenate([w_in[:, :o], zl(QK_NOPE), w_kpe, zl(pad_r),
                              zl(QK_NOPE), _swap_halves(w_kpe), zl(pad_r)], axis=1).astype(BF16)
    wq = jnp.pad(w_uq, ((0, 0), (0, 0), (0, pad_r))).reshape(Q_LORA, H * HEAD_W).astype(BF16)
    wqs = jnp.pad(_swap_halves(w_qpe), ((0, 0), (0, 0), (QK_NOPE, pad_r))).reshape(Q_LORA, H * HEAD_W).astype(BF16)
    wk = jnp.pad(mla_w_uk[0], ((0, 0), (0, 0), (0, HEAD_W - QK_NOPE))).reshape(KV_LORA, H * HEAD_W).astype(BF16)
    wv_pair = mla_w_uv[0].reshape(KV_LORA, H // 2, 2, V_HEAD)
    wv = jnp.stack([jnp.pad(wv_pair[:, :, r], ((0, 0), (0, 0), (_val_lane0(r), HEAD_W - V_HEAD - _val_lane0(r))))
                    for r in range(2)], axis=2).reshape(KV_LORA, H * HEAD_W).astype(BF16)
    mha_w = (row(norm_pre_mix[0]), w_in_m, row(mla_q_norm[0]), wq, wqs, row(mla_kv_norm[0]), wk, wv)
    cosk, sink = _rope_tables_head(N_META + np.arange(T))
    q_p, k_p, v_p, ckv_p, kpe_p = _mha_proj(x_prompt, cosk, sink, mha_w, tr=512)
    cosm, sinm = _rope_tables_head(np.arange(N_META))
    _, k_m, v_m, _, _ = _mha_proj(meta_tokens[None], cosm, sinm, mha_w, tr=N_META)
    mpad = ((0, 0), (0, 0), (0, LANES - N_META), (0, 0))
    attn_p = _mha_flash(q_p, k_p, v_p, jnp.pad(k_m, mpad), jnp.pad(v_m, mpad), tq=512)

    wup_b, wdn_b = mlp_w_up.astype(BF16), mlp_w_down.astype(BF16)
    mlp0 = (row(norm_pre_mlp[0]), wup_b, wdn_b, row(norm_post_mlp[0]), 0)
    xp2 = _mlp(x_prompt.reshape(B * T, D), *mlp0, tm=512, attn=(attn_p.reshape(B * T, D), w_o, g_post0))
    xs2 = _mlp(xs1, *mlp0, tm=xs1.shape[0])

    hk = GLA_HEADS * GLA_DK
    hv = GLA_HEADS * GLA_DV
    gw = gla_w_in[0]
    gw_p = jnp.concatenate([gw, jnp.zeros((D, LANES - GATE_RANK), F32)], axis=1).astype(BF16)
    wg_p = jnp.concatenate([gla_w_gate[0], jnp.zeros((LANES - GATE_RANK, hk), F32)], axis=0).astype(BF16)
    gla_pw = (row(norm_pre_mix[1]), gw_p, wg_p, row(gla_b_gate[0]))
    gn = row(gla_norm[0])
    gwo = gla_w_o[0].astype(BF16)
    g_post1 = row(norm_post_mix[1])

    q_s, k_s, v_s, r_s, la_s = _gla_proj(xs2, *gla_pw, tr=xs2.shape[0])
    def seqs(a, n, l):
        return jnp.pad(a.reshape(n, l, a.shape[-1]), ((0, 0), (0, GLA_SHORT_ROWS - l), (0, 0)))

    o_gs, s_gs = _gla_tokens(*(seqs(a[:n_s], DB, S) for a in (q_s, k_s, la_s, v_s)), state_gla[0], nblk=4)
    zero_state = jnp.zeros((1, GLA_HEADS, GLA_DK, GLA_DV), F32)
    o_gm, s_gm = _gla_tokens(*(seqs(a[n_s:], 1, N_META) for a in (q_s, k_s, la_s, v_s)), zero_state, nblk=1)
    o_gsmall = jnp.concatenate([o_gs[:, :S].reshape(n_s, hv), o_gm[:, :N_META].reshape(N_META, hv)], axis=0)
    xs3 = _gla_out_small(o_gsmall, r_s, xs2, gn, gwo, g_post1)

    xp3, s_gp = _gla_prompt(xp2.reshape(B, T, D), *gla_pw, s_gm[0], gn, gwo, g_post1, rows=512)

    mlp1 = (row(norm_pre_mlp[1]), wup_b, wdn_b, row(norm_post_mlp[1]), 1)
    y_prompt = _mlp(xp3.reshape(B * T, D), *mlp1, tm=512).reshape(B, T, D)
    xs4 = _mlp(xs3, *mlp1, tm=xs3.shape[0])
    y_sample = xs4[:n_s].reshape(DB, S, D)

    bmeta = lambda a: jnp.broadcast_to(a[None], (B,) + a.shape)
    new_ckv_prompt = jnp.concatenate([bmeta(ckv_s[n_s:]), ckv_p], axis=1)[None]
    new_kpe_prompt = jnp.concatenate([bmeta(kpe_s[n_s:]), kpe_p], axis=1)[None]
    new_ckv_sample = ckv_s[:n_s].reshape(1, DB, S, KV_LORA)
    new_kpe_sample = kpe_s[:n_s].reshape(1, DB, S, QK_ROPE)
    return (y_prompt, y_sample, new_ckv_prompt, new_kpe_prompt, new_ckv_sample, new_kpe_sample,
            s_gp[None], s_gs[None])
```

```python
import functools

import jax
import jax.numpy as jnp
import numpy as np
from jax import lax
from jax.experimental import pallas as pl
from jax.experimental.pallas import tpu as pltpu

F32 = jnp.float32
BF16 = jnp.bfloat16

N_META = 16
MLA_HEADS = 16
Q_LORA = 256
KV_LORA = 256
QK_NOPE = 64
QK_ROPE = 32
V_HEAD = 64
MLA_SCALE = (QK_NOPE + QK_ROPE) ** -0.5
ROPE_BASE = 10000.0
GLA_HEADS = 4
GLA_DK = 128
GLA_DV = 256
GLA_SCALE = GLA_DK ** -0.5
GATE_RANK = 16
GATE_TAU = 16.0
GLA_CHUNK = 64
EPS = 1e-6
LOG2E = 1.4426950408889634

LANES = 128
VMEM_LIMIT = 56 * 1024 * 1024
ROW_TILE = 512


def _cparams(*sem):
    return pltpu.CompilerParams(dimension_semantics=sem, vmem_limit_bytes=VMEM_LIMIT)


def _rms(x, w):
    return x * lax.rsqrt(jnp.mean(x * x, axis=-1, keepdims=True) + EPS) * w


def _dot(a, b):
    return jnp.dot(a, b, preferred_element_type=F32)


def _dot_nt(a, b):
    return lax.dot_general(a, b, (((1,), (1,)), ((), ())), preferred_element_type=F32)


def _dot_tn(a, b):
    return lax.dot_general(a, b, (((0,), (0,)), ((), ())), preferred_element_type=F32)


def _full(shape):
    n = len(shape)
    return pl.BlockSpec(shape, lambda *_: (0,) * n)


def _mla_proj_kernel(x_ref, cos_ref, sin_ref, gpre_ref, win_ref, qn_ref, wqn_ref, wukt_ref, wqp_ref, wqps_ref,
                     kvn_ref, qlat_ref, qpe_ref, ckv_ref, kpe_ref, klat_ref, kpeb_ref):
    h = _rms(x_ref[...], gpre_ref[...]).astype(BF16)
    a = _dot(h, win_ref[...])
    cqn = _rms(a[:, :Q_LORA], qn_ref[...]).astype(BF16)
    ckv = _rms(a[:, Q_LORA:Q_LORA + KV_LORA], kvn_ref[...])
    cos = cos_ref[...]
    sin = sin_ref[...]
    o = Q_LORA + KV_LORA
    kpe = a[:, o:o + QK_ROPE] * cos[:, :QK_ROPE] + a[:, o + LANES:o + LANES + QK_ROPE] * sin[:, :QK_ROPE]
    q_nope = _dot(cqn, wqn_ref[...])
    qp = _dot(cqn, wqp_ref[...])
    qps = _dot(cqn, wqps_ref[...])
    nl = MLA_HEADS * QK_ROPE // LANES
    qpe = jnp.concatenate(
        [(qp[:, j * LANES:(j + 1) * LANES] * cos + qps[:, j * LANES:(j + 1) * LANES] * sin) * MLA_SCALE
         for j in range(nl)], axis=1)
    for hd in range(MLA_HEADS):
        ql = _dot(q_nope[:, hd * LANES:(hd + 1) * LANES].astype(BF16), wukt_ref[hd])
        qlat_ref[:, hd * KV_LORA:(hd + 1) * KV_LORA] = (ql * MLA_SCALE).astype(BF16)
    qpe_ref[...] = qpe.astype(BF16)
    ckv_ref[...] = ckv
    kpe_ref[...] = kpe
    klat_ref[...] = ckv.astype(BF16)
    kpeb_ref[...] = kpe.astype(BF16)


def _mla_proj_small(x, cos, sin, w):
    R, D = x.shape
    H = MLA_HEADS
    in_specs = [_full((R, D)), _full((R, LANES)), _full((R, LANES))] + [_full(a.shape) for a in w]
    shapes = [((R, H * KV_LORA), BF16), ((R, H * QK_ROPE), BF16), ((R, KV_LORA), F32),
              ((R, QK_ROPE), F32), ((R, KV_LORA), BF16), ((R, QK_ROPE), BF16)]
    return pl.pallas_call(
        _mla_proj_kernel, grid=(1,), in_specs=in_specs,
        out_specs=[_full(s) for s, _ in shapes],
        out_shape=[jax.ShapeDtypeStruct(s, d) for s, d in shapes],
        compiler_params=_cparams("arbitrary"), name="mla_proj_small",
    )(x, cos, sin, *w)


HEAD_W = LANES


def _val_lane0(hd):
    return (hd % 2) * V_HEAD


def _sum_lane(hd):
    return (1 - hd % 2) * V_HEAD


def _mha_proj_kernel(x_ref, cosk_ref, sink_ref, gpre_ref, win_ref, qn_ref, wq_ref, wqs_ref, kvn_ref,
                     wk_ref, wv_ref, q_ref, k_ref, v_ref, ckv_ref, kpe_ref):
    h = _rms(x_ref[0], gpre_ref[...]).astype(BF16)
    a = _dot(h, win_ref[...])
    cqn = _rms(a[:, :Q_LORA], qn_ref[...]).astype(BF16)
    ckv = _rms(a[:, Q_LORA:Q_LORA + KV_LORA], kvn_ref[...])
    cosk = cosk_ref[...]
    sink = sink_ref[...]
    o = Q_LORA + KV_LORA
    kpe = a[:, o:o + HEAD_W] * cosk + a[:, o + HEAD_W:o + 2 * HEAD_W] * sink
    ckv_ref[0] = ckv
    kpe_ref[0] = kpe[:, QK_NOPE:QK_NOPE + QK_ROPE]
    ckv_b = ckv.astype(BF16)
    k_all = _dot(ckv_b, wk_ref[...])
    v_all = _dot(ckv_b, wv_ref[...])
    q_raw = _dot(cqn, wq_ref[...])
    q_swp = _dot(cqn, wqs_ref[...])
    lane = lax.broadcasted_iota(jnp.int32, (1, HEAD_W), 1)
    cosq = cosk + jnp.where(lane < QK_NOPE, 1.0, 0.0)
    for hd in range(MLA_HEADS):
        sl = slice(hd * HEAD_W, (hd + 1) * HEAD_W)
        k_ref[0, hd] = (k_all[:, sl] + kpe).astype(BF16)
        q_ref[0, hd] = ((q_raw[:, sl] * cosq + q_swp[:, sl] * sink) * (MLA_SCALE * LOG2E)).astype(BF16)
        v_ref[0, hd] = (v_all[:, sl] + jnp.where(lane == _sum_lane(hd), 1.0, 0.0)).astype(BF16)


def _mha_proj(x, cosk, sink, w, tr):
    B, T, D = x.shape
    H = MLA_HEADS
    in_specs = [pl.BlockSpec((1, tr, D), lambda b, i: (b, i, 0)),
                pl.BlockSpec((tr, HEAD_W), lambda b, i: (i, 0)),
                pl.BlockSpec((tr, HEAD_W), lambda b, i: (i, 0))] + [_full(a.shape) for a in w]
    out_shape = [jax.ShapeDtypeStruct((B, H, T, HEAD_W), BF16),
                 jax.ShapeDtypeStruct((B, H, T, HEAD_W), BF16),
                 jax.ShapeDtypeStruct((B, H, T, HEAD_W), BF16),
                 jax.ShapeDtypeStruct((B, T, KV_LORA), F32),
                 jax.ShapeDtypeStruct((B, T, QK_ROPE), F32)]
    out_specs = [pl.BlockSpec((1, H, tr, HEAD_W), lambda b, i: (b, 0, i, 0)),
                 pl.BlockSpec((1, H, tr, HEAD_W), lambda b, i: (b, 0, i, 0)),
                 pl.BlockSpec((1, H, tr, HEAD_W), lambda b, i: (b, 0, i, 0)),
                 pl.BlockSpec((1, tr, KV_LORA), lambda b, i: (b, i, 0)),
                 pl.BlockSpec((1, tr, QK_ROPE), lambda b, i: (b, i, 0))]
    return pl.pallas_call(
        _mha_proj_kernel, grid=(B, T // tr), in_specs=in_specs, out_specs=out_specs, out_shape=out_shape,
        compiler_params=_cparams("arbitrary", "arbitrary"), name="mha_proj",
    )(x, cosk, sink, *w)


def _mha_flash_kernel(q_ref, k_ref, v_ref, km_ref, vm_ref, o_ref, m_sc, acc_sc, *, tq):
    i = pl.program_id(2)
    nh = q_ref.shape[1]

    def scores(hh, rows, kb, mask):
        s = _dot_nt(q_ref[0, hh, rows], kb)
        return s if mask is None else jnp.where(mask, s, -jnp.inf)

    def apply(hh, rows, s, vb, first):
        parts = [s[:, c * LANES:(c + 1) * LANES] for c in range(s.shape[1] // LANES)]
        mrow = jnp.max(functools.reduce(jnp.maximum, parts), axis=-1, keepdims=True)
        if first:
            m_new = jnp.broadcast_to(mrow, (s.shape[0], LANES))
        else:
            m_old = m_sc[hh, rows]
            m_new = jnp.maximum(m_old, mrow)
        p = jnp.concatenate([jnp.exp2(part - m_new) for part in parts], axis=1)
        pv = _dot(p.astype(BF16), vb)
        if first:
            acc_sc[hh, rows] = pv
        else:
            acc_sc[hh, rows] = jnp.exp2(m_old - m_new) * acc_sc[hh, rows] + pv
        m_sc[hh, rows] = m_new

    nm = km_ref.shape[2]

    def step(pieces, first):
        ss = [[scores(hh, rows, kb(hh), mask) for rows, kb, _, mask in pieces] for hh in range(nh)]
        for hh in range(nh):
            for (rows, _, vb, _), s in zip(pieces, ss[hh]):
                apply(hh, rows, s, vb(hh), first)

    off = pl.multiple_of(i * tq, tq)
    hq = tq // 2

    def diag_piece(first_row, nkeys):
        rows = slice(first_row, first_row + hq)
        col = lax.broadcasted_iota(jnp.int32, (1, nkeys + nm), 1)
        need = jnp.where(col < first_row, -1,
                         jnp.where(col < nkeys, col - first_row, jnp.where(col - nkeys < N_META, -1, tq)))
        mask = lax.broadcasted_iota(jnp.int32, (hq, nkeys + nm), 0) >= need
        kb = lambda hh: jnp.concatenate([k_ref[0, hh, pl.ds(off, nkeys), :], km_ref[0, hh]], axis=0)
        vb = lambda g: jnp.concatenate([v_ref[0, g, pl.ds(off, nkeys), :], vm_ref[0, g]], axis=0)
        return rows, kb, vb, mask

    step([diag_piece(0, hq), diag_piece(hq, tq)], True)

    def body(j, carry):
        offj = pl.multiple_of(j * tq, tq)
        step([(slice(None), lambda hh: k_ref[0, hh, pl.ds(offj, tq), :],
               lambda g: v_ref[0, g, pl.ds(offj, tq), :], None)], False)
        return carry

    lax.fori_loop(0, i, body, 0)

    lane = lax.broadcasted_iota(jnp.int32, (tq, LANES), 1)
    for g in range(nh // 2):
        pair = []
        for hh in (2 * g, 2 * g + 1):
            acc = acc_sc[hh]
            pair.append(acc / acc[:, _sum_lane(hh):_sum_lane(hh) + 1])
        o_ref[0, :, g * LANES:(g + 1) * LANES] = jnp.where(lane < V_HEAD, pair[0], pair[1]).astype(o_ref.dtype)


MHA_HEADS_PER_STEP = 8


def _mha_flash(q, k, v, km, vm, tq):
    B, H, T, W = q.shape
    nh = MHA_HEADS_PER_STEP
    heads = lambda n: pl.BlockSpec((1, nh, n, W), lambda b, p, i: (b, p, 0, 0))
    return pl.pallas_call(
        functools.partial(_mha_flash_kernel, tq=tq), grid=(B, H // nh, T // tq),
        in_specs=[pl.BlockSpec((1, nh, tq, W), lambda b, p, i: (b, p, i, 0)), heads(T), heads(T),
                  pl.BlockSpec((1, nh) + km.shape[2:], lambda b, p, i: (0, p, 0, 0)),
                  pl.BlockSpec((1, nh) + vm.shape[2:], lambda b, p, i: (0, p, 0, 0))],
        out_specs=pl.BlockSpec((1, tq, nh * V_HEAD), lambda b, p, i: (b, i, p)),
        out_shape=jax.ShapeDtypeStruct((B, T, H * V_HEAD), BF16),
        scratch_shapes=[pltpu.VMEM((nh, tq, LANES), F32), pltpu.VMEM((nh, tq, LANES), F32)],
        compiler_params=_cparams("arbitrary", "arbitrary", "arbitrary"), name="mha_flash",
    )(q, k, v, km, vm)


def _meta_attn_kernel(qlat_ref, qpe_ref, kl_ref, kp_ref, o_ref):
    kl = kl_ref[...]
    s = _dot_nt(qlat_ref[...], kl) + _dot_nt(qpe_ref[...], kp_ref[...])
    r, n = s.shape
    tok = lax.broadcasted_iota(jnp.int32, (r // MLA_HEADS, MLA_HEADS, n), 0).reshape(r, n)
    col = lax.broadcasted_iota(jnp.int32, (r, n), 1)
    s = jnp.where(col <= tok, s, -jnp.inf)
    p = jnp.exp(s - jnp.max(s, axis=-1, keepdims=True))
    l = jnp.sum(p, axis=-1, keepdims=True)
    o_ref[...] = _dot(p.astype(BF16), kl) / l


def _meta_attn(qlat, qpe, kl, kp):
    r = qlat.shape[0]
    return pl.pallas_call(
        _meta_attn_kernel, grid=(1,),
        in_specs=[_full(qlat.shape), _full(qpe.shape), _full(kl.shape), _full(kp.shape)],
        out_specs=_full((r, KV_LORA)), out_shape=jax.ShapeDtypeStruct((r, KV_LORA), F32),
        compiler_params=_cparams("arbitrary"), name="meta_attn",
    )(qlat, qpe, kl, kp)


def _decode_kernel(pt_ref, qlat_ref, qpe_ref, knl_ref, knp_ref, ckv_hbm, kpt_hbm, o_ref,
                   ckv_buf, kpt_buf, sem, m_sc, l_sc, acc_sc, *, layer, npages, gp, nbuf, n_new):
    s = pl.program_id(0)
    nseq = pl.num_programs(0)
    ngroups = npages // gp
    page = ckv_buf.shape[1] // gp
    ql = qlat_ref[...]
    qp = qpe_ref[...]
    rq = ql.shape[0]

    def group_copies(seq, g):
        slot = g % nbuf
        cps = []
        for p in range(gp):
            pid = pt_ref[seq * npages + g * gp + p]
            cps.append(pltpu.make_async_copy(ckv_hbm.at[layer, pid],
                                             ckv_buf.at[slot, pl.ds(p * page, page), :], sem.at[slot]))
            cps.append(pltpu.make_async_copy(kpt_hbm.at[layer, pid],
                                             kpt_buf.at[slot, :, pl.ds(p * page, page)], sem.at[slot]))
        return cps

    @pl.when(s == 0)
    def _():
        for g in range(nbuf - 1):
            for cp in group_copies(0, g):
                cp.start()

    m_sc[...] = jnp.full(m_sc.shape, -jnp.inf, F32)
    l_sc[...] = jnp.zeros(l_sc.shape, F32)
    acc_sc[...] = jnp.zeros(acc_sc.shape, F32)

    def accumulate(kl, s):
        parts = [s[:, c * LANES:(c + 1) * LANES] for c in range(s.shape[1] // LANES)]
        m_old = m_sc[...]
        m_new = jnp.maximum(m_old, jnp.max(functools.reduce(jnp.maximum, parts), axis=-1, keepdims=True))
        alpha = jnp.exp(m_old - m_new)
        ps = [jnp.exp(part - m_new) for part in parts]
        p = ps[0] if len(ps) == 1 else jnp.concatenate(ps, axis=1)
        l_sc[...] = alpha * l_sc[...] + functools.reduce(jnp.add, ps)
        acc_sc[...] = (jnp.concatenate([alpha] * (KV_LORA // LANES), axis=1) * acc_sc[...]
                       + _dot(p.astype(BF16), kl))
        m_sc[...] = m_new

    def scores(g):
        for cp in group_copies(s, g):
            cp.wait()
        slot = g % nbuf
        kl = ckv_buf[slot].astype(BF16)
        return kl, _dot_nt(ql, kl) + _dot(qp, kpt_buf[slot].astype(BF16))

    cur = scores(0)
    for g in range(ngroups):
        nxt = g + nbuf - 1
        if nxt < ngroups:
            for cp in group_copies(s, nxt):
                cp.start()
        else:
            @pl.when(s + 1 < nseq)
            def _():
                for cp in group_copies(s + 1, nxt - ngroups):
                    cp.start()
        ahead = scores(g + 1) if g + 1 < ngroups else None
        accumulate(*cur)
        cur = ahead

    kn = knl_ref[0]
    sn = _dot_nt(ql, kn) + _dot_nt(qp, knp_ref[0])
    n = sn.shape[1]
    tok = lax.broadcasted_iota(jnp.int32, (n_new, rq // n_new, n), 0).reshape(rq, n)
    col = lax.broadcasted_iota(jnp.int32, (rq, n), 1)
    accumulate(kn, jnp.where(col <= tok, sn, -jnp.inf))
    o_ref[...] = acc_sc[...] / jnp.sum(l_sc[...], axis=-1, keepdims=True)


DECODE_GROUP_PAGES = 16
DECODE_RING_SLOTS = 4


def _decode_attn(page_table, qlat, qpe, cache_ckv, cache_kpt, layer, knl, knp):
    nseq, npages = page_table.shape
    page = cache_ckv.shape[2]
    n_new = qlat.shape[0] // (nseq * MLA_HEADS)
    rq = n_new * MLA_HEADS
    gp, nbuf = DECODE_GROUP_PAGES, DECODE_RING_SLOTS
    assert npages % gp == 0 and (npages // gp) % nbuf == 0
    pt = page_table.reshape(-1)
    in_specs = [pl.BlockSpec((rq, KV_LORA), lambda s, pt_ref: (s, 0)),
                pl.BlockSpec((rq, QK_ROPE), lambda s, pt_ref: (s, 0)),
                pl.BlockSpec((1,) + knl.shape[1:], lambda s, pt_ref: (s, 0, 0)),
                pl.BlockSpec((1,) + knp.shape[1:], lambda s, pt_ref: (s, 0, 0)),
                pl.BlockSpec(memory_space=pl.ANY), pl.BlockSpec(memory_space=pl.ANY)]
    grid_spec = pltpu.PrefetchScalarGridSpec(
        num_scalar_prefetch=1, grid=(nseq,), in_specs=in_specs,
        out_specs=pl.BlockSpec((rq, KV_LORA), lambda s, pt_ref: (s, 0)),
        scratch_shapes=[pltpu.VMEM((nbuf, gp * page, KV_LORA), F32), pltpu.VMEM((nbuf, QK_ROPE, gp * page), F32),
                        pltpu.SemaphoreType.DMA((nbuf,)),
                        pltpu.VMEM((rq, LANES), F32), pltpu.VMEM((rq, LANES), F32),
                        pltpu.VMEM((rq, KV_LORA), F32)])
    return pl.pallas_call(
        functools.partial(_decode_kernel, layer=layer, npages=npages, gp=gp, nbuf=nbuf, n_new=n_new),
        grid_spec=grid_spec,
        out_shape=jax.ShapeDtypeStruct((nseq * rq, KV_LORA), F32),
        compiler_params=_cparams("arbitrary"), name="decode_attn",
    )(pt, qlat, qpe, knl, knp, cache_ckv, cache_kpt)


def _mla_out_kernel(o_ref, x_ref, wuv_ref, wo_ref, gpost_ref, y_ref, ocat_sc):
    for hd in range(MLA_HEADS):
        oh = _dot(o_ref[:, hd * KV_LORA:(hd + 1) * KV_LORA].astype(BF16), wuv_ref[hd])
        ocat_sc[:, hd * V_HEAD:(hd + 1) * V_HEAD] = oh
    m = _dot(ocat_sc[...].astype(BF16), wo_ref[...])
    y_ref[...] = x_ref[...] + _rms(m, gpost_ref[...])


def _mla_out_small(o, x, wuv, wo, gpost):
    R, D = x.shape
    return pl.pallas_call(
        _mla_out_kernel, grid=(1,),
        in_specs=[_full(o.shape), _full(x.shape), _full(wuv.shape), _full(wo.shape), _full(gpost.shape)],
        out_specs=_full((R, D)), out_shape=jax.ShapeDtypeStruct((R, D), F32),
        scratch_shapes=[pltpu.VMEM((R, MLA_HEADS * V_HEAD), F32)],
        compiler_params=_cparams("arbitrary"), name="mla_out_small",
    )(o, x, wuv, wo, gpost)


def _mlp_kernel(*refs, fc, attn):
    if attn:
        a_ref, x_ref, wo_ref, gmix_ref, gpre_ref, wup_ref, wdn_ref, gpost_ref, o_ref = refs
        x = x_ref[...] + _rms(_dot(a_ref[...], wo_ref[...]), gmix_ref[...])
    else:
        x_ref, gpre_ref, wup_ref, wdn_ref, gpost_ref, o_ref = refs
        x = x_ref[...]
    h = _rms(x, gpre_ref[...]).astype(BF16)
    dff = wup_ref.shape[1]
    acc = None
    for c in range(dff // fc):
        u = _dot(h, wup_ref[:, c * fc:(c + 1) * fc])
        u = jnp.square(jnp.maximum(u, 0.0)).astype(BF16)
        d = _dot(u, wdn_ref[c * fc:(c + 1) * fc, :])
        acc = d if acc is None else acc + d
    o_ref[...] = x + _rms(acc, gpost_ref[...])


def _mlp(x, gpre, wup, wdn, gpost, layer, tm, fc=1024, attn=None):
    N, D = x.shape
    const = lambda a: pl.BlockSpec(a.shape, lambda i: (0, 0), pipeline_mode=pl.Buffered(1))
    stacked = lambda a: pl.BlockSpec((None,) + a.shape[1:], lambda i: (layer, 0, 0),
                                     pipeline_mode=pl.Buffered(1))
    rows = lambda: pl.BlockSpec((tm, D), lambda i: (i, 0))
    args = [x, gpre, wup, wdn, gpost]
    in_specs = [rows(), const(gpre), stacked(wup), stacked(wdn), const(gpost)]
    if attn is not None:
        a, wo, gmix = attn
        args = [a, x, wo, gmix] + args[1:]
        in_specs = [rows(), rows(), const(wo), const(gmix)] + in_specs[1:]
    return pl.pallas_call(
        functools.partial(_mlp_kernel, fc=fc, attn=attn is not None), grid=(N // tm,),
        in_specs=in_specs, out_specs=rows(),
        out_shape=jax.ShapeDtypeStruct((N, D), F32),
        compiler_params=_cparams("arbitrary"), name="mlp",
    )(*args)


def _gla_project(x, gpre_ref, win_ref, wg_ref, bg_ref):
    hk = GLA_HEADS * GLA_DK
    hv = GLA_HEADS * GLA_DV
    h = _rms(x, gpre_ref[...]).astype(BF16)
    a = _dot(h, win_ref[...])
    gd = a[:, 2 * hk + 2 * hv:].astype(BF16)
    z = _dot(gd, wg_ref[...]) + bg_ref[...]
    la = (jnp.minimum(z, 0.0) - jnp.log(1.0 + jnp.exp(-jnp.abs(z)))) * (1.0 / GATE_TAU)
    return (a[:, :hk] * GLA_SCALE, a[:, hk:2 * hk], a[:, 2 * hk:2 * hk + hv].astype(BF16),
            a[:, 2 * hk + hv:2 * hk + 2 * hv], la)


def _gla_proj_kernel(x_ref, gpre_ref, win_ref, wg_ref, bg_ref, q_ref, k_ref, v_ref, r_ref, la_ref):
    q_ref[...], k_ref[...], v_ref[...], r_ref[...], la_ref[...] = _gla_project(
        x_ref[...], gpre_ref, win_ref, wg_ref, bg_ref)


def _gla_proj(x, gpre, win, wg, bg, tr):
    N, D = x.shape
    hk = GLA_HEADS * GLA_DK
    hv = GLA_HEADS * GLA_DV
    row = lambda w: pl.BlockSpec((tr, w), lambda i: (i, 0))
    return pl.pallas_call(
        _gla_proj_kernel, grid=(N // tr,),
        in_specs=[row(D), _full(gpre.shape), _full(win.shape), _full(wg.shape), _full(bg.shape)],
        out_specs=[row(hk), row(hk), row(hv), row(hv), row(hk)],
        out_shape=[jax.ShapeDtypeStruct((N, w), d)
                   for w, d in ((hk, F32), (hk, F32), (hv, BF16), (hv, F32), (hk, F32))],
        compiler_params=_cparams("arbitrary"), name="gla_proj",
    )(x, gpre, win, wg, bg)


def _gla_gate(o, r, gn):
    return _rms(o, gn) * (r / (1.0 + jnp.exp(-r)))


def _cumsum_rows(x, chunk):
    pos = lax.broadcasted_iota(jnp.int32, x.shape, 0) % chunk
    d = 1
    while d < chunk:
        x = x + jnp.where(pos >= d, pltpu.roll(x, d, axis=0), 0.0)
        d *= 2
    return x


def _gla_chunks(q, k, v, r, la, s_sc, gn):
    rows = q.shape[0]
    c_sz = GLA_CHUNK
    nc = rows // c_sz
    b = _cumsum_rows(la, c_sz)
    b_last = [b[(c + 1) * c_sz - 1:(c + 1) * c_sz, :] for c in range(nc)]
    q_in = (q * jnp.exp(b)).astype(BF16)
    k_in = (k * jnp.exp(-b)).astype(BF16)
    k_dec = jnp.concatenate([k[c * c_sz:(c + 1) * c_sz] * jnp.exp(b_last[c] - b[c * c_sz:(c + 1) * c_sz])
                             for c in range(nc)], axis=0).astype(BF16)
    dec = [jnp.exp(bl) for bl in b_last]
    ri = lax.broadcasted_iota(jnp.int32, (rows, rows), 0)
    ci = lax.broadcasted_iota(jnp.int32, (rows, rows), 1)
    tril = jnp.where(ci <= ri, ci, -1) >= (ri // c_sz) * c_sz
    outs = []
    for hd in range(GLA_HEADS):
        ks = slice(hd * GLA_DK, (hd + 1) * GLA_DK)
        vs = slice(hd * GLA_DV, (hd + 1) * GLA_DV)
        vh = v[:, vs]
        a = jnp.where(tril, _dot_nt(q_in[:, ks], k_in[:, ks]), 0.0).astype(BF16)
        o_intra = _dot(a, vh)
        kvs = [_dot_tn(k_dec[c * c_sz:(c + 1) * c_sz, ks], vh[c * c_sz:(c + 1) * c_sz]) for c in range(nc)]
        s = s_sc[hd]
        o_inter = []
        for c in range(nc):
            o_inter.append(_dot(q_in[c * c_sz:(c + 1) * c_sz, ks], s.astype(BF16)))
            dcol = jnp.transpose(jnp.broadcast_to(dec[c][:, ks], (GLA_DK, GLA_DK)))
            s = jnp.concatenate([dcol] * (GLA_DV // GLA_DK), axis=1) * s + kvs[c]
        s_sc[hd] = s
        outs.append(_gla_gate(o_intra + jnp.concatenate(o_inter, axis=0), r[:, vs], gn))
    return jnp.concatenate(outs, axis=1)


GLA_BLOCK_ROWS = 256


def _gla_prompt_kernel(x_ref, gpre_ref, win_ref, wg_ref, bg_ref, s0_ref, gn_ref, wo_ref, gpost_ref,
                       y_ref, sfin_ref, s_sc):
    @pl.when(pl.program_id(1) == 0)
    def _():
        s_sc[...] = s0_ref[...]

    blocks = [slice(r0, r0 + GLA_BLOCK_ROWS) for r0 in range(0, x_ref.shape[1], GLA_BLOCK_ROWS)]
    proj = [_gla_project(x_ref[0, sl], gpre_ref, win_ref, wg_ref, bg_ref) for sl in blocks]
    gn = gn_ref[...]
    for sl, (q, k, v, r, la) in zip(blocks, proj):
        o = _gla_chunks(q, k, v, r, la, s_sc, gn)
        m = _dot(o.astype(BF16), wo_ref[...])
        y_ref[0, sl] = x_ref[0, sl] + _rms(m, gpost_ref[...])

    @pl.when(pl.program_id(1) == pl.num_programs(1) - 1)
    def _():
        sfin_ref[0] = s_sc[...]


def _gla_prompt(x, gpre, win, wg, bg, s0, gn, wo, gpost, rows):
    B, T, D = x.shape
    blk = pl.BlockSpec((1, rows, D), lambda b, i: (b, i, 0))
    const = lambda a: pl.BlockSpec(a.shape, lambda b, i: (0,) * a.ndim, pipeline_mode=pl.Buffered(1))
    return pl.pallas_call(
        _gla_prompt_kernel, grid=(B, T // rows),
        in_specs=[blk] + [const(a) for a in (gpre, win, wg, bg, s0, gn, wo, gpost)],
        out_specs=[blk, pl.BlockSpec((1,) + s0.shape, lambda b, i: (b, 0, 0, 0))],
        out_shape=[jax.ShapeDtypeStruct((B, T, D), F32), jax.ShapeDtypeStruct((B,) + s0.shape, F32)],
        scratch_shapes=[pltpu.VMEM(s0.shape, F32)],
        compiler_params=_cparams("arbitrary", "arbitrary"), name="gla_prompt",
    )(x, gpre, win, wg, bg, s0, gn, wo, gpost)


GLA_SHORT_ROWS = 16


def _gla_tokens_kernel(q_ref, k_ref, la_ref, v_ref, s0_ref, o_ref, sfin_ref):
    nblk, rows, _ = q_ref.shape
    zk = jnp.zeros((LANES - rows, GLA_DK), BF16)
    zv = jnp.zeros((LANES - rows, GLA_DV), BF16)
    tril = (lax.broadcasted_iota(jnp.int32, (rows, LANES), 1) <= lax.broadcasted_iota(jnp.int32, (rows, LANES), 0))
    for n in range(nblk):
        b = _cumsum_rows(la_ref[n], rows)
        b_last = b[rows - 1:rows, :]
        k = k_ref[n]
        q_in = (q_ref[n] * jnp.exp(b)).astype(BF16)
        k_in = (k * jnp.exp(-b)).astype(BF16)
        k_dec = (k * jnp.exp(b_last - b)).astype(BF16)
        dec = jnp.exp(b_last)
        for hd in range(GLA_HEADS):
            ks = slice(hd * GLA_DK, (hd + 1) * GLA_DK)
            vs = slice(hd * GLA_DV, (hd + 1) * GLA_DV)
            v = jnp.concatenate([v_ref[n, :, vs].astype(BF16), zv], axis=0)
            a = jnp.where(tril, _dot_nt(q_in[:, ks], jnp.concatenate([k_in[:, ks], zk], axis=0)), 0.0)
            s = s0_ref[n, hd]
            o_ref[n, :, vs] = _dot(a.astype(BF16), v) + _dot(q_in[:, ks], s.astype(BF16))
            dcol = jnp.transpose(jnp.broadcast_to(dec[:, ks], (GLA_DK, GLA_DK)))
            sfin_ref[n, hd] = (jnp.concatenate([dcol] * (GLA_DV // GLA_DK), axis=1) * s
                               + _dot_tn(jnp.concatenate([k_dec[:, ks], zk], axis=0), v))


def _gla_tokens(q, k, la, v, s0, nblk):
    nseq, rows, hk = q.shape
    hv = v.shape[-1]
    col = pl.BlockSpec((nblk, rows, hk), lambda s: (s, 0, 0))
    val = pl.BlockSpec((nblk, rows, hv), lambda s: (s, 0, 0))
    st = pl.BlockSpec((nblk,) + s0.shape[1:], lambda s: (s, 0, 0, 0))
    return pl.pallas_call(
        _gla_tokens_kernel, grid=(nseq // nblk,),
        in_specs=[col, col, col, val, st], out_specs=[val, st],
        out_shape=[jax.ShapeDtypeStruct((nseq, rows, hv), F32), jax.ShapeDtypeStruct(s0.shape, F32)],
        compiler_params=_cparams("arbitrary"), name="gla_tokens",
    )(q, k, la, v, s0)


def _gla_out_kernel(o_ref, r_ref, x_ref, gn_ref, wo_ref, gpost_ref, y_ref, ocat_sc):
    gn = gn_ref[...]
    for hd in range(GLA_HEADS):
        vs = slice(hd * GLA_DV, (hd + 1) * GLA_DV)
        ocat_sc[:, vs] = _gla_gate(o_ref[:, vs], r_ref[:, vs], gn)
    m = _dot(ocat_sc[...].astype(BF16), wo_ref[...])
    y_ref[...] = x_ref[...] + _rms(m, gpost_ref[...])


def _gla_out_small(o, r, x, gn, wo, gpost):
    R, D = x.shape
    return pl.pallas_call(
        _gla_out_kernel, grid=(1,),
        in_specs=[_full(o.shape), _full(r.shape), _full(x.shape), _full(gn.shape), _full(wo.shape),
                  _full(gpost.shape)],
        out_specs=_full((R, D)), out_shape=jax.ShapeDtypeStruct((R, D), F32),
        scratch_shapes=[pltpu.VMEM(o.shape, F32)],
        compiler_params=_cparams("arbitrary"), name="gla_out_small",
    )(o, r, x, gn, wo, gpost)


def _rope_cos_sin(pos):
    half = QK_ROPE // 2
    inv = ROPE_BASE ** (-np.arange(half, dtype=np.float64) / half)
    ang = np.asarray(pos, np.float64)[:, None] * inv[None, :]
    return np.cos(ang), np.sin(ang)


def _rope_tables(pos):
    c, s = _rope_cos_sin(pos)
    reps = LANES // QK_ROPE
    return (jnp.asarray(np.tile(np.concatenate([c, c], axis=1), (1, reps)), F32),
            jnp.asarray(np.tile(np.concatenate([-s, s], axis=1), (1, reps)), F32))


def _rope_tables_head(pos):
    c, s = _rope_cos_sin(pos)
    z = lambda n: np.zeros((len(pos), n))
    tail = HEAD_W - QK_NOPE - QK_ROPE
    return (jnp.asarray(np.concatenate([z(QK_NOPE), c, c, z(tail)], axis=1), F32),
            jnp.asarray(np.concatenate([z(QK_NOPE), -s, s, z(tail)], axis=1), F32))


def _swap_halves(w):
    half = QK_ROPE // 2
    return jnp.concatenate([w[..., half:], w[..., :half]], axis=-1)


def kernel(x_prompt, x_sample, cache_ckv, cache_kpe, state_gla, page_table, meta_tokens, norm_pre_mix, norm_post_mix, norm_pre_mlp, norm_post_mlp, mla_w_in, mla_q_norm, mla_w_uq, mla_kv_norm, mla_w_uk, mla_w_uv, mla_w_o, gla_w_in, gla_w_gate, gla_b_gate, gla_norm, gla_w_o, mlp_w_up, mlp_w_down):
    B, T, D = x_prompt.shape
    DB, S, _ = x_sample.shape
    H = MLA_HEADS
    n_s = DB * S
    past_len = page_table.shape[1] * cache_ckv.shape[2]
    row = lambda a: a.reshape(1, -1)

    w_in = mla_w_in[0]
    o = Q_LORA + KV_LORA
    w_kpe = w_in[:, o:]
    zpad = jnp.zeros((D, LANES - QK_ROPE), F32)
    w_in_p = jnp.concatenate([w_in[:, :o], w_kpe, zpad, _swap_halves(w_kpe), zpad], axis=1).astype(BF16)
    w_uq = mla_w_uq[0].reshape(Q_LORA, H, QK_NOPE + QK_ROPE)
    w_qn = jnp.pad(w_uq[:, :, :QK_NOPE], ((0, 0), (0, 0), (0, LANES - QK_NOPE))).reshape(Q_LORA, H * LANES)
    w_ukt = jnp.pad(jnp.transpose(mla_w_uk[0], (1, 2, 0)), ((0, 0), (0, LANES - QK_NOPE), (0, 0)))
    w_qpe = w_uq[:, :, QK_NOPE:]
    w_qp = w_qpe.reshape(Q_LORA, H * QK_ROPE).astype(BF16)
    w_qps = _swap_halves(w_qpe).reshape(Q_LORA, H * QK_ROPE).astype(BF16)
    proj_w = (row(norm_pre_mix[0]), w_in_p, row(mla_q_norm[0]), w_qn.astype(BF16), w_ukt.astype(BF16),
              w_qp, w_qps, row(mla_kv_norm[0]))
    w_uv = jnp.transpose(mla_w_uv[0], (1, 0, 2)).astype(BF16)
    w_o = mla_w_o[0].astype(BF16)
    g_post0 = row(norm_post_mix[0])

    x_small = jnp.concatenate([x_sample.reshape(n_s, D), meta_tokens], axis=0)
    pos_small = np.concatenate([past_len + np.tile(np.arange(S), DB), np.arange(N_META)])
    cos_s, sin_s = _rope_tables(pos_small)
    qlat_s, qpe_s, ckv_s, kpe_s, klat_s, kpeb_s = _mla_proj_small(x_small, cos_s, sin_s, proj_w)

    npad = LANES
    kml = jnp.pad(klat_s[n_s:], ((0, npad - N_META), (0, 0)))
    kmp = jnp.pad(kpeb_s[n_s:], ((0, npad - N_META), (0, 0)))
    o_meta = _meta_attn(qlat_s[n_s:].reshape(N_META * H, KV_LORA), qpe_s[n_s:].reshape(N_META * H, QK_ROPE),
                        kml, kmp)
    knl = jnp.pad(klat_s[:n_s].reshape(DB, S, KV_LORA), ((0, 0), (0, npad - S), (0, 0)))
    knp = jnp.pad(kpeb_s[:n_s].reshape(DB, S, QK_ROPE), ((0, 0), (0, npad - S), (0, 0)))
    o_samp = _decode_attn(page_table, qlat_s[:n_s].reshape(n_s * H, KV_LORA),
                          qpe_s[:n_s].reshape(n_s * H, QK_ROPE), cache_ckv, jnp.swapaxes(cache_kpe, 2, 3),
                          0, knl, knp)
    o_small = jnp.concatenate([o_samp.reshape(n_s, H * KV_LORA), o_meta.reshape(N_META, H * KV_LORA)], axis=0)
    xs1 = _mla_out_small(o_small, x_small, w_uv, w_o, g_post0)

    zl = lambda n: jnp.zeros((D, n), F32)
    pad_r = HEAD_W - QK_NOPE - QK_ROPE
    w_in_m = jnp.concatenate([w_in[:, :o], zl(QK_NOPE), w_kpe, zl(pad_r),
                              zl(QK_NOPE), _swap_halves(w_kpe), zl(pad_r)], axis=1).astype(BF16)
    wq = jnp.pad(w_uq, ((0, 0), (0, 0), (0, pad_r))).reshape(Q_LORA, H * HEAD_W).astype(BF16)
    wqs = jnp.pad(_swap_halves(w_qpe), ((0, 0), (0, 0), (QK_NOPE, pad_r))).reshape(Q_LORA, H * HEAD_W).astype(BF16)
    wk = jnp.pad(mla_w_uk[0], ((0, 0), (0, 0), (0, HEAD_W - QK_NOPE))).reshape(KV_LORA, H * HEAD_W).astype(BF16)
    wv_pair = mla_w_uv[0].reshape(KV_LORA, H // 2, 2, V_HEAD)
    wv = jnp.stack([jnp.pad(wv_pair[:, :, r], ((0, 0), (0, 0), (_val_lane0(r), HEAD_W - V_HEAD - _val_lane0(r))))
                    for r in range(2)], axis=2).reshape(KV_LORA, H * HEAD_W).astype(BF16)
    mha_w = (row(norm_pre_mix[0]), w_in_m, row(mla_q_norm[0]), wq, wqs, row(mla_kv_norm[0]), wk, wv)
    cosk, sink = _rope_tables_head(N_META + np.arange(T))
    q_p, k_p, v_p, ckv_p, kpe_p = _mha_proj(x_prompt, cosk, sink, mha_w, tr=ROW_TILE)
    cosm, sinm = _rope_tables_head(np.arange(N_META))
    _, k_m, v_m, _, _ = _mha_proj(meta_tokens[None], cosm, sinm, mha_w, tr=N_META)
    mpad = ((0, 0), (0, 0), (0, LANES - N_META), (0, 0))
    attn_p = _mha_flash(q_p, k_p, v_p, jnp.pad(k_m, mpad), jnp.pad(v_m, mpad), tq=ROW_TILE)

    wup_b, wdn_b = mlp_w_up.astype(BF16), mlp_w_down.astype(BF16)
    mlp0 = (row(norm_pre_mlp[0]), wup_b, wdn_b, row(norm_post_mlp[0]), 0)
    xp2 = _mlp(x_prompt.reshape(B * T, D), *mlp0, tm=ROW_TILE, attn=(attn_p.reshape(B * T, D), w_o, g_post0))
    xs2 = _mlp(xs1, *mlp0, tm=xs1.shape[0])

    hk = GLA_HEADS * GLA_DK
    hv = GLA_HEADS * GLA_DV
    gw = gla_w_in[0]
    gw_p = jnp.concatenate([gw, jnp.zeros((D, LANES - GATE_RANK), F32)], axis=1).astype(BF16)
    wg_p = jnp.concatenate([gla_w_gate[0], jnp.zeros((LANES - GATE_RANK, hk), F32)], axis=0).astype(BF16)
    gla_pw = (row(norm_pre_mix[1]), gw_p, wg_p, row(gla_b_gate[0]))
    gn = row(gla_norm[0])
    gwo = gla_w_o[0].astype(BF16)
    g_post1 = row(norm_post_mix[1])

    q_s, k_s, v_s, r_s, la_s = _gla_proj(xs2, *gla_pw, tr=xs2.shape[0])
    def seqs(a, n, l):
        return jnp.pad(a.reshape(n, l, a.shape[-1]), ((0, 0), (0, GLA_SHORT_ROWS - l), (0, 0)))

    o_gs, s_gs = _gla_tokens(*(seqs(a[:n_s], DB, S) for a in (q_s, k_s, la_s, v_s)), state_gla[0], nblk=4)
    zero_state = jnp.zeros((1, GLA_HEADS, GLA_DK, GLA_DV), F32)
    o_gm, s_gm = _gla_tokens(*(seqs(a[n_s:], 1, N_META) for a in (q_s, k_s, la_s, v_s)), zero_state, nblk=1)
    o_gsmall = jnp.concatenate([o_gs[:, :S].reshape(n_s, hv), o_gm[:, :N_META].reshape(N_META, hv)], axis=0)
    xs3 = _gla_out_small(o_gsmall, r_s, xs2, gn, gwo, g_post1)

    xp3, s_gp = _gla_prompt(xp2.reshape(B, T, D), *gla_pw, s_gm[0], gn, gwo, g_post1, rows=ROW_TILE)

    mlp1 = (row(norm_pre_mlp[1]), wup_b, wdn_b, row(norm_post_mlp[1]), 1)
    y_prompt = _mlp(xp3.reshape(B * T, D), *mlp1, tm=ROW_TILE).reshape(B, T, D)
    xs4 = _mlp(xs3, *mlp1, tm=xs3.shape[0])
    y_sample = xs4[:n_s].reshape(DB, S, D)

    bmeta = lambda a: jnp.broadcast_to(a[None], (B,) + a.shape)
    new_ckv_prompt = jnp.concatenate([bmeta(ckv_s[n_s:]), ckv_p], axis=1)[None]
    new_kpe_prompt = jnp.concatenate([bmeta(kpe_s[n_s:]), kpe_p], axis=1)[None]
    new_ckv_sample = ckv_s[:n_s].reshape(1, DB, S, KV_LORA)
    new_kpe_sample = kpe_s[:n_s].reshape(1, DB, S, QK_ROPE)
    return (y_prompt, y_sample, new_ckv_prompt, new_kpe_prompt, new_ckv_sample, new_kpe_sample,
            s_gp[None], s_gs[None])
```

```python
import functools

import jax
import jax.numpy as jnp
import numpy as np
from jax import lax
from jax.experimental import pallas as pl
from jax.experimental.pallas import tpu as pltpu

F32 = jnp.float32
BF16 = jnp.bfloat16

N_META = 16
MLA_HEADS = 16
Q_LORA = 256
KV_LORA = 256
QK_NOPE = 64
QK_ROPE = 32
V_HEAD = 64
MLA_SCALE = (QK_NOPE + QK_ROPE) ** -0.5
ROPE_BASE = 10000.0
GLA_HEADS = 4
GLA_DK = 128
GLA_DV = 256
GLA_SCALE = GLA_DK ** -0.5
GATE_RANK = 16
GATE_TAU = 16.0
GLA_CHUNK = 64
EPS = 1e-6
LOG2E = 1.4426950408889634

LANES = 128
VMEM_LIMIT = 56 * 1024 * 1024
ROW_TILE = 512


def _cparams(*sem):
    return pltpu.CompilerParams(dimension_semantics=sem, vmem_limit_bytes=VMEM_LIMIT)


def _rms(x, w):
    return x * lax.rsqrt(jnp.mean(x * x, axis=-1, keepdims=True) + EPS) * w


def _dot(a, b):
    return jnp.dot(a, b, preferred_element_type=F32)


def _dot_nt(a, b):
    return lax.dot_general(a, b, (((1,), (1,)), ((), ())), preferred_element_type=F32)


def _dot_tn(a, b):
    return lax.dot_general(a, b, (((0,), (0,)), ((), ())), preferred_element_type=F32)


def _full(shape):
    n = len(shape)
    return pl.BlockSpec(shape, lambda *_: (0,) * n)


def _mla_proj_kernel(x_ref, cos_ref, sin_ref, gpre_ref, win_ref, qn_ref, wqn_ref, wukt_ref, wqp_ref, wqps_ref,
                     kvn_ref, qlat_ref, qpe_ref, ckv_ref, kpe_ref, klat_ref, kpeb_ref):
    h = _rms(x_ref[...], gpre_ref[...]).astype(BF16)
    a = _dot(h, win_ref[...])
    cqn = _rms(a[:, :Q_LORA], qn_ref[...]).astype(BF16)
    ckv = _rms(a[:, Q_LORA:Q_LORA + KV_LORA], kvn_ref[...])
    cos = cos_ref[...]
    sin = sin_ref[...]
    o = Q_LORA + KV_LORA
    kpe = a[:, o:o + QK_ROPE] * cos[:, :QK_ROPE] + a[:, o + LANES:o + LANES + QK_ROPE] * sin[:, :QK_ROPE]
    q_nope = _dot(cqn, wqn_ref[...])
    qp = _dot(cqn, wqp_ref[...])
    qps = _dot(cqn, wqps_ref[...])
    nl = MLA_HEADS * QK_ROPE // LANES
    qpe = jnp.concatenate(
        [(qp[:, j * LANES:(j + 1) * LANES] * cos + qps[:, j * LANES:(j + 1) * LANES] * sin) * MLA_SCALE
         for j in range(nl)], axis=1)
    for hd in range(MLA_HEADS):
        ql = _dot(q_nope[:, hd * LANES:(hd + 1) * LANES].astype(BF16), wukt_ref[hd])
        qlat_ref[:, hd * KV_LORA:(hd + 1) * KV_LORA] = (ql * MLA_SCALE).astype(BF16)
    qpe_ref[...] = qpe.astype(BF16)
    ckv_ref[...] = ckv
    kpe_ref[...] = kpe
    klat_ref[...] = ckv.astype(BF16)
    kpeb_ref[...] = kpe.astype(BF16)


def _mla_proj_small(x, cos, sin, w):
    R, D = x.shape
    H = MLA_HEADS
    in_specs = [_full((R, D)), _full((R, LANES)), _full((R, LANES))] + [_full(a.shape) for a in w]
    shapes = [((R, H * KV_LORA), BF16), ((R, H * QK_ROPE), BF16), ((R, KV_LORA), F32),
              ((R, QK_ROPE), F32), ((R, KV_LORA), BF16), ((R, QK_ROPE), BF16)]
    return pl.pallas_call(
        _mla_proj_kernel, grid=(1,), in_specs=in_specs,
        out_specs=[_full(s) for s, _ in shapes],
        out_shape=[jax.ShapeDtypeStruct(s, d) for s, d in shapes],
        compiler_params=_cparams("arbitrary"), name="mla_proj_small",
    )(x, cos, sin, *w)


HEAD_W = LANES


def _val_lane0(hd):
    return (hd % 2) * V_HEAD


def _sum_lane(hd):
    return (1 - hd % 2) * V_HEAD


def _mha_proj_kernel(x_ref, cosk_ref, sink_ref, gpre_ref, win_ref, qn_ref, wq_ref, wqs_ref, kvn_ref,
                     wk_ref, wv_ref, q_ref, k_ref, v_ref, ckv_ref, kpe_ref):
    h = _rms(x_ref[0], gpre_ref[...]).astype(BF16)
    a = _dot(h, win_ref[...])
    cqn = _rms(a[:, :Q_LORA], qn_ref[...]).astype(BF16)
    ckv = _rms(a[:, Q_LORA:Q_LORA + KV_LORA], kvn_ref[...])
    cosk = cosk_ref[...]
    sink = sink_ref[...]
    o = Q_LORA + KV_LORA
    kpe = a[:, o:o + HEAD_W] * cosk + a[:, o + HEAD_W:o + 2 * HEAD_W] * sink
    ckv_ref[0] = ckv
    kpe_ref[0] = kpe[:, QK_NOPE:QK_NOPE + QK_ROPE]
    ckv_b = ckv.astype(BF16)
    k_all = _dot(ckv_b, wk_ref[...])
    v_all = _dot(ckv_b, wv_ref[...])
    q_raw = _dot(cqn, wq_ref[...])
    q_swp = _dot(cqn, wqs_ref[...])
    lane = lax.broadcasted_iota(jnp.int32, (1, HEAD_W), 1)
    cosq = cosk + jnp.where(lane < QK_NOPE, 1.0, 0.0)
    for hd in range(MLA_HEADS):
        sl = slice(hd * HEAD_W, (hd + 1) * HEAD_W)
        k_ref[0, hd] = (k_all[:, sl] + kpe).astype(BF16)
        q_ref[0, hd] = ((q_raw[:, sl] * cosq + q_swp[:, sl] * sink) * (MLA_SCALE * LOG2E)).astype(BF16)
        v_ref[0, hd] = (v_all[:, sl] + jnp.where(lane == _sum_lane(hd), 1.0, 0.0)).astype(BF16)


def _mha_proj(x, cosk, sink, w, tr):
    B, T, D = x.shape
    H = MLA_HEADS
    in_specs = [pl.BlockSpec((1, tr, D), lambda b, i: (b, i, 0)),
                pl.BlockSpec((tr, HEAD_W), lambda b, i: (i, 0)),
                pl.BlockSpec((tr, HEAD_W), lambda b, i: (i, 0))] + [_full(a.shape) for a in w]
    out_shape = [jax.ShapeDtypeStruct((B, H, T, HEAD_W), BF16),
                 jax.ShapeDtypeStruct((B, H, T, HEAD_W), BF16),
                 jax.ShapeDtypeStruct((B, H, T, HEAD_W), BF16),
                 jax.ShapeDtypeStruct((B, T, KV_LORA), F32),
                 jax.ShapeDtypeStruct((B, T, QK_ROPE), F32)]
    out_specs = [pl.BlockSpec((1, H, tr, HEAD_W), lambda b, i: (b, 0, i, 0)),
                 pl.BlockSpec((1, H, tr, HEAD_W), lambda b, i: (b, 0, i, 0)),
                 pl.BlockSpec((1, H, tr, HEAD_W), lambda b, i: (b, 0, i, 0)),
                 pl.BlockSpec((1, tr, KV_LORA), lambda b, i: (b, i, 0)),
                 pl.BlockSpec((1, tr, QK_ROPE), lambda b, i: (b, i, 0))]
    return pl.pallas_call(
        _mha_proj_kernel, grid=(B, T // tr), in_specs=in_specs, out_specs=out_specs, out_shape=out_shape,
        compiler_params=_cparams("arbitrary", "arbitrary"), name="mha_proj",
    )(x, cosk, sink, *w)


def _mha_flash_kernel(q_ref, k_ref, v_ref, km_ref, vm_ref, o_ref, m_sc, acc_sc, *, tq):
    i = pl.program_id(2)
    nh = q_ref.shape[1]

    def scores(hh, rows, kb, mask):
        s = _dot_nt(q_ref[0, hh, rows], kb)
        return s if mask is None else jnp.where(mask, s, -jnp.inf)

    def apply(hh, rows, s, vb, first):
        parts = [s[:, c * LANES:(c + 1) * LANES] for c in range(s.shape[1] // LANES)]
        mrow = jnp.max(functools.reduce(jnp.maximum, parts), axis=-1, keepdims=True)
        if first:
            m_new = jnp.broadcast_to(mrow, (s.shape[0], LANES))
        else:
            m_old = m_sc[hh, rows]
            m_new = jnp.maximum(m_old, mrow)
        p = jnp.concatenate([jnp.exp2(part - m_new) for part in parts], axis=1)
        pv = _dot(p.astype(BF16), vb)
        if first:
            acc_sc[hh, rows] = pv
        else:
            acc_sc[hh, rows] = jnp.exp2(m_old - m_new) * acc_sc[hh, rows] + pv
        m_sc[hh, rows] = m_new

    nm = km_ref.shape[2]

    def step(pieces, first):
        ss = [[scores(hh, rows, kb(hh), mask) for rows, kb, _, mask in pieces] for hh in range(nh)]
        for hh in range(nh):
            for (rows, _, vb, _), s in zip(pieces, ss[hh]):
                apply(hh, rows, s, vb(hh), first)

    off = pl.multiple_of(i * tq, tq)
    hq = tq // 2

    def diag_piece(first_row, nkeys):
        rows = slice(first_row, first_row + hq)
        col = lax.broadcasted_iota(jnp.int32, (1, nkeys + nm), 1)
        need = jnp.where(col < first_row, -1,
                         jnp.where(col < nkeys, col - first_row, jnp.where(col - nkeys < N_META, -1, tq)))
        mask = lax.broadcasted_iota(jnp.int32, (hq, nkeys + nm), 0) >= need
        kb = lambda hh: jnp.concatenate([k_ref[0, hh, pl.ds(off, nkeys), :], km_ref[0, hh]], axis=0)
        vb = lambda g: jnp.concatenate([v_ref[0, g, pl.ds(off, nkeys), :], vm_ref[0, g]], axis=0)
        return rows, kb, vb, mask

    step([diag_piece(0, hq), diag_piece(hq, tq)], True)

    def body(j, carry):
        offj = pl.multiple_of(j * tq, tq)
        step([(slice(None), lambda hh: k_ref[0, hh, pl.ds(offj, tq), :],
               lambda g: v_ref[0, g, pl.ds(offj, tq), :], None)], False)
        return carry

    lax.fori_loop(0, i, body, 0)

    lane = lax.broadcasted_iota(jnp.int32, (tq, LANES), 1)
    for g in range(nh // 2):
        pair = []
        for hh in (2 * g, 2 * g + 1):
            acc = acc_sc[hh]
            pair.append(acc / acc[:, _sum_lane(hh):_sum_lane(hh) + 1])
        o_ref[0, :, g * LANES:(g + 1) * LANES] = jnp.where(lane < V_HEAD, pair[0], pair[1]).astype(o_ref.dtype)


MHA_HEADS_PER_STEP = 8


def _mha_flash(q, k, v, km, vm, tq):
    B, H, T, W = q.shape
    nh = MHA_HEADS_PER_STEP
    heads = lambda n: pl.BlockSpec((1, nh, n, W), lambda b, p, i: (b, p, 0, 0))
    return pl.pallas_call(
        functools.partial(_mha_flash_kernel, tq=tq), grid=(B, H // nh, T // tq),
        in_specs=[pl.BlockSpec((1, nh, tq, W), lambda b, p, i: (b, p, i, 0)), heads(T), heads(T),
                  pl.BlockSpec((1, nh) + km.shape[2:], lambda b, p, i: (0, p, 0, 0)),
                  pl.BlockSpec((1, nh) + vm.shape[2:], lambda b, p, i: (0, p, 0, 0))],
        out_specs=pl.BlockSpec((1, tq, nh * V_HEAD), lambda b, p, i: (b, i, p)),
        out_shape=jax.ShapeDtypeStruct((B, T, H * V_HEAD), BF16),
        scratch_shapes=[pltpu.VMEM((nh, tq, LANES), F32), pltpu.VMEM((nh, tq, LANES), F32)],
        compiler_params=_cparams("arbitrary", "arbitrary", "arbitrary"), name="mha_flash",
    )(q, k, v, km, vm)


def _meta_attn_kernel(qlat_ref, qpe_ref, kl_ref, kp_ref, o_ref):
    kl = kl_ref[...]
    s = _dot_nt(qlat_ref[...], kl) + _dot_nt(qpe_ref[...], kp_ref[...])
    r, n = s.shape
    tok = lax.broadcasted_iota(jnp.int32, (r // MLA_HEADS, MLA_HEADS, n), 0).reshape(r, n)
    col = lax.broadcasted_iota(jnp.int32, (r, n), 1)
    s = jnp.where(col <= tok, s, -jnp.inf)
    p = jnp.exp(s - jnp.max(s, axis=-1, keepdims=True))
    l = jnp.sum(p, axis=-1, keepdims=True)
    o_ref[...] = _dot(p.astype(BF16), kl) / l


def _meta_attn(qlat, qpe, kl, kp):
    r = qlat.shape[0]
    return pl.pallas_call(
        _meta_attn_kernel, grid=(1,),
        in_specs=[_full(qlat.shape), _full(qpe.shape), _full(kl.shape), _full(kp.shape)],
        out_specs=_full((r, KV_LORA)), out_shape=jax.ShapeDtypeStruct((r, KV_LORA), F32),
        compiler_params=_cparams("arbitrary"), name="meta_attn",
    )(qlat, qpe, kl, kp)


def _decode_kernel(pt_ref, qlat_ref, qpe_ref, knl_ref, knp_ref, ckv_hbm, kpt_hbm, o_ref,
                   ckv_buf, kpt_buf, sem, m_sc, l_sc, acc_sc, *, layer, npages, gp, nbuf, n_new):
    s = pl.program_id(0)
    nseq = pl.num_programs(0)
    ngroups = npages // gp
    page = ckv_buf.shape[1] // gp
    ql = qlat_ref[...]
    qp = qpe_ref[...]
    rq = ql.shape[0]

    def group_copies(seq, g):
        slot = g % nbuf
        cps = []
        for p in range(gp):
            pid = pt_ref[seq * npages + g * gp + p]
            cps.append(pltpu.make_async_copy(ckv_hbm.at[layer, pid],
                                             ckv_buf.at[slot, pl.ds(p * page, page), :], sem.at[slot]))
            cps.append(pltpu.make_async_copy(kpt_hbm.at[layer, pid],
                                             kpt_buf.at[slot, :, pl.ds(p * page, page)], sem.at[slot]))
        return cps

    @pl.when(s == 0)
    def _():
        for g in range(nbuf - 1):
            for cp in group_copies(0, g):
                cp.start()

    m_sc[...] = jnp.full(m_sc.shape, -jnp.inf, F32)
    l_sc[...] = jnp.zeros(l_sc.shape, F32)
    acc_sc[...] = jnp.zeros(acc_sc.shape, F32)

    def accumulate(kl, s):
        parts = [s[:, c * LANES:(c + 1) * LANES] for c in range(s.shape[1] // LANES)]
        m_old = m_sc[...]
        m_new = jnp.maximum(m_old, jnp.max(functools.reduce(jnp.maximum, parts), axis=-1, keepdims=True))
        alpha = jnp.exp(m_old - m_new)
        ps = [jnp.exp(part - m_new) for part in parts]
        p = ps[0] if len(ps) == 1 else jnp.concatenate(ps, axis=1)
        l_sc[...] = alpha * l_sc[...] + functools.reduce(jnp.add, ps)
        acc_sc[...] = (jnp.concatenate([alpha] * (KV_LORA // LANES), axis=1) * acc_sc[...]
                       + _dot(p.astype(BF16), kl))
        m_sc[...] = m_new

    def scores(g):
        for cp in group_copies(s, g):
            cp.wait()
        slot = g % nbuf
        kl = ckv_buf[slot].astype(BF16)
        return kl, _dot_nt(ql, kl) + _dot(qp, kpt_buf[slot].astype(BF16))

    cur = scores(0)
    for g in range(ngroups):
        nxt = g + nbuf - 1
        if nxt < ngroups:
            for cp in group_copies(s, nxt):
                cp.start()
        else:
            @pl.when(s + 1 < nseq)
            def _():
                for cp in group_copies(s + 1, nxt - ngroups):
                    cp.start()
        ahead = scores(g + 1) if g + 1 < ngroups else None
        accumulate(*cur)
        cur = ahead

    kn = knl_ref[0]
    sn = _dot_nt(ql, kn) + _dot_nt(qp, knp_ref[0])
    n = sn.shape[1]
    tok = lax.broadcasted_iota(jnp.int32, (n_new, rq // n_new, n), 0).reshape(rq, n)
    col = lax.broadcasted_iota(jnp.int32, (rq, n), 1)
    accumulate(kn, jnp.where(col <= tok, sn, -jnp.inf))
    o_ref[...] = acc_sc[...] / jnp.sum(l_sc[...], axis=-1, keepdims=True)


DECODE_GROUP_PAGES = 16
DECODE_RING_SLOTS = 4


def _decode_attn(page_table, qlat, qpe, cache_ckv, cache_kpt, layer, knl, knp):
    nseq, npages = page_table.shape
    page = cache_ckv.shape[2]
    n_new = qlat.shape[0] // (nseq * MLA_HEADS)
    rq = n_new * MLA_HEADS
    gp, nbuf = DECODE_GROUP_PAGES, DECODE_RING_SLOTS
    assert npages % gp == 0 and (npages // gp) % nbuf == 0
    pt = page_table.reshape(-1)
    in_specs = [pl.BlockSpec((rq, KV_LORA), lambda s, pt_ref: (s, 0)),
                pl.BlockSpec((rq, QK_ROPE), lambda s, pt_ref: (s, 0)),
                pl.BlockSpec((1,) + knl.shape[1:], lambda s, pt_ref: (s, 0, 0)),
                pl.BlockSpec((1,) + knp.shape[1:], lambda s, pt_ref: (s, 0, 0)),
                pl.BlockSpec(memory_space=pl.ANY), pl.BlockSpec(memory_space=pl.ANY)]
    grid_spec = pltpu.PrefetchScalarGridSpec(
        num_scalar_prefetch=1, grid=(nseq,), in_specs=in_specs,
        out_specs=pl.BlockSpec((rq, KV_LORA), lambda s, pt_ref: (s, 0)),
        scratch_shapes=[pltpu.VMEM((nbuf, gp * page, KV_LORA), F32), pltpu.VMEM((nbuf, QK_ROPE, gp * page), F32),
                        pltpu.SemaphoreType.DMA((nbuf,)),
                        pltpu.VMEM((rq, LANES), F32), pltpu.VMEM((rq, LANES), F32),
                        pltpu.VMEM((rq, KV_LORA), F32)])
    return pl.pallas_call(
        functools.partial(_decode_kernel, layer=layer, npages=npages, gp=gp, nbuf=nbuf, n_new=n_new),
        grid_spec=grid_spec,
        out_shape=jax.ShapeDtypeStruct((nseq * rq, KV_LORA), F32),
        compiler_params=_cparams("arbitrary"), name="decode_attn",
    )(pt, qlat, qpe, knl, knp, cache_ckv, cache_kpt)


def _mla_out_kernel(oa_ref, ob_ref, x_ref, wuv_ref, wo_ref, gpost_ref, y_ref, ocat_sc):
    row0 = 0
    for o_ref in (oa_ref, ob_ref):
        n = o_ref.shape[0]
        for hd in range(MLA_HEADS):
            oh = _dot(o_ref[:, hd * KV_LORA:(hd + 1) * KV_LORA].astype(BF16), wuv_ref[hd])
            ocat_sc[row0:row0 + n, hd * V_HEAD:(hd + 1) * V_HEAD] = oh
        row0 += n
    m = _dot(ocat_sc[...].astype(BF16), wo_ref[...])
    y_ref[...] = x_ref[...] + _rms(m, gpost_ref[...])


def _mla_out_small(oa, ob, x, wuv, wo, gpost):
    R, D = x.shape
    args = (oa, ob, x, wuv, wo, gpost)
    return pl.pallas_call(
        _mla_out_kernel, grid=(1,), in_specs=[_full(a.shape) for a in args],
        out_specs=_full((R, D)), out_shape=jax.ShapeDtypeStruct((R, D), F32),
        scratch_shapes=[pltpu.VMEM((R, MLA_HEADS * V_HEAD), F32)],
        compiler_params=_cparams("arbitrary"), name="mla_out_small",
    )(*args)


def _mlp_kernel(*refs, fc, attn):
    if attn:
        a_ref, x_ref, wo_ref, gmix_ref, gpre_ref, wup_ref, wdn_ref, gpost_ref, o_ref = refs
        x = x_ref[...] + _rms(_dot(a_ref[...], wo_ref[...]), gmix_ref[...])
    else:
        x_ref, gpre_ref, wup_ref, wdn_ref, gpost_ref, o_ref = refs
        x = x_ref[...]
    h = _rms(x, gpre_ref[...]).astype(BF16)
    dff = wup_ref.shape[1]
    acc = None
    for c in range(dff // fc):
        u = _dot(h, wup_ref[:, c * fc:(c + 1) * fc])
        u = jnp.square(jnp.maximum(u, 0.0)).astype(BF16)
        d = _dot(u, wdn_ref[c * fc:(c + 1) * fc, :])
        acc = d if acc is None else acc + d
    o_ref[...] = x + _rms(acc, gpost_ref[...])


def _mlp(x, gpre, wup, wdn, gpost, layer, tm, fc=1024, attn=None):
    N, D = x.shape
    const = lambda a: pl.BlockSpec(a.shape, lambda i: (0, 0), pipeline_mode=pl.Buffered(1))
    stacked = lambda a: pl.BlockSpec((None,) + a.shape[1:], lambda i: (layer, 0, 0),
                                     pipeline_mode=pl.Buffered(1))
    rows = lambda: pl.BlockSpec((tm, D), lambda i: (i, 0))
    args = [x, gpre, wup, wdn, gpost]
    in_specs = [rows(), const(gpre), stacked(wup), stacked(wdn), const(gpost)]
    if attn is not None:
        a, wo, gmix = attn
        args = [a, x, wo, gmix] + args[1:]
        in_specs = [rows(), rows(), const(wo), const(gmix)] + in_specs[1:]
    return pl.pallas_call(
        functools.partial(_mlp_kernel, fc=fc, attn=attn is not None), grid=(N // tm,),
        in_specs=in_specs, out_specs=rows(),
        out_shape=jax.ShapeDtypeStruct((N, D), F32),
        compiler_params=_cparams("arbitrary"), name="mlp",
    )(*args)


def _gla_project(x, gpre_ref, win_ref, wg_ref, bg_ref):
    hk = GLA_HEADS * GLA_DK
    hv = GLA_HEADS * GLA_DV
    h = _rms(x, gpre_ref[...]).astype(BF16)
    a = _dot(h, win_ref[...])
    gd = a[:, 2 * hk + 2 * hv:].astype(BF16)
    z = _dot(gd, wg_ref[...]) + bg_ref[...]
    la = (jnp.minimum(z, 0.0) - jnp.log(1.0 + jnp.exp(-jnp.abs(z)))) * (1.0 / GATE_TAU)
    return (a[:, :hk] * GLA_SCALE, a[:, hk:2 * hk], a[:, 2 * hk:2 * hk + hv].astype(BF16),
            a[:, 2 * hk + hv:2 * hk + 2 * hv], la)


def _gla_proj_kernel(x_ref, gpre_ref, win_ref, wg_ref, bg_ref, q_ref, k_ref, v_ref, r_ref, la_ref):
    q_ref[...], k_ref[...], v_ref[...], r_ref[...], la_ref[...] = _gla_project(
        x_ref[...], gpre_ref, win_ref, wg_ref, bg_ref)


def _gla_proj(x, gpre, win, wg, bg, tr):
    N, D = x.shape
    hk = GLA_HEADS * GLA_DK
    hv = GLA_HEADS * GLA_DV
    row = lambda w: pl.BlockSpec((tr, w), lambda i: (i, 0))
    return pl.pallas_call(
        _gla_proj_kernel, grid=(N // tr,),
        in_specs=[row(D), _full(gpre.shape), _full(win.shape), _full(wg.shape), _full(bg.shape)],
        out_specs=[row(hk), row(hk), row(hv), row(hv), row(hk)],
        out_shape=[jax.ShapeDtypeStruct((N, w), d)
                   for w, d in ((hk, F32), (hk, F32), (hv, BF16), (hv, F32), (hk, F32))],
        compiler_params=_cparams("arbitrary"), name="gla_proj",
    )(x, gpre, win, wg, bg)


def _gla_gate(o, r, gn):
    return _rms(o, gn) * (r / (1.0 + jnp.exp(-r)))


def _cumsum_rows(x, chunk):
    pos = lax.broadcasted_iota(jnp.int32, x.shape, 0) % chunk
    d = 1
    while d < chunk:
        x = x + jnp.where(pos >= d, pltpu.roll(x, d, axis=0), 0.0)
        d *= 2
    return x


def _gla_chunks(q, k, v, r, la, s_sc, gn):
    rows = q.shape[0]
    c_sz = GLA_CHUNK
    nc = rows // c_sz
    b = _cumsum_rows(la, c_sz)
    b_last = [b[(c + 1) * c_sz - 1:(c + 1) * c_sz, :] for c in range(nc)]
    q_in = (q * jnp.exp(b)).astype(BF16)
    k_in = (k * jnp.exp(-b)).astype(BF16)
    k_dec = jnp.concatenate([k[c * c_sz:(c + 1) * c_sz] * jnp.exp(b_last[c] - b[c * c_sz:(c + 1) * c_sz])
                             for c in range(nc)], axis=0).astype(BF16)
    dec = [jnp.exp(bl) for bl in b_last]
    ri = lax.broadcasted_iota(jnp.int32, (rows, rows), 0)
    ci = lax.broadcasted_iota(jnp.int32, (rows, rows), 1)
    tril = jnp.where(ci <= ri, ci, -1) >= (ri // c_sz) * c_sz
    outs = []
    for hd in range(GLA_HEADS):
        ks = slice(hd * GLA_DK, (hd + 1) * GLA_DK)
        vs = slice(hd * GLA_DV, (hd + 1) * GLA_DV)
        vh = v[:, vs]
        a = jnp.where(tril, _dot_nt(q_in[:, ks], k_in[:, ks]), 0.0).astype(BF16)
        o_intra = _dot(a, vh)
        kvs = [_dot_tn(k_dec[c * c_sz:(c + 1) * c_sz, ks], vh[c * c_sz:(c + 1) * c_sz]) for c in range(nc)]
        s = s_sc[hd]
        o_inter = []
        for c in range(nc):
            o_inter.append(_dot(q_in[c * c_sz:(c + 1) * c_sz, ks], s.astype(BF16)))
            dcol = jnp.transpose(jnp.broadcast_to(dec[c][:, ks], (GLA_DK, GLA_DK)))
            s = jnp.concatenate([dcol] * (GLA_DV // GLA_DK), axis=1) * s + kvs[c]
        s_sc[hd] = s
        outs.append(_gla_gate(o_intra + jnp.concatenate(o_inter, axis=0), r[:, vs], gn))
    return jnp.concatenate(outs, axis=1)


GLA_BLOCK_ROWS = 256


def _gla_prompt_kernel(x_ref, gpre_ref, win_ref, wg_ref, bg_ref, s0_ref, gn_ref, wo_ref, gpost_ref,
                       y_ref, sfin_ref, s_sc):
    @pl.when(pl.program_id(1) == 0)
    def _():
        s_sc[...] = s0_ref[...]

    blocks = [slice(r0, r0 + GLA_BLOCK_ROWS) for r0 in range(0, x_ref.shape[1], GLA_BLOCK_ROWS)]
    proj = [_gla_project(x_ref[0, sl], gpre_ref, win_ref, wg_ref, bg_ref) for sl in blocks]
    gn = gn_ref[...]
    for sl, (q, k, v, r, la) in zip(blocks, proj):
        o = _gla_chunks(q, k, v, r, la, s_sc, gn)
        m = _dot(o.astype(BF16), wo_ref[...])
        y_ref[0, sl] = x_ref[0, sl] + _rms(m, gpost_ref[...])

    @pl.when(pl.program_id(1) == pl.num_programs(1) - 1)
    def _():
        sfin_ref[0] = s_sc[...]


def _gla_prompt(x, gpre, win, wg, bg, s0, gn, wo, gpost, rows):
    B, T, D = x.shape
    blk = pl.BlockSpec((1, rows, D), lambda b, i: (b, i, 0))
    const = lambda a: pl.BlockSpec(a.shape, lambda b, i: (0,) * a.ndim, pipeline_mode=pl.Buffered(1))
    return pl.pallas_call(
        _gla_prompt_kernel, grid=(B, T // rows),
        in_specs=[blk] + [const(a) for a in (gpre, win, wg, bg, s0, gn, wo, gpost)],
        out_specs=[blk, pl.BlockSpec((1,) + s0.shape, lambda b, i: (b, 0, 0, 0))],
        out_shape=[jax.ShapeDtypeStruct((B, T, D), F32), jax.ShapeDtypeStruct((B,) + s0.shape, F32)],
        scratch_shapes=[pltpu.VMEM(s0.shape, F32)],
        compiler_params=_cparams("arbitrary", "arbitrary"), name="gla_prompt",
    )(x, gpre, win, wg, bg, s0, gn, wo, gpost)


GLA_SHORT_ROWS = 16


def _gla_tokens_kernel(q_ref, k_ref, la_ref, v_ref, s0_ref, o_ref, sfin_ref):
    nblk, rows, _ = q_ref.shape
    zk = jnp.zeros((LANES - rows, GLA_DK), BF16)
    zv = jnp.zeros((LANES - rows, GLA_DV), BF16)
    tril = (lax.broadcasted_iota(jnp.int32, (rows, LANES), 1) <= lax.broadcasted_iota(jnp.int32, (rows, LANES), 0))
    for n in range(nblk):
        b = _cumsum_rows(la_ref[n], rows)
        b_last = b[rows - 1:rows, :]
        k = k_ref[n]
        q_in = (q_ref[n] * jnp.exp(b)).astype(BF16)
        k_in = (k * jnp.exp(-b)).astype(BF16)
        k_dec = (k * jnp.exp(b_last - b)).astype(BF16)
        dec = jnp.exp(b_last)
        for hd in range(GLA_HEADS):
            ks = slice(hd * GLA_DK, (hd + 1) * GLA_DK)
            vs = slice(hd * GLA_DV, (hd + 1) * GLA_DV)
            v = jnp.concatenate([v_ref[n, :, vs].astype(BF16), zv], axis=0)
            a = jnp.where(tril, _dot_nt(q_in[:, ks], jnp.concatenate([k_in[:, ks], zk], axis=0)), 0.0)
            s = s0_ref[n, hd]
            o_ref[n, :, vs] = _dot(a.astype(BF16), v) + _dot(q_in[:, ks], s.astype(BF16))
            dcol = jnp.transpose(jnp.broadcast_to(dec[:, ks], (GLA_DK, GLA_DK)))
            sfin_ref[n, hd] = (jnp.concatenate([dcol] * (GLA_DV // GLA_DK), axis=1) * s
                               + _dot_tn(jnp.concatenate([k_dec[:, ks], zk], axis=0), v))


def _gla_tokens(q, k, la, v, s0, nblk):
    nseq, rows, hk = q.shape
    hv = v.shape[-1]
    col = pl.BlockSpec((nblk, rows, hk), lambda s: (s, 0, 0))
    val = pl.BlockSpec((nblk, rows, hv), lambda s: (s, 0, 0))
    st = pl.BlockSpec((nblk,) + s0.shape[1:], lambda s: (s, 0, 0, 0))
    return pl.pallas_call(
        _gla_tokens_kernel, grid=(nseq // nblk,),
        in_specs=[col, col, col, val, st], out_specs=[val, st],
        out_shape=[jax.ShapeDtypeStruct((nseq, rows, hv), F32), jax.ShapeDtypeStruct(s0.shape, F32)],
        compiler_params=_cparams("arbitrary"), name="gla_tokens",
    )(q, k, la, v, s0)


def _gla_out_kernel(o_ref, r_ref, x_ref, gn_ref, wo_ref, gpost_ref, y_ref, ocat_sc):
    gn = gn_ref[...]
    for hd in range(GLA_HEADS):
        vs = slice(hd * GLA_DV, (hd + 1) * GLA_DV)
        ocat_sc[:, vs] = _gla_gate(o_ref[:, vs], r_ref[:, vs], gn)
    m = _dot(ocat_sc[...].astype(BF16), wo_ref[...])
    y_ref[...] = x_ref[...] + _rms(m, gpost_ref[...])


def _gla_out_small(o, r, x, gn, wo, gpost):
    R, D = x.shape
    return pl.pallas_call(
        _gla_out_kernel, grid=(1,),
        in_specs=[_full(o.shape), _full(r.shape), _full(x.shape), _full(gn.shape), _full(wo.shape),
                  _full(gpost.shape)],
        out_specs=_full((R, D)), out_shape=jax.ShapeDtypeStruct((R, D), F32),
        scratch_shapes=[pltpu.VMEM(o.shape, F32)],
        compiler_params=_cparams("arbitrary"), name="gla_out_small",
    )(o, r, x, gn, wo, gpost)


def _rope_cos_sin(pos):
    half = QK_ROPE // 2
    inv = ROPE_BASE ** (-np.arange(half, dtype=np.float64) / half)
    ang = np.asarray(pos, np.float64)[:, None] * inv[None, :]
    return np.cos(ang), np.sin(ang)


def _rope_tables(pos):
    c, s = _rope_cos_sin(pos)
    reps = LANES // QK_ROPE
    return (jnp.asarray(np.tile(np.concatenate([c, c], axis=1), (1, reps)), F32),
            jnp.asarray(np.tile(np.concatenate([-s, s], axis=1), (1, reps)), F32))


def _rope_tables_head(pos):
    c, s = _rope_cos_sin(pos)
    z = lambda n: np.zeros((len(pos), n))
    tail = HEAD_W - QK_NOPE - QK_ROPE
    return (jnp.asarray(np.concatenate([z(QK_NOPE), c, c, z(tail)], axis=1), F32),
            jnp.asarray(np.concatenate([z(QK_NOPE), -s, s, z(tail)], axis=1), F32))


def _swap_halves(w):
    half = QK_ROPE // 2
    return jnp.concatenate([w[..., half:], w[..., :half]], axis=-1)


def kernel(x_prompt, x_sample, cache_ckv, cache_kpe, state_gla, page_table, meta_tokens, norm_pre_mix, norm_post_mix, norm_pre_mlp, norm_post_mlp, mla_w_in, mla_q_norm, mla_w_uq, mla_kv_norm, mla_w_uk, mla_w_uv, mla_w_o, gla_w_in, gla_w_gate, gla_b_gate, gla_norm, gla_w_o, mlp_w_up, mlp_w_down):
    B, T, D = x_prompt.shape
    DB, S, _ = x_sample.shape
    H = MLA_HEADS
    n_s = DB * S
    past_len = page_table.shape[1] * cache_ckv.shape[2]
    row = lambda a: a.reshape(1, -1)

    w_in = mla_w_in[0]
    o = Q_LORA + KV_LORA
    w_kpe = w_in[:, o:]
    zpad = jnp.zeros((D, LANES - QK_ROPE), F32)
    w_in_p = jnp.concatenate([w_in[:, :o], w_kpe, zpad, _swap_halves(w_kpe), zpad], axis=1).astype(BF16)
    w_uq = mla_w_uq[0].reshape(Q_LORA, H, QK_NOPE + QK_ROPE)
    w_qn = jnp.pad(w_uq[:, :, :QK_NOPE], ((0, 0), (0, 0), (0, LANES - QK_NOPE))).reshape(Q_LORA, H * LANES)
    w_ukt = jnp.pad(jnp.transpose(mla_w_uk[0], (1, 2, 0)), ((0, 0), (0, LANES - QK_NOPE), (0, 0)))
    w_qpe = w_uq[:, :, QK_NOPE:]
    w_qp = w_qpe.reshape(Q_LORA, H * QK_ROPE).astype(BF16)
    w_qps = _swap_halves(w_qpe).reshape(Q_LORA, H * QK_ROPE).astype(BF16)
    proj_w = (row(norm_pre_mix[0]), w_in_p, row(mla_q_norm[0]), w_qn.astype(BF16), w_ukt.astype(BF16),
              w_qp, w_qps, row(mla_kv_norm[0]))
    w_uv = jnp.transpose(mla_w_uv[0], (1, 0, 2)).astype(BF16)
    w_o = mla_w_o[0].astype(BF16)
    g_post0 = row(norm_post_mix[0])

    x_small = jnp.concatenate([x_sample.reshape(n_s, D), meta_tokens], axis=0)
    pos_small = np.concatenate([past_len + np.tile(np.arange(S), DB), np.arange(N_META)])
    cos_s, sin_s = _rope_tables(pos_small)
    qlat_s, qpe_s, ckv_s, kpe_s, klat_s, kpeb_s = _mla_proj_small(x_small, cos_s, sin_s, proj_w)

    npad = LANES
    kml = jnp.pad(klat_s[n_s:], ((0, npad - N_META), (0, 0)))
    kmp = jnp.pad(kpeb_s[n_s:], ((0, npad - N_META), (0, 0)))
    o_meta = _meta_attn(qlat_s[n_s:].reshape(N_META * H, KV_LORA), qpe_s[n_s:].reshape(N_META * H, QK_ROPE),
                        kml, kmp)
    knl = jnp.pad(klat_s[:n_s].reshape(DB, S, KV_LORA), ((0, 0), (0, npad - S), (0, 0)))
    knp = jnp.pad(kpeb_s[:n_s].reshape(DB, S, QK_ROPE), ((0, 0), (0, npad - S), (0, 0)))
    o_samp = _decode_attn(page_table, qlat_s[:n_s].reshape(n_s * H, KV_LORA),
                          qpe_s[:n_s].reshape(n_s * H, QK_ROPE), cache_ckv, jnp.swapaxes(cache_kpe, 2, 3),
                          0, knl, knp)
    xs1 = _mla_out_small(o_samp.reshape(n_s, H * KV_LORA), o_meta.reshape(N_META, H * KV_LORA), x_small,
                         w_uv, w_o, g_post0)

    zl = lambda n: jnp.zeros((D, n), F32)
    pad_r = HEAD_W - QK_NOPE - QK_ROPE
    w_in_m = jnp.concatenate([w_in[:, :o], zl(QK_NOPE), w_kpe, zl(pad_r),
                              zl(QK_NOPE), _swap_halves(w_kpe), zl(pad_r)], axis=1).astype(BF16)
    wq = jnp.pad(w_uq, ((0, 0), (0, 0), (0, pad_r))).reshape(Q_LORA, H * HEAD_W).astype(BF16)
    wqs = jnp.pad(_swap_halves(w_qpe), ((0, 0), (0, 0), (QK_NOPE, pad_r))).reshape(Q_LORA, H * HEAD_W).astype(BF16)
    wk = jnp.pad(mla_w_uk[0], ((0, 0), (0, 0), (0, HEAD_W - QK_NOPE))).reshape(KV_LORA, H * HEAD_W).astype(BF16)
    wv_pair = mla_w_uv[0].reshape(KV_LORA, H // 2, 2, V_HEAD)
    wv = jnp.stack([jnp.pad(wv_pair[:, :, r], ((0, 0), (0, 0), (_val_lane0(r), HEAD_W - V_HEAD - _val_lane0(r))))
                    for r in range(2)], axis=2).reshape(KV_LORA, H * HEAD_W).astype(BF16)
    mha_w = (row(norm_pre_mix[0]), w_in_m, row(mla_q_norm[0]), wq, wqs, row(mla_kv_norm[0]), wk, wv)
    cosk, sink = _rope_tables_head(N_META + np.arange(T))
    q_p, k_p, v_p, ckv_p, kpe_p = _mha_proj(x_prompt, cosk, sink, mha_w, tr=ROW_TILE)
    cosm, sinm = _rope_tables_head(np.arange(N_META))
    _, k_m, v_m, _, _ = _mha_proj(meta_tokens[None], cosm, sinm, mha_w, tr=N_META)
    mpad = ((0, 0), (0, 0), (0, LANES - N_META), (0, 0))
    attn_p = _mha_flash(q_p, k_p, v_p, jnp.pad(k_m, mpad), jnp.pad(v_m, mpad), tq=ROW_TILE)

    wup_b, wdn_b = mlp_w_up.astype(BF16), mlp_w_down.astype(BF16)
    mlp0 = (row(norm_pre_mlp[0]), wup_b, wdn_b, row(norm_post_mlp[0]), 0)
    xp2 = _mlp(x_prompt.reshape(B * T, D), *mlp0, tm=ROW_TILE, attn=(attn_p.reshape(B * T, D), w_o, g_post0))
    xs2 = _mlp(xs1, *mlp0, tm=xs1.shape[0])

    hk = GLA_HEADS * GLA_DK
    hv = GLA_HEADS * GLA_DV
    gw = gla_w_in[0]
    gw_p = jnp.concatenate([gw, jnp.zeros((D, LANES - GATE_RANK), F32)], axis=1).astype(BF16)
    wg_p = jnp.concatenate([gla_w_gate[0], jnp.zeros((LANES - GATE_RANK, hk), F32)], axis=0).astype(BF16)
    gla_pw = (row(norm_pre_mix[1]), gw_p, wg_p, row(gla_b_gate[0]))
    gn = row(gla_norm[0])
    gwo = gla_w_o[0].astype(BF16)
    g_post1 = row(norm_post_mix[1])

    q_s, k_s, v_s, r_s, la_s = _gla_proj(xs2, *gla_pw, tr=xs2.shape[0])
    def seqs(a, n, l):
        return jnp.pad(a.reshape(n, l, a.shape[-1]), ((0, 0), (0, GLA_SHORT_ROWS - l), (0, 0)))

    o_gs, s_gs = _gla_tokens(*(seqs(a[:n_s], DB, S) for a in (q_s, k_s, la_s, v_s)), state_gla[0], nblk=8)
    zero_state = jnp.zeros((1, GLA_HEADS, GLA_DK, GLA_DV), F32)
    o_gm, s_gm = _gla_tokens(*(seqs(a[n_s:], 1, N_META) for a in (q_s, k_s, la_s, v_s)), zero_state, nblk=1)
    o_gsmall = jnp.concatenate([o_gs[:, :S].reshape(n_s, hv), o_gm[:, :N_META].reshape(N_META, hv)], axis=0)
    xs3 = _gla_out_small(o_gsmall, r_s, xs2, gn, gwo, g_post1)

    xp3, s_gp = _gla_prompt(xp2.reshape(B, T, D), *gla_pw, s_gm[0], gn, gwo, g_post1, rows=ROW_TILE)

    mlp1 = (row(norm_pre_mlp[1]), wup_b, wdn_b, row(norm_post_mlp[1]), 1)
    y_prompt = _mlp(xp3.reshape(B * T, D), *mlp1, tm=ROW_TILE).reshape(B, T, D)
    xs4 = _mlp(xs3, *mlp1, tm=xs3.shape[0])
    y_sample = xs4[:n_s].reshape(DB, S, D)

    bmeta = lambda a: jnp.broadcast_to(a[None], (B,) + a.shape)
    new_ckv_prompt = jnp.concatenate([bmeta(ckv_s[n_s:]), ckv_p], axis=1)[None]
    new_kpe_prompt = jnp.concatenate([bmeta(kpe_s[n_s:]), kpe_p], axis=1)[None]
    new_ckv_sample = ckv_s[:n_s].reshape(1, DB, S, KV_LORA)
    new_kpe_sample = kpe_s[:n_s].reshape(1, DB, S, QK_ROPE)
    return (y_prompt, y_sample, new_ckv_prompt, new_kpe_prompt, new_ckv_sample, new_kpe_sample,
            s_gp[None], s_gs[None])
```

```python
import functools

import jax
import jax.numpy as jnp
import numpy as np
from jax import lax
from jax.experimental import pallas as pl
from jax.experimental.pallas import tpu as pltpu

F32 = jnp.float32
BF16 = jnp.bfloat16

N_META = 16
MLA_HEADS = 16
Q_LORA = 256
KV_LORA = 256
QK_NOPE = 64
QK_ROPE = 32
V_HEAD = 64
MLA_SCALE = (QK_NOPE + QK_ROPE) ** -0.5
ROPE_BASE = 10000.0
GLA_HEADS = 4
GLA_DK = 128
GLA_DV = 256
GLA_SCALE = GLA_DK ** -0.5
GATE_RANK = 16
GATE_TAU = 16.0
GLA_CHUNK = 64
EPS = 1e-6
LOG2E = 1.4426950408889634

LANES = 128
VMEM_LIMIT = 56 * 1024 * 1024
ROW_TILE = 512


def _cparams(*sem):
    return pltpu.CompilerParams(dimension_semantics=sem, vmem_limit_bytes=VMEM_LIMIT)


def _rms(x, w):
    return x * lax.rsqrt(jnp.mean(x * x, axis=-1, keepdims=True) + EPS) * w


def _dot(a, b):
    return jnp.dot(a, b, preferred_element_type=F32)


def _dot_nt(a, b):
    return lax.dot_general(a, b, (((1,), (1,)), ((), ())), preferred_element_type=F32)


def _dot_tn(a, b):
    return lax.dot_general(a, b, (((0,), (0,)), ((), ())), preferred_element_type=F32)


def _full(shape):
    n = len(shape)
    return pl.BlockSpec(shape, lambda *_: (0,) * n)


def _mla_proj_kernel(x_ref, cos_ref, sin_ref, gpre_ref, win_ref, qn_ref, wqn_ref, wukt_ref, wqp_ref, wqps_ref,
                     kvn_ref, qlat_ref, qpe_ref, ckv_ref, kpe_ref, klat_ref, kpeb_ref):
    h = _rms(x_ref[...], gpre_ref[...]).astype(BF16)
    a = _dot(h, win_ref[...])
    cqn = _rms(a[:, :Q_LORA], qn_ref[...]).astype(BF16)
    ckv = _rms(a[:, Q_LORA:Q_LORA + KV_LORA], kvn_ref[...])
    cos = cos_ref[...]
    sin = sin_ref[...]
    o = Q_LORA + KV_LORA
    kpe = a[:, o:o + QK_ROPE] * cos[:, :QK_ROPE] + a[:, o + LANES:o + LANES + QK_ROPE] * sin[:, :QK_ROPE]
    q_nope = _dot(cqn, wqn_ref[...])
    qp = _dot(cqn, wqp_ref[...])
    qps = _dot(cqn, wqps_ref[...])
    nl = MLA_HEADS * QK_ROPE // LANES
    qpe = jnp.concatenate(
        [(qp[:, j * LANES:(j + 1) * LANES] * cos + qps[:, j * LANES:(j + 1) * LANES] * sin) * MLA_SCALE
         for j in range(nl)], axis=1)
    for hd in range(MLA_HEADS):
        ql = _dot(q_nope[:, hd * LANES:(hd + 1) * LANES].astype(BF16), wukt_ref[hd])
        qlat_ref[:, hd * KV_LORA:(hd + 1) * KV_LORA] = (ql * MLA_SCALE).astype(BF16)
    qpe_ref[...] = qpe.astype(BF16)
    ckv_ref[...] = ckv
    kpe_ref[...] = kpe
    klat_ref[...] = ckv.astype(BF16)
    kpeb_ref[...] = kpe.astype(BF16)


def _mla_proj_small(x, cos, sin, w):
    R, D = x.shape
    H = MLA_HEADS
    in_specs = [_full((R, D)), _full((R, LANES)), _full((R, LANES))] + [_full(a.shape) for a in w]
    shapes = [((R, H * KV_LORA), BF16), ((R, H * QK_ROPE), BF16), ((R, KV_LORA), F32),
              ((R, QK_ROPE), F32), ((R, KV_LORA), BF16), ((R, QK_ROPE), BF16)]
    return pl.pallas_call(
        _mla_proj_kernel, grid=(1,), in_specs=in_specs,
        out_specs=[_full(s) for s, _ in shapes],
        out_shape=[jax.ShapeDtypeStruct(s, d) for s, d in shapes],
        compiler_params=_cparams("arbitrary"), name="mla_proj_small",
    )(x, cos, sin, *w)


HEAD_W = LANES


def _val_lane0(hd):
    return (hd % 2) * V_HEAD


def _sum_lane(hd):
    return (1 - hd % 2) * V_HEAD


def _mha_proj_kernel(x_ref, cosk_ref, sink_ref, gpre_ref, win_ref, qn_ref, wq_ref, wqs_ref, kvn_ref,
                     wk_ref, wv_ref, q_ref, k_ref, v_ref, ckv_ref, kpe_ref):
    h = _rms(x_ref[0], gpre_ref[...]).astype(BF16)
    a = _dot(h, win_ref[...])
    cqn = _rms(a[:, :Q_LORA], qn_ref[...]).astype(BF16)
    ckv = _rms(a[:, Q_LORA:Q_LORA + KV_LORA], kvn_ref[...])
    cosk = cosk_ref[...]
    sink = sink_ref[...]
    o = Q_LORA + KV_LORA
    kpe = a[:, o:o + HEAD_W] * cosk + a[:, o + HEAD_W:o + 2 * HEAD_W] * sink
    ckv_ref[0] = ckv
    kpe_ref[0] = kpe[:, QK_NOPE:QK_NOPE + QK_ROPE]
    ckv_b = ckv.astype(BF16)
    k_all = _dot(ckv_b, wk_ref[...])
    v_all = _dot(ckv_b, wv_ref[...])
    q_raw = _dot(cqn, wq_ref[...])
    q_swp = _dot(cqn, wqs_ref[...])
    lane = lax.broadcasted_iota(jnp.int32, (1, HEAD_W), 1)
    cosq = cosk + jnp.where(lane < QK_NOPE, 1.0, 0.0)
    for hd in range(MLA_HEADS):
        sl = slice(hd * HEAD_W, (hd + 1) * HEAD_W)
        k_ref[0, hd] = (k_all[:, sl] + kpe).astype(BF16)
        q_ref[0, hd] = ((q_raw[:, sl] * cosq + q_swp[:, sl] * sink) * (MLA_SCALE * LOG2E)).astype(BF16)
        v_ref[0, hd] = (v_all[:, sl] + jnp.where(lane == _sum_lane(hd), 1.0, 0.0)).astype(BF16)


def _mha_proj(x, cosk, sink, w, tr):
    B, T, D = x.shape
    H = MLA_HEADS
    in_specs = [pl.BlockSpec((1, tr, D), lambda b, i: (b, i, 0)),
                pl.BlockSpec((tr, HEAD_W), lambda b, i: (i, 0)),
                pl.BlockSpec((tr, HEAD_W), lambda b, i: (i, 0))] + [_full(a.shape) for a in w]
    out_shape = [jax.ShapeDtypeStruct((B, H, T, HEAD_W), BF16),
                 jax.ShapeDtypeStruct((B, H, T, HEAD_W), BF16),
                 jax.ShapeDtypeStruct((B, H, T, HEAD_W), BF16),
                 jax.ShapeDtypeStruct((B, T, KV_LORA), F32),
                 jax.ShapeDtypeStruct((B, T, QK_ROPE), F32)]
    out_specs = [pl.BlockSpec((1, H, tr, HEAD_W), lambda b, i: (b, 0, i, 0)),
                 pl.BlockSpec((1, H, tr, HEAD_W), lambda b, i: (b, 0, i, 0)),
                 pl.BlockSpec((1, H, tr, HEAD_W), lambda b, i: (b, 0, i, 0)),
                 pl.BlockSpec((1, tr, KV_LORA), lambda b, i: (b, i, 0)),
                 pl.BlockSpec((1, tr, QK_ROPE), lambda b, i: (b, i, 0))]
    return pl.pallas_call(
        _mha_proj_kernel, grid=(B, T // tr), in_specs=in_specs, out_specs=out_specs, out_shape=out_shape,
        compiler_params=_cparams("arbitrary", "arbitrary"), name="mha_proj",
    )(x, cosk, sink, *w)


def _mha_flash_kernel(q_ref, k_ref, v_ref, km_ref, vm_ref, o_ref, m_sc, acc_sc, *, tq):
    i = pl.program_id(2)
    nh = q_ref.shape[1]

    def scores(hh, rows, kb, mask):
        s = _dot_nt(q_ref[0, hh, rows], kb)
        return s if mask is None else jnp.where(mask, s, -jnp.inf)

    def apply(hh, rows, s, vb, first):
        parts = [s[:, c * LANES:(c + 1) * LANES] for c in range(s.shape[1] // LANES)]
        mrow = jnp.max(functools.reduce(jnp.maximum, parts), axis=-1, keepdims=True)
        if first:
            m_new = jnp.broadcast_to(mrow, (s.shape[0], LANES))
        else:
            m_old = m_sc[hh, rows]
            m_new = jnp.maximum(m_old, mrow)
        p = jnp.concatenate([jnp.exp2(part - m_new) for part in parts], axis=1)
        pv = _dot(p.astype(BF16), vb)
        if first:
            acc_sc[hh, rows] = pv
        else:
            acc_sc[hh, rows] = jnp.exp2(m_old - m_new) * acc_sc[hh, rows] + pv
        m_sc[hh, rows] = m_new

    nm = km_ref.shape[2]

    def step(pieces, first):
        ss = [[scores(hh, rows, kb(hh), mask) for rows, kb, _, mask in pieces] for hh in range(nh)]
        for hh in range(nh):
            for (rows, _, vb, _), s in zip(pieces, ss[hh]):
                apply(hh, rows, s, vb(hh), first)

    off = pl.multiple_of(i * tq, tq)
    hq = tq // 2

    def diag_piece(first_row, nkeys):
        rows = slice(first_row, first_row + hq)
        col = lax.broadcasted_iota(jnp.int32, (1, nkeys + nm), 1)
        need = jnp.where(col < first_row, -1,
                         jnp.where(col < nkeys, col - first_row, jnp.where(col - nkeys < N_META, -1, tq)))
        mask = lax.broadcasted_iota(jnp.int32, (hq, nkeys + nm), 0) >= need
        kb = lambda hh: jnp.concatenate([k_ref[0, hh, pl.ds(off, nkeys), :], km_ref[0, hh]], axis=0)
        vb = lambda g: jnp.concatenate([v_ref[0, g, pl.ds(off, nkeys), :], vm_ref[0, g]], axis=0)
        return rows, kb, vb, mask

    step([diag_piece(0, hq), diag_piece(hq, tq)], True)

    def body(j, carry):
        offj = pl.multiple_of(j * tq, tq)
        step([(slice(None), lambda hh: k_ref[0, hh, pl.ds(offj, tq), :],
               lambda g: v_ref[0, g, pl.ds(offj, tq), :], None)], False)
        return carry

    lax.fori_loop(0, i, body, 0)

    lane = lax.broadcasted_iota(jnp.int32, (tq, LANES), 1)
    for g in range(nh // 2):
        pair = []
        for hh in (2 * g, 2 * g + 1):
            acc = acc_sc[hh]
            pair.append(acc / acc[:, _sum_lane(hh):_sum_lane(hh) + 1])
        o_ref[0, :, g * LANES:(g + 1) * LANES] = jnp.where(lane < V_HEAD, pair[0], pair[1]).astype(o_ref.dtype)


MHA_HEADS_PER_STEP = 8


def _mha_flash(q, k, v, km, vm, tq):
    B, H, T, W = q.shape
    nh = MHA_HEADS_PER_STEP
    heads = lambda n: pl.BlockSpec((1, nh, n, W), lambda b, p, i: (b, p, 0, 0))
    return pl.pallas_call(
        functools.partial(_mha_flash_kernel, tq=tq), grid=(B, H // nh, T // tq),
        in_specs=[pl.BlockSpec((1, nh, tq, W), lambda b, p, i: (b, p, i, 0)), heads(T), heads(T),
                  pl.BlockSpec((1, nh) + km.shape[2:], lambda b, p, i: (0, p, 0, 0)),
                  pl.BlockSpec((1, nh) + vm.shape[2:], lambda b, p, i: (0, p, 0, 0))],
        out_specs=pl.BlockSpec((1, tq, nh * V_HEAD), lambda b, p, i: (b, i, p)),
        out_shape=jax.ShapeDtypeStruct((B, T, H * V_HEAD), BF16),
        scratch_shapes=[pltpu.VMEM((nh, tq, LANES), F32), pltpu.VMEM((nh, tq, LANES), F32)],
        compiler_params=_cparams("arbitrary", "arbitrary", "arbitrary"), name="mha_flash",
    )(q, k, v, km, vm)


def _meta_attn_kernel(qlat_ref, qpe_ref, kl_ref, kp_ref, o_ref):
    kl = kl_ref[...]
    s = _dot_nt(qlat_ref[...], kl) + _dot_nt(qpe_ref[...], kp_ref[...])
    r, n = s.shape
    tok = lax.broadcasted_iota(jnp.int32, (r // MLA_HEADS, MLA_HEADS, n), 0).reshape(r, n)
    col = lax.broadcasted_iota(jnp.int32, (r, n), 1)
    s = jnp.where(col <= tok, s, -jnp.inf)
    p = jnp.exp(s - jnp.max(s, axis=-1, keepdims=True))
    l = jnp.sum(p, axis=-1, keepdims=True)
    o_ref[...] = _dot(p.astype(BF16), kl) / l


def _meta_attn(qlat, qpe, kl, kp):
    r = qlat.shape[0]
    return pl.pallas_call(
        _meta_attn_kernel, grid=(1,),
        in_specs=[_full(qlat.shape), _full(qpe.shape), _full(kl.shape), _full(kp.shape)],
        out_specs=_full((r, KV_LORA)), out_shape=jax.ShapeDtypeStruct((r, KV_LORA), F32),
        compiler_params=_cparams("arbitrary"), name="meta_attn",
    )(qlat, qpe, kl, kp)


def _decode_kernel(pt_ref, qlat_ref, qpe_ref, knl_ref, knp_ref, ckv_hbm, kpt_hbm, o_ref,
                   ckv_buf, kpt_buf, sem, m_sc, l_sc, acc_sc, *, layer, npages, gp, nbuf, n_new):
    s = pl.program_id(0)
    nseq = pl.num_programs(0)
    ngroups = npages // gp
    page = ckv_buf.shape[1] // gp
    ql = qlat_ref[...]
    qp = qpe_ref[...]
    rq = ql.shape[0]

    def group_copies(seq, g):
        slot = g % nbuf
        cps = []
        for p in range(gp):
            pid = pt_ref[seq * npages + g * gp + p]
            cps.append(pltpu.make_async_copy(ckv_hbm.at[layer, pid],
                                             ckv_buf.at[slot, pl.ds(p * page, page), :], sem.at[slot]))
            cps.append(pltpu.make_async_copy(kpt_hbm.at[layer, pid],
                                             kpt_buf.at[slot, :, pl.ds(p * page, page)], sem.at[slot]))
        return cps

    @pl.when(s == 0)
    def _():
        for g in range(nbuf - 1):
            for cp in group_copies(0, g):
                cp.start()

    m_sc[...] = jnp.full(m_sc.shape, -jnp.inf, F32)
    l_sc[...] = jnp.zeros(l_sc.shape, F32)
    acc_sc[...] = jnp.zeros(acc_sc.shape, F32)

    def accumulate(kl, s):
        parts = [s[:, c * LANES:(c + 1) * LANES] for c in range(s.shape[1] // LANES)]
        m_old = m_sc[...]
        m_new = jnp.maximum(m_old, jnp.max(functools.reduce(jnp.maximum, parts), axis=-1, keepdims=True))
        alpha = jnp.exp(m_old - m_new)
        ps = [jnp.exp(part - m_new) for part in parts]
        p = ps[0] if len(ps) == 1 else jnp.concatenate(ps, axis=1)
        l_sc[...] = alpha * l_sc[...] + functools.reduce(jnp.add, ps)
        acc_sc[...] = (jnp.concatenate([alpha] * (KV_LORA // LANES), axis=1) * acc_sc[...]
                       + _dot(p.astype(BF16), kl))
        m_sc[...] = m_new

    def scores(g):
        for cp in group_copies(s, g):
            cp.wait()
        slot = g % nbuf
        kl = ckv_buf[slot].astype(BF16)
        return kl, _dot_nt(ql, kl) + _dot(qp, kpt_buf[slot].astype(BF16))

    cur = scores(0)
    for g in range(ngroups):
        nxt = g + nbuf - 1
        if nxt < ngroups:
            for cp in group_copies(s, nxt):
                cp.start()
        else:
            @pl.when(s + 1 < nseq)
            def _():
                for cp in group_copies(s + 1, nxt - ngroups):
                    cp.start()
        ahead = scores(g + 1) if g + 1 < ngroups else None
        accumulate(*cur)
        cur = ahead

    kn = knl_ref[0]
    sn = _dot_nt(ql, kn) + _dot_nt(qp, knp_ref[0])
    n = sn.shape[1]
    tok = lax.broadcasted_iota(jnp.int32, (n_new, rq // n_new, n), 0).reshape(rq, n)
    col = lax.broadcasted_iota(jnp.int32, (rq, n), 1)
    accumulate(kn, jnp.where(col <= tok, sn, -jnp.inf))
    o_ref[...] = (acc_sc[...] / jnp.sum(l_sc[...], axis=-1, keepdims=True)).astype(o_ref.dtype)


DECODE_GROUP_PAGES = 16
DECODE_RING_SLOTS = 4


def _decode_attn(page_table, qlat, qpe, cache_ckv, cache_kpt, layer, knl, knp):
    nseq, npages = page_table.shape
    page = cache_ckv.shape[2]
    n_new = qlat.shape[0] // (nseq * MLA_HEADS)
    rq = n_new * MLA_HEADS
    gp, nbuf = DECODE_GROUP_PAGES, DECODE_RING_SLOTS
    assert npages % gp == 0 and (npages // gp) % nbuf == 0
    pt = page_table.reshape(-1)
    in_specs = [pl.BlockSpec((rq, KV_LORA), lambda s, pt_ref: (s, 0)),
                pl.BlockSpec((rq, QK_ROPE), lambda s, pt_ref: (s, 0)),
                pl.BlockSpec((1,) + knl.shape[1:], lambda s, pt_ref: (s, 0, 0)),
                pl.BlockSpec((1,) + knp.shape[1:], lambda s, pt_ref: (s, 0, 0)),
                pl.BlockSpec(memory_space=pl.ANY), pl.BlockSpec(memory_space=pl.ANY)]
    grid_spec = pltpu.PrefetchScalarGridSpec(
        num_scalar_prefetch=1, grid=(nseq,), in_specs=in_specs,
        out_specs=pl.BlockSpec((rq, KV_LORA), lambda s, pt_ref: (s, 0)),
        scratch_shapes=[pltpu.VMEM((nbuf, gp * page, KV_LORA), F32), pltpu.VMEM((nbuf, QK_ROPE, gp * page), F32),
                        pltpu.SemaphoreType.DMA((nbuf,)),
                        pltpu.VMEM((rq, LANES), F32), pltpu.VMEM((rq, LANES), F32),
                        pltpu.VMEM((rq, KV_LORA), F32)])
    return pl.pallas_call(
        functools.partial(_decode_kernel, layer=layer, npages=npages, gp=gp, nbuf=nbuf, n_new=n_new),
        grid_spec=grid_spec,
        out_shape=jax.ShapeDtypeStruct((nseq * rq, KV_LORA), BF16),
        compiler_params=_cparams("arbitrary"), name="decode_attn",
    )(pt, qlat, qpe, knl, knp, cache_ckv, cache_kpt)


def _mla_out_kernel(oa_ref, ob_ref, x_ref, wuv_ref, wo_ref, gpost_ref, y_ref, ocat_sc):
    row0 = 0
    for o_ref in (oa_ref, ob_ref):
        n = o_ref.shape[0]
        for hd in range(MLA_HEADS):
            oh = _dot(o_ref[:, hd * KV_LORA:(hd + 1) * KV_LORA].astype(BF16), wuv_ref[hd])
            ocat_sc[row0:row0 + n, hd * V_HEAD:(hd + 1) * V_HEAD] = oh
        row0 += n
    m = _dot(ocat_sc[...].astype(BF16), wo_ref[...])
    y_ref[...] = x_ref[...] + _rms(m, gpost_ref[...])


def _mla_out_small(oa, ob, x, wuv, wo, gpost):
    R, D = x.shape
    args = (oa, ob, x, wuv, wo, gpost)
    return pl.pallas_call(
        _mla_out_kernel, grid=(1,), in_specs=[_full(a.shape) for a in args],
        out_specs=_full((R, D)), out_shape=jax.ShapeDtypeStruct((R, D), F32),
        scratch_shapes=[pltpu.VMEM((R, MLA_HEADS * V_HEAD), F32)],
        compiler_params=_cparams("arbitrary"), name="mla_out_small",
    )(*args)


def _mlp_kernel(*refs, fc, attn):
    if attn:
        a_ref, x_ref, wo_ref, gmix_ref, gpre_ref, wup_ref, wdn_ref, gpost_ref, o_ref = refs
        x = x_ref[...] + _rms(_dot(a_ref[...], wo_ref[...]), gmix_ref[...])
    else:
        x_ref, gpre_ref, wup_ref, wdn_ref, gpost_ref, o_ref = refs
        x = x_ref[...]
    h = _rms(x, gpre_ref[...]).astype(BF16)
    dff = wup_ref.shape[1]
    acc = None
    for c in range(dff // fc):
        u = _dot(h, wup_ref[:, c * fc:(c + 1) * fc])
        u = jnp.square(jnp.maximum(u, 0.0)).astype(BF16)
        d = _dot(u, wdn_ref[c * fc:(c + 1) * fc, :])
        acc = d if acc is None else acc + d
    o_ref[...] = x + _rms(acc, gpost_ref[...])


def _mlp(x, gpre, wup, wdn, gpost, layer, tm, fc=1024, attn=None):
    N, D = x.shape
    const = lambda a: pl.BlockSpec(a.shape, lambda i: (0, 0), pipeline_mode=pl.Buffered(1))
    stacked = lambda a: pl.BlockSpec((None,) + a.shape[1:], lambda i: (layer, 0, 0),
                                     pipeline_mode=pl.Buffered(1))
    rows = lambda: pl.BlockSpec((tm, D), lambda i: (i, 0))
    args = [x, gpre, wup, wdn, gpost]
    in_specs = [rows(), const(gpre), stacked(wup), stacked(wdn), const(gpost)]
    if attn is not None:
        a, wo, gmix = attn
        args = [a, x, wo, gmix] + args[1:]
        in_specs = [rows(), rows(), const(wo), const(gmix)] + in_specs[1:]
    return pl.pallas_call(
        functools.partial(_mlp_kernel, fc=fc, attn=attn is not None), grid=(N // tm,),
        in_specs=in_specs, out_specs=rows(),
        out_shape=jax.ShapeDtypeStruct((N, D), F32),
        compiler_params=_cparams("arbitrary"), name="mlp",
    )(*args)


def _gla_project(x, gpre_ref, win_ref, wg_ref, bg_ref):
    hk = GLA_HEADS * GLA_DK
    hv = GLA_HEADS * GLA_DV
    h = _rms(x, gpre_ref[...]).astype(BF16)
    a = _dot(h, win_ref[...])
    gd = a[:, 2 * hk + 2 * hv:].astype(BF16)
    z = _dot(gd, wg_ref[...]) + bg_ref[...]
    la = (jnp.minimum(z, 0.0) - jnp.log(1.0 + jnp.exp(-jnp.abs(z)))) * (1.0 / GATE_TAU)
    return (a[:, :hk] * GLA_SCALE, a[:, hk:2 * hk], a[:, 2 * hk:2 * hk + hv].astype(BF16),
            a[:, 2 * hk + hv:2 * hk + 2 * hv], la)


def _gla_proj_kernel(x_ref, gpre_ref, win_ref, wg_ref, bg_ref, q_ref, k_ref, v_ref, r_ref, la_ref):
    q_ref[...], k_ref[...], v_ref[...], r_ref[...], la_ref[...] = _gla_project(
        x_ref[...], gpre_ref, win_ref, wg_ref, bg_ref)


def _gla_proj(x, gpre, win, wg, bg, tr):
    N, D = x.shape
    hk = GLA_HEADS * GLA_DK
    hv = GLA_HEADS * GLA_DV
    row = lambda w: pl.BlockSpec((tr, w), lambda i: (i, 0))
    return pl.pallas_call(
        _gla_proj_kernel, grid=(N // tr,),
        in_specs=[row(D), _full(gpre.shape), _full(win.shape), _full(wg.shape), _full(bg.shape)],
        out_specs=[row(hk), row(hk), row(hv), row(hv), row(hk)],
        out_shape=[jax.ShapeDtypeStruct((N, w), d)
                   for w, d in ((hk, F32), (hk, F32), (hv, BF16), (hv, F32), (hk, F32))],
        compiler_params=_cparams("arbitrary"), name="gla_proj",
    )(x, gpre, win, wg, bg)


def _gla_gate(o, r, gn):
    return _rms(o, gn) * (r / (1.0 + jnp.exp(-r)))


def _cumsum_rows(x, chunk):
    pos = lax.broadcasted_iota(jnp.int32, x.shape, 0) % chunk
    d = 1
    while d < chunk:
        x = x + jnp.where(pos >= d, pltpu.roll(x, d, axis=0), 0.0)
        d *= 2
    return x


def _gla_chunks(q, k, v, r, la, s_sc, gn):
    rows = q.shape[0]
    c_sz = GLA_CHUNK
    nc = rows // c_sz
    b = _cumsum_rows(la, c_sz)
    b_last = [b[(c + 1) * c_sz - 1:(c + 1) * c_sz, :] for c in range(nc)]
    q_in = (q * jnp.exp(b)).astype(BF16)
    k_in = (k * jnp.exp(-b)).astype(BF16)
    k_dec = jnp.concatenate([k[c * c_sz:(c + 1) * c_sz] * jnp.exp(b_last[c] - b[c * c_sz:(c + 1) * c_sz])
                             for c in range(nc)], axis=0).astype(BF16)
    dec = [jnp.exp(bl) for bl in b_last]
    ri = lax.broadcasted_iota(jnp.int32, (rows, rows), 0)
    ci = lax.broadcasted_iota(jnp.int32, (rows, rows), 1)
    tril = jnp.where(ci <= ri, ci, -1) >= (ri // c_sz) * c_sz
    outs = []
    for hd in range(GLA_HEADS):
        ks = slice(hd * GLA_DK, (hd + 1) * GLA_DK)
        vs = slice(hd * GLA_DV, (hd + 1) * GLA_DV)
        vh = v[:, vs]
        a = jnp.where(tril, _dot_nt(q_in[:, ks], k_in[:, ks]), 0.0).astype(BF16)
        o_intra = _dot(a, vh)
        kvs = [_dot_tn(k_dec[c * c_sz:(c + 1) * c_sz, ks], vh[c * c_sz:(c + 1) * c_sz]) for c in range(nc)]
        s = s_sc[hd]
        o_inter = []
        for c in range(nc):
            o_inter.append(_dot(q_in[c * c_sz:(c + 1) * c_sz, ks], s.astype(BF16)))
            dcol = jnp.transpose(jnp.broadcast_to(dec[c][:, ks], (GLA_DK, GLA_DK)))
            s = jnp.concatenate([dcol] * (GLA_DV // GLA_DK), axis=1) * s + kvs[c]
        s_sc[hd] = s
        outs.append(_gla_gate(o_intra + jnp.concatenate(o_inter, axis=0), r[:, vs], gn))
    return jnp.concatenate(outs, axis=1)


GLA_BLOCK_ROWS = 256


def _gla_prompt_kernel(x_ref, gpre_ref, win_ref, wg_ref, bg_ref, s0_ref, gn_ref, wo_ref, gpost_ref,
                       y_ref, sfin_ref, s_sc):
    @pl.when(pl.program_id(1) == 0)
    def _():
        s_sc[...] = s0_ref[...]

    blocks = [slice(r0, r0 + GLA_BLOCK_ROWS) for r0 in range(0, x_ref.shape[1], GLA_BLOCK_ROWS)]
    proj = [_gla_project(x_ref[0, sl], gpre_ref, win_ref, wg_ref, bg_ref) for sl in blocks]
    gn = gn_ref[...]
    for sl, (q, k, v, r, la) in zip(blocks, proj):
        o = _gla_chunks(q, k, v, r, la, s_sc, gn)
        m = _dot(o.astype(BF16), wo_ref[...])
        y_ref[0, sl] = x_ref[0, sl] + _rms(m, gpost_ref[...])

    @pl.when(pl.program_id(1) == pl.num_programs(1) - 1)
    def _():
        sfin_ref[0] = s_sc[...]


def _gla_prompt(x, gpre, win, wg, bg, s0, gn, wo, gpost, rows):
    B, T, D = x.shape
    blk = pl.BlockSpec((1, rows, D), lambda b, i: (b, i, 0))
    const = lambda a: pl.BlockSpec(a.shape, lambda b, i: (0,) * a.ndim, pipeline_mode=pl.Buffered(1))
    return pl.pallas_call(
        _gla_prompt_kernel, grid=(B, T // rows),
        in_specs=[blk] + [const(a) for a in (gpre, win, wg, bg, s0, gn, wo, gpost)],
        out_specs=[blk, pl.BlockSpec((1,) + s0.shape, lambda b, i: (b, 0, 0, 0))],
        out_shape=[jax.ShapeDtypeStruct((B, T, D), F32), jax.ShapeDtypeStruct((B,) + s0.shape, F32)],
        scratch_shapes=[pltpu.VMEM(s0.shape, F32)],
        compiler_params=_cparams("arbitrary", "arbitrary"), name="gla_prompt",
    )(x, gpre, win, wg, bg, s0, gn, wo, gpost)


GLA_SHORT_ROWS = 16


def _gla_tokens_kernel(q_ref, k_ref, la_ref, v_ref, s0_ref, o_ref, sfin_ref):
    nblk, rows, _ = q_ref.shape
    zk = jnp.zeros((LANES - rows, GLA_DK), BF16)
    zv = jnp.zeros((LANES - rows, GLA_DV), BF16)
    tril = (lax.broadcasted_iota(jnp.int32, (rows, LANES), 1) <= lax.broadcasted_iota(jnp.int32, (rows, LANES), 0))
    for n in range(nblk):
        b = _cumsum_rows(la_ref[n], rows)
        b_last = b[rows - 1:rows, :]
        k = k_ref[n]
        q_in = (q_ref[n] * jnp.exp(b)).astype(BF16)
        k_in = (k * jnp.exp(-b)).astype(BF16)
        k_dec = (k * jnp.exp(b_last - b)).astype(BF16)
        dec = jnp.exp(b_last)
        for hd in range(GLA_HEADS):
            ks = slice(hd * GLA_DK, (hd + 1) * GLA_DK)
            vs = slice(hd * GLA_DV, (hd + 1) * GLA_DV)
            v = jnp.concatenate([v_ref[n, :, vs].astype(BF16), zv], axis=0)
            a = jnp.where(tril, _dot_nt(q_in[:, ks], jnp.concatenate([k_in[:, ks], zk], axis=0)), 0.0)
            s = s0_ref[n, hd]
            o_ref[n, :, vs] = _dot(a.astype(BF16), v) + _dot(q_in[:, ks], s.astype(BF16))
            dcol = jnp.transpose(jnp.broadcast_to(dec[:, ks], (GLA_DK, GLA_DK)))
            sfin_ref[n, hd] = (jnp.concatenate([dcol] * (GLA_DV // GLA_DK), axis=1) * s
                               + _dot_tn(jnp.concatenate([k_dec[:, ks], zk], axis=0), v))


def _gla_tokens(q, k, la, v, s0, nblk):
    nseq, rows, hk = q.shape
    hv = v.shape[-1]
    col = pl.BlockSpec((nblk, rows, hk), lambda s: (s, 0, 0))
    val = pl.BlockSpec((nblk, rows, hv), lambda s: (s, 0, 0))
    st = pl.BlockSpec((nblk,) + s0.shape[1:], lambda s: (s, 0, 0, 0))
    return pl.pallas_call(
        _gla_tokens_kernel, grid=(nseq // nblk,),
        in_specs=[col, col, col, val, st], out_specs=[val, st],
        out_shape=[jax.ShapeDtypeStruct((nseq, rows, hv), F32), jax.ShapeDtypeStruct(s0.shape, F32)],
        compiler_params=_cparams("arbitrary"), name="gla_tokens",
    )(q, k, la, v, s0)


def _gla_out_kernel(o_ref, r_ref, x_ref, gn_ref, wo_ref, gpost_ref, y_ref, ocat_sc):
    gn = gn_ref[...]
    for hd in range(GLA_HEADS):
        vs = slice(hd * GLA_DV, (hd + 1) * GLA_DV)
        ocat_sc[:, vs] = _gla_gate(o_ref[:, vs], r_ref[:, vs], gn)
    m = _dot(ocat_sc[...].astype(BF16), wo_ref[...])
    y_ref[...] = x_ref[...] + _rms(m, gpost_ref[...])


def _gla_out_small(o, r, x, gn, wo, gpost):
    R, D = x.shape
    return pl.pallas_call(
        _gla_out_kernel, grid=(1,),
        in_specs=[_full(o.shape), _full(r.shape), _full(x.shape), _full(gn.shape), _full(wo.shape),
                  _full(gpost.shape)],
        out_specs=_full((R, D)), out_shape=jax.ShapeDtypeStruct((R, D), F32),
        scratch_shapes=[pltpu.VMEM(o.shape, F32)],
        compiler_params=_cparams("arbitrary"), name="gla_out_small",
    )(o, r, x, gn, wo, gpost)


def _rope_cos_sin(pos):
    half = QK_ROPE // 2
    inv = ROPE_BASE ** (-np.arange(half, dtype=np.float64) / half)
    ang = np.asarray(pos, np.float64)[:, None] * inv[None, :]
    return np.cos(ang), np.sin(ang)


def _rope_tables(pos):
    c, s = _rope_cos_sin(pos)
    reps = LANES // QK_ROPE
    return (jnp.asarray(np.tile(np.concatenate([c, c], axis=1), (1, reps)), F32),
            jnp.asarray(np.tile(np.concatenate([-s, s], axis=1), (1, reps)), F32))


def _rope_tables_head(pos):
    c, s = _rope_cos_sin(pos)
    z = lambda n: np.zeros((len(pos), n))
    tail = HEAD_W - QK_NOPE - QK_ROPE
    return (jnp.asarray(np.concatenate([z(QK_NOPE), c, c, z(tail)], axis=1), F32),
            jnp.asarray(np.concatenate([z(QK_NOPE), -s, s, z(tail)], axis=1), F32))


def _swap_halves(w):
    half = QK_ROPE // 2
    return jnp.concatenate([w[..., half:], w[..., :half]], axis=-1)


def kernel(x_prompt, x_sample, cache_ckv, cache_kpe, state_gla, page_table, meta_tokens, norm_pre_mix, norm_post_mix, norm_pre_mlp, norm_post_mlp, mla_w_in, mla_q_norm, mla_w_uq, mla_kv_norm, mla_w_uk, mla_w_uv, mla_w_o, gla_w_in, gla_w_gate, gla_b_gate, gla_norm, gla_w_o, mlp_w_up, mlp_w_down):
    B, T, D = x_prompt.shape
    DB, S, _ = x_sample.shape
    H = MLA_HEADS
    n_s = DB * S
    past_len = page_table.shape[1] * cache_ckv.shape[2]
    row = lambda a: a.reshape(1, -1)

    w_in = mla_w_in[0]
    o = Q_LORA + KV_LORA
    w_kpe = w_in[:, o:]
    zpad = jnp.zeros((D, LANES - QK_ROPE), F32)
    w_in_p = jnp.concatenate([w_in[:, :o], w_kpe, zpad, _swap_halves(w_kpe), zpad], axis=1).astype(BF16)
    w_uq = mla_w_uq[0].reshape(Q_LORA, H, QK_NOPE + QK_ROPE)
    w_qn = jnp.pad(w_uq[:, :, :QK_NOPE], ((0, 0), (0, 0), (0, LANES - QK_NOPE))).reshape(Q_LORA, H * LANES)
    w_ukt = jnp.pad(jnp.transpose(mla_w_uk[0], (1, 2, 0)), ((0, 0), (0, LANES - QK_NOPE), (0, 0)))
    w_qpe = w_uq[:, :, QK_NOPE:]
    w_qp = w_qpe.reshape(Q_LORA, H * QK_ROPE).astype(BF16)
    w_qps = _swap_halves(w_qpe).reshape(Q_LORA, H * QK_ROPE).astype(BF16)
    proj_w = (row(norm_pre_mix[0]), w_in_p, row(mla_q_norm[0]), w_qn.astype(BF16), w_ukt.astype(BF16),
              w_qp, w_qps, row(mla_kv_norm[0]))
    w_uv = jnp.transpose(mla_w_uv[0], (1, 0, 2)).astype(BF16)
    w_o = mla_w_o[0].astype(BF16)
    g_post0 = row(norm_post_mix[0])

    x_small = jnp.concatenate([x_sample.reshape(n_s, D), meta_tokens], axis=0)
    pos_small = np.concatenate([past_len + np.tile(np.arange(S), DB), np.arange(N_META)])
    cos_s, sin_s = _rope_tables(pos_small)
    qlat_s, qpe_s, ckv_s, kpe_s, klat_s, kpeb_s = _mla_proj_small(x_small, cos_s, sin_s, proj_w)

    npad = LANES
    kml = jnp.pad(klat_s[n_s:], ((0, npad - N_META), (0, 0)))
    kmp = jnp.pad(kpeb_s[n_s:], ((0, npad - N_META), (0, 0)))
    o_meta = _meta_attn(qlat_s[n_s:].reshape(N_META * H, KV_LORA), qpe_s[n_s:].reshape(N_META * H, QK_ROPE),
                        kml, kmp)
    knl = jnp.pad(klat_s[:n_s].reshape(DB, S, KV_LORA), ((0, 0), (0, npad - S), (0, 0)))
    knp = jnp.pad(kpeb_s[:n_s].reshape(DB, S, QK_ROPE), ((0, 0), (0, npad - S), (0, 0)))
    o_samp = _decode_attn(page_table, qlat_s[:n_s].reshape(n_s * H, KV_LORA),
                          qpe_s[:n_s].reshape(n_s * H, QK_ROPE), cache_ckv, jnp.swapaxes(cache_kpe, 2, 3),
                          0, knl, knp)
    xs1 = _mla_out_small(o_samp.reshape(n_s, H * KV_LORA), o_meta.reshape(N_META, H * KV_LORA), x_small,
                         w_uv, w_o, g_post0)

    zl = lambda n: jnp.zeros((D, n), F32)
    pad_r = HEAD_W - QK_NOPE - QK_ROPE
    w_in_m = jnp.concatenate([w_in[:, :o], zl(QK_NOPE), w_kpe, zl(pad_r),
                              zl(QK_NOPE), _swap_halves(w_kpe), zl(pad_r)], axis=1).astype(BF16)
    wq = jnp.pad(w_uq, ((0, 0), (0, 0), (0, pad_r))).reshape(Q_LORA, H * HEAD_W).astype(BF16)
    wqs = jnp.pad(_swap_halves(w_qpe), ((0, 0), (0, 0), (QK_NOPE, pad_r))).reshape(Q_LORA, H * HEAD_W).astype(BF16)
    wk = jnp.pad(mla_w_uk[0], ((0, 0), (0, 0), (0, HEAD_W - QK_NOPE))).reshape(KV_LORA, H * HEAD_W).astype(BF16)
    wv_pair = mla_w_uv[0].reshape(KV_LORA, H // 2, 2, V_HEAD)
    wv = jnp.stack([jnp.pad(wv_pair[:, :, r], ((0, 0), (0, 0), (_val_lane0(r), HEAD_W - V_HEAD - _val_lane0(r))))
                    for r in range(2)], axis=2).reshape(KV_LORA, H * HEAD_W).astype(BF16)
    mha_w = (row(norm_pre_mix[0]), w_in_m, row(mla_q_norm[0]), wq, wqs, row(mla_kv_norm[0]), wk, wv)
    cosk, sink = _rope_tables_head(N_META + np.arange(T))
    q_p, k_p, v_p, ckv_p, kpe_p = _mha_proj(x_prompt, cosk, sink, mha_w, tr=ROW_TILE)
    cosm, sinm = _rope_tables_head(np.arange(N_META))
    _, k_m, v_m, _, _ = _mha_proj(meta_tokens[None], cosm, sinm, mha_w, tr=N_META)
    mpad = ((0, 0), (0, 0), (0, LANES - N_META), (0, 0))
    attn_p = _mha_flash(q_p, k_p, v_p, jnp.pad(k_m, mpad), jnp.pad(v_m, mpad), tq=ROW_TILE)

    wup_b, wdn_b = mlp_w_up.astype(BF16), mlp_w_down.astype(BF16)
    mlp0 = (row(norm_pre_mlp[0]), wup_b, wdn_b, row(norm_post_mlp[0]), 0)
    xp2 = _mlp(x_prompt.reshape(B * T, D), *mlp0, tm=ROW_TILE, attn=(attn_p.reshape(B * T, D), w_o, g_post0))
    xs2 = _mlp(xs1, *mlp0, tm=xs1.shape[0])

    hk = GLA_HEADS * GLA_DK
    hv = GLA_HEADS * GLA_DV
    gw = gla_w_in[0]
    gw_p = jnp.concatenate([gw, jnp.zeros((D, LANES - GATE_RANK), F32)], axis=1).astype(BF16)
    wg_p = jnp.concatenate([gla_w_gate[0], jnp.zeros((LANES - GATE_RANK, hk), F32)], axis=0).astype(BF16)
    gla_pw = (row(norm_pre_mix[1]), gw_p, wg_p, row(gla_b_gate[0]))
    gn = row(gla_norm[0])
    gwo = gla_w_o[0].astype(BF16)
    g_post1 = row(norm_post_mix[1])

    q_s, k_s, v_s, r_s, la_s = _gla_proj(xs2, *gla_pw, tr=xs2.shape[0])
    def seqs(a, n, l):
        return jnp.pad(a.reshape(n, l, a.shape[-1]), ((0, 0), (0, GLA_SHORT_ROWS - l), (0, 0)))

    o_gs, s_gs = _gla_tokens(*(seqs(a[:n_s], DB, S) for a in (q_s, k_s, la_s, v_s)), state_gla[0], nblk=8)
    zero_state = jnp.zeros((1, GLA_HEADS, GLA_DK, GLA_DV), F32)
    o_gm, s_gm = _gla_tokens(*(seqs(a[n_s:], 1, N_META) for a in (q_s, k_s, la_s, v_s)), zero_state, nblk=1)
    o_gsmall = jnp.concatenate([o_gs[:, :S].reshape(n_s, hv), o_gm[:, :N_META].reshape(N_META, hv)], axis=0)
    xs3 = _gla_out_small(o_gsmall, r_s, xs2, gn, gwo, g_post1)

    xp3, s_gp = _gla_prompt(xp2.reshape(B, T, D), *gla_pw, s_gm[0], gn, gwo, g_post1, rows=ROW_TILE)

    mlp1 = (row(norm_pre_mlp[1]), wup_b, wdn_b, row(norm_post_mlp[1]), 1)
    y_prompt = _mlp(xp3.reshape(B * T, D), *mlp1, tm=ROW_TILE).reshape(B, T, D)
    xs4 = _mlp(xs3, *mlp1, tm=xs3.shape[0])
    y_sample = xs4[:n_s].reshape(DB, S, D)

    bmeta = lambda a: jnp.broadcast_to(a[None], (B,) + a.shape)
    new_ckv_prompt = jnp.concatenate([bmeta(ckv_s[n_s:]), ckv_p], axis=1)[None]
    new_kpe_prompt = jnp.concatenate([bmeta(kpe_s[n_s:]), kpe_p], axis=1)[None]
    new_ckv_sample = ckv_s[:n_s].reshape(1, DB, S, KV_LORA)
    new_kpe_sample = kpe_s[:n_s].reshape(1, DB, S, QK_ROPE)
    return (y_prompt, y_sample, new_ckv_prompt, new_kpe_prompt, new_ckv_sample, new_kpe_sample,
            s_gp[None], s_gs[None])
```
